```python
import jax, jax.numpy as jnp
from jax import lax
import numpy as np

D_MODEL = 1024
BATCH = 2
SEQ = 8192
DEPTH = 1

CHUNK = 64
SUB_CHUNK = 16
N_SUB = CHUNK // SUB_CHUNK
N_MEM = 256
M_HEADS = 4
M_DK = 128
M_DV = 128
M_QK = M_HEADS * M_DK
M_WIDTH = M_HEADS * M_DV
CONV_W = 4
G_HEADS = 4
G_DK = 64
G_DV = 128
G_QK = G_HEADS * G_DK
G_WIDTH = G_HEADS * G_DV
G_RANK = 16
G_TAU = 16.0
MIX_WIDTH = M_WIDTH + G_WIDTH
X_HEADS = 4
X_DH = D_MODEL // X_HEADS
D_FF = 4 * D_MODEL
ALPHA = (2.0 * DEPTH) ** 0.25
BETA = (8.0 * DEPTH) ** -0.25
LN_EPS = 1e-5
IN_SIZES = (M_QK, M_QK, M_WIDTH, M_WIDTH, M_HEADS, M_HEADS, G_QK, G_QK, G_WIDTH, G_WIDTH, G_RANK)
IN_COLS = sum(IN_SIZES)
IN_SPLITS = tuple(int(s) for s in np.cumsum(IN_SIZES)[:-1])

kernel_name = "hymba_mlstm_gla_deepnorm_memxattn"


def layer_norm(x, g, b):
    xf = x.astype(jnp.float32)
    mu = jnp.mean(xf, -1, keepdims=True)
    var = jnp.mean(jnp.square(xf - mu), -1, keepdims=True)
    return ((xf - mu) * lax.rsqrt(var + LN_EPS) * g + b).astype(x.dtype)


def head_layer_norm(h, g):
    mu = jnp.mean(h, -1, keepdims=True)
    var = jnp.mean(jnp.square(h - mu), -1, keepdims=True)
    hn = (h - mu) * lax.rsqrt(var + LN_EPS)
    return hn.reshape(h.shape[0], h.shape[1], -1) * g


def head_rms_norm(h, g):
    hn = h * lax.rsqrt(jnp.mean(jnp.square(h), -1, keepdims=True) + LN_EPS)
    return hn.reshape(h.shape[0], h.shape[1], -1) * g


def causal_depthwise_conv(u, w, b):
    T = u.shape[1]
    up = jnp.pad(u, ((0, 0), (CONV_W - 1, 0), (0, 0)))
    return sum(up[:, j:j + T] * w[j] for j in range(CONV_W)) + b


def to_heads(u, n_heads):
    B_, T, C = u.shape
    return u.reshape(B_, T, n_heads, C // n_heads).transpose(0, 2, 1, 3)


def mlstm_chunkwise(q, k, v, i_pre, f_pre):
    B_, H, T, DK = q.shape
    DV = v.shape[-1]
    NC = T // CHUNK
    q = q.reshape(B_, H, NC, CHUNK, DK)
    k = k.reshape(B_, H, NC, CHUNK, DK) * (DK ** -0.5)
    v = v.reshape(B_, H, NC, CHUNK, DV)
    log_f = jax.nn.log_sigmoid(f_pre).reshape(B_, H, NC, CHUNK)
    log_i = i_pre.reshape(B_, H, NC, CHUNK)
    b = jnp.cumsum(log_f, -1)
    g = b[..., -1]
    a = g[..., None] - b + log_i
    a_max = jnp.max(a, -1)
    w = jnp.exp(a - a_max[..., None])
    U = jnp.einsum('bhcl,bhcld,bhcle->bhcde', w, k, v)
    u = jnp.einsum('bhcl,bhcld->bhcd', w, k)

    def step(carry, inp):
        C, n, m = carry
        U_c, u_c, g_c, am_c = inp
        m_new = jnp.maximum(g_c + m, am_c)
        dec = jnp.exp(g_c + m - m_new)
        inj = jnp.exp(am_c - m_new)
        C_new = dec[..., None, None] * C + inj[..., None, None] * U_c
        n_new = dec[..., None] * n + inj[..., None] * u_c
        return (C_new, n_new, m_new), (C, n, m)

    init = (jnp.zeros((B_, H, DK, DV), jnp.float32), jnp.zeros((B_, H, DK), jnp.float32),
            jnp.zeros((B_, H), jnp.float32))
    xs = (jnp.moveaxis(U, 2, 0), jnp.moveaxis(u, 2, 0), jnp.moveaxis(g, 2, 0), jnp.moveaxis(a_max, 2, 0))
    _, (C_prev, n_prev, m_prev) = lax.scan(step, init, xs)
    C_prev = jnp.moveaxis(C_prev, 0, 2)
    n_prev = jnp.moveaxis(n_prev, 0, 2)
    m_prev = jnp.moveaxis(m_prev, 0, 2)

    causal = jnp.tril(jnp.ones((CHUNK, CHUNK), bool))
    D = b[..., :, None] - b[..., None, :] + log_i[..., None, :]
    D = jnp.where(causal, D, -jnp.inf)
    inter = b + m_prev[..., None]
    m_t = jnp.maximum(inter, jnp.max(D, -1))
    Wts = jnp.exp(D - m_t[..., None])
    sc = jnp.exp(inter - m_t)
    S = jnp.einsum('bhcld,bhcsd->bhcls', q, k) * Wts
    num = (sc[..., None] * jnp.einsum('bhcld,bhcde->bhcle', q, C_prev)
           + jnp.einsum('bhcls,bhcse->bhcle', S, v))
    den = sc * jnp.einsum('bhcld,bhcd->bhcl', q, n_prev) + jnp.sum(S, -1)
    h = num / jnp.maximum(jnp.abs(den), jnp.exp(-m_t))[..., None]
    return h.reshape(B_, H, T, DV)


def gla_chunked(q, k, v, log_a):
    B_, H, T, DK = q.shape
    DV = v.shape[-1]
    NC = T // CHUNK
    shp = (B_, H, NC, N_SUB, SUB_CHUNK)
    q = q.reshape(*shp, DK) * (DK ** -0.5)
    k = k.reshape(*shp, DK)
    la = log_a.reshape(*shp, DK)
    bc = jnp.cumsum(la.reshape(B_, H, NC, CHUNK, DK), axis=3).reshape(*shp, DK)
    b_start = bc[..., :, :1, :] - la[..., :, :1, :]
    b_end = bc[..., :, -1, :]
    causal_s = jnp.tril(jnp.ones((SUB_CHUNK, SUB_CHUNK), bool))
    expo = bc[..., :, None, :] - bc[..., None, :, :]
    expo = jnp.where(causal_s[..., None], expo, -jnp.inf)
    a_diag = jnp.sum(q[..., :, None, :] * k[..., None, :, :] * jnp.exp(expo), -1)
    q_hat = q * jnp.exp(bc - b_start)
    k_hat = k * jnp.exp(b_end[..., :, None, :] - bc)
    mid = b_start[..., :, 0, :][..., :, None, :] - b_end[..., None, :, :]
    earlier = jnp.tril(jnp.ones((N_SUB, N_SUB), bool), -1)
    mid = jnp.exp(jnp.where(earlier[..., None], mid, -jnp.inf))
    a_off = jnp.einsum('bhcjtd,bhcjid,bhcisd->bhcjtis', q_hat, mid, k_hat)
    eye = jnp.eye(N_SUB, dtype=a_off.dtype)[:, None, :, None]
    A = (a_off + eye * a_diag[..., :, :, None, :]).reshape(B_, H, NC, CHUNK, CHUNK)
    vf = v.reshape(B_, H, NC, CHUNK, DV)
    o_intra = jnp.einsum('bhcts,bhcse->bhcte', A, vf)
    bcf = bc.reshape(B_, H, NC, CHUNK, DK)
    qf = q.reshape(B_, H, NC, CHUNK, DK)
    kf = k.reshape(B_, H, NC, CHUNK, DK)
    g = bcf[..., -1, :]
    U = jnp.einsum('bhcld,bhcle->bhcde', kf * jnp.exp(g[..., None, :] - bcf), vf)

    def step(S, inp):
        U_c, g_c = inp
        return jnp.exp(g_c)[..., None] * S + U_c, S

    _, S_prev = lax.scan(step, jnp.zeros((B_, H, DK, DV), jnp.float32),
                         (jnp.moveaxis(U, 2, 0), jnp.moveaxis(g, 2, 0)))
    S_prev = jnp.moveaxis(S_prev, 0, 2)
    o_inter = jnp.einsum('bhcld,bhcde->bhcle', qf * jnp.exp(bcf), S_prev)
    return (o_intra + o_inter).reshape(B_, H, T, DV)


def hybrid_mixer(x, w_in, conv_w, conv_b, m_i_bias, m_f_bias, m_norm_g, g_lr_w, g_lr_b, g_norm_g, w_out):
    proj = (x @ w_in).astype(jnp.float32)
    mq, mk, mv, mo, mi, mf, gq, gk, gv, gg, glr = jnp.split(proj, IN_SPLITS, axis=-1)
    qk = jax.nn.silu(causal_depthwise_conv(jnp.concatenate([mq, mk], -1),
                                           conv_w.astype(jnp.float32), conv_b.astype(jnp.float32)))
    mq, mk = jnp.split(qk, 2, axis=-1)
    i_pre = (mi + m_i_bias.astype(jnp.float32)).transpose(0, 2, 1)
    f_pre = (mf + m_f_bias.astype(jnp.float32)).transpose(0, 2, 1)
    hm = mlstm_chunkwise(to_heads(mq, M_HEADS), to_heads(mk, M_HEADS), to_heads(mv, M_HEADS), i_pre, f_pre)
    m_out = jax.nn.sigmoid(mo) * head_layer_norm(hm.transpose(0, 2, 1, 3), m_norm_g.astype(jnp.float32))
    log_a = jax.nn.log_sigmoid(glr @ g_lr_w.astype(jnp.float32) + g_lr_b.astype(jnp.float32)) / G_TAU
    hg = gla_chunked(to_heads(gq, G_HEADS), to_heads(gk, G_HEADS), to_heads(gv, G_HEADS), to_heads(log_a, G_HEADS))
    g_out = jax.nn.silu(gg) * head_rms_norm(hg.transpose(0, 2, 1, 3), g_norm_g.astype(jnp.float32))
    y = jnp.concatenate([m_out, g_out], -1).astype(x.dtype)
    return y @ w_out


def memory_cross_attention(x, mem, w_q, w_k, w_v, w_o):
    B_, T, _ = x.shape
    q = (x @ w_q).reshape(B_, T, X_HEADS, X_DH)
    k = (mem @ w_k).reshape(B_, N_MEM, X_HEADS, X_DH)
    v = (mem @ w_v).reshape(B_, N_MEM, X_HEADS, X_DH)
    s = jnp.einsum('bthd,bmhd->bhtm', q, k).astype(jnp.float32) * (X_DH ** -0.5)
    p = jax.nn.softmax(s, axis=-1).astype(v.dtype)
    o = jnp.einsum('bhtm,bmhd->bthd', p, v).reshape(B_, T, D_MODEL)
    return o @ w_o


def squared_relu_mlp(x, w_ff1, w_ff2):
    return jnp.square(jax.nn.relu(x @ w_ff1)) @ w_ff2


def setup_inputs(seed: int = 0) -> dict:
    key = jax.random.key(seed)
    ks = jax.random.split(key, 28)
    L = DEPTH

    def nrm(k, shape, scale):
        return jax.random.normal(k, shape, jnp.float32) * scale

    return {
        "x": nrm(ks[0], (BATCH, SEQ, D_MODEL), 1.0),
        "mem": nrm(ks[1], (BATCH, N_MEM, D_MODEL), 1.0),
        "ln_in_g": 1.0 + nrm(ks[2], (D_MODEL,), 0.02),
        "ln_in_b": nrm(ks[3], (D_MODEL,), 0.02),
        "w_in": nrm(ks[4], (L, D_MODEL, IN_COLS), D_MODEL ** -0.5),
        "conv_w": nrm(ks[5], (L, CONV_W, 2 * M_QK), CONV_W ** -0.5),
        "conv_b": nrm(ks[6], (L, 2 * M_QK), 0.02),
        "m_i_bias": nrm(ks[7], (L, M_HEADS), 0.1),
        "m_f_bias": jnp.linspace(3.0, 6.0, M_HEADS, dtype=jnp.float32)[None] + nrm(ks[8], (L, M_HEADS), 0.1),
        "m_norm_g": 1.0 + nrm(ks[9], (L, M_WIDTH), 0.02),
        "g_lr_w": nrm(ks[10], (L, G_RANK, G_QK), G_RANK ** -0.5),
        "g_lr_b": nrm(ks[11], (L, G_QK), 0.02),
        "g_norm_g": 1.0 + nrm(ks[12], (L, G_WIDTH), 0.02),
        "w_out": nrm(ks[13], (L, MIX_WIDTH, D_MODEL), MIX_WIDTH ** -0.5 * BETA),
        "ln1_g": 1.0 + nrm(ks[14], (L, D_MODEL), 0.02),
        "ln1_b": nrm(ks[15], (L, D_MODEL), 0.02),
        "x_wq": nrm(ks[16], (L, D_MODEL, D_MODEL), D_MODEL ** -0.5),
        "x_wk": nrm(ks[17], (L, D_MODEL, D_MODEL), D_MODEL ** -0.5),
        "x_wv": nrm(ks[18], (L, D_MODEL, D_MODEL), D_MODEL ** -0.5 * BETA),
        "x_wo": nrm(ks[19], (L, D_MODEL, D_MODEL), D_MODEL ** -0.5 * BETA),
        "ln2_g": 1.0 + nrm(ks[20], (L, D_MODEL), 0.02),
        "ln2_b": nrm(ks[21], (L, D_MODEL), 0.02),
        "w_ff1": nrm(ks[22], (L, D_MODEL, D_FF), D_MODEL ** -0.5 * BETA),
        "w_ff2": nrm(ks[23], (L, D_FF, D_MODEL), D_FF ** -0.5 * BETA),
        "ln3_g": 1.0 + nrm(ks[24], (L, D_MODEL), 0.02),
        "ln3_b": nrm(ks[25], (L, D_MODEL), 0.02),
    }


def reference(x, mem, ln_in_g, ln_in_b, w_in, conv_w, conv_b, m_i_bias, m_f_bias, m_norm_g,
              g_lr_w, g_lr_b, g_norm_g, w_out, ln1_g, ln1_b, x_wq, x_wk, x_wv, x_wo,
              ln2_g, ln2_b, w_ff1, w_ff2, ln3_g, ln3_b):
    h = layer_norm(x, ln_in_g, ln_in_b)
    for l in range(DEPTH):
        mix = hybrid_mixer(h, w_in[l], conv_w[l], conv_b[l], m_i_bias[l], m_f_bias[l], m_norm_g[l],
                           g_lr_w[l], g_lr_b[l], g_norm_g[l], w_out[l])
        h = layer_norm(ALPHA * h + mix, ln1_g[l], ln1_b[l])
        xa = memory_cross_attention(h, mem, x_wq[l], x_wk[l], x_wv[l], x_wo[l])
        h = layer_norm(ALPHA * h + xa, ln2_g[l], ln2_b[l])
        ff = squared_relu_mlp(h, w_ff1[l], w_ff2[l])
        h = layer_norm(ALPHA * h + ff, ln3_g[l], ln3_b[l])
    return h
```

```python
import functools

import numpy as np
import jax
import jax.numpy as jnp
from jax import lax
from jax.experimental import pallas as pl
from jax.experimental.pallas import tpu as pltpu

D_MODEL = 1024
CHUNK = 64
SUB = 16
WIN = 2 * CHUNK
N_MEM = 256
M_HEADS, M_DK, M_DV = 4, 128, 128
G_HEADS, G_DK, G_DV = 4, 64, 128
G_RANK = 16
G_TAU = 16.0
X_HEADS = 4
X_DH = D_MODEL // X_HEADS
D_FF = 4 * D_MODEL
FF_BLK = 512
DEPTH = 1
ALPHA = (2.0 * DEPTH) ** 0.25
LN_EPS = 1e-5
LANE = 128

C_MQ, C_MK, C_MV, C_MO = 0, 512, 1024, 1536
C_GQ, C_GK, C_GV, C_GG = 2048, 2304, 2560, 3072
C_GATE = 3584
IN_COLS_R = C_GATE + LANE
O_MQ, O_MK, O_MV, O_MO, O_MI, O_MF = 0, 512, 1024, 1536, 2048, 2052
O_GQ, O_GK, O_GV, O_GG, O_GLR = 2056, 2312, 2568, 3080, 3592

VMEM_LIMIT = 56 * 1024 * 1024

_BF = jnp.bfloat16
_F32 = jnp.float32


def _dot(a, b):
    return jnp.dot(a.astype(_BF), b.astype(_BF), preferred_element_type=_F32)


def _dot_nt(a, b):
    return lax.dot_general(a.astype(_BF), b.astype(_BF), (((1,), (1,)), ((), ())),
                           preferred_element_type=_F32)


def _dot_tn(a, b):
    return lax.dot_general(a.astype(_BF), b.astype(_BF), (((0,), (0,)), ((), ())),
                           preferred_element_type=_F32)


def _split3(x):
    hi = x.astype(_BF)
    r1 = x - hi.astype(_F32)
    mid = r1.astype(_BF)
    lo = (r1 - mid.astype(_F32)).astype(_BF)
    return jnp.concatenate([hi, mid, lo], axis=1)


def _merge3(y, n):
    return y[:, 0:n] + y[:, n:2 * n] + y[:, 2 * n:3 * n]


def _layer_norm(x, g, b):
    mu = jnp.mean(x, axis=-1, keepdims=True)
    xc = x - mu
    var = jnp.mean(xc * xc, axis=-1, keepdims=True)
    return xc * lax.rsqrt(var + LN_EPS) * g + b


def _log_sigmoid(x):
    return -(jnp.maximum(-x, 0.0) + jnp.log1p(jnp.exp(-jnp.abs(x))))


def _sigmoid(x):
    return 1.0 / (1.0 + jnp.exp(-x))


def _mixer_kernel(x_ref, lnin_g, lnin_b, w_in, conv_w, conv_b, gate_bias, m_norm_g, glw, glb, g_norm_g,
                  w_out, ln1_g, ln1_b, tri64, tri16, ones16, gsel, rsel,
                  o_ref, uext, c_st, n_st, m_st, s_st, *, tt):
    i = pl.program_id(1)

    @pl.when(i == 0)
    def _():
        uext[0:8, :] = jnp.zeros((8, 2 * 512), _F32)
        c_st[...] = jnp.zeros(c_st.shape, _F32)
        n_st[...] = jnp.zeros(n_st.shape, _F32)
        m_st[...] = jnp.zeros(m_st.shape, _F32)
        s_st[...] = jnp.zeros(s_st.shape, _F32)

    h0 = _layer_norm(x_ref[0], lnin_g[...], lnin_b[...])
    proj = _dot(h0, w_in[...])

    uext[8:8 + tt, :] = proj[:, C_MQ:C_MV]
    cw = conv_w[...]
    conv = (uext[5:5 + tt, :] * cw[0:1, :] + uext[6:6 + tt, :] * cw[1:2, :]
            + uext[7:7 + tt, :] * cw[2:3, :] + uext[8:8 + tt, :] * cw[3:4, :]) + conv_b[...]
    uext[0:8, :] = uext[tt:tt + 8, :]
    qk = conv * _sigmoid(conv)
    mq = qk[:, 0:512]
    mk = qk[:, 512:1024] * (M_DK ** -0.5)
    mv = proj[:, C_MV:C_MO]
    mo = proj[:, C_MO:C_GQ]

    gb = proj[:, C_GATE:C_GATE + LANE] + gate_bias[...]
    lane = lax.broadcasted_iota(jnp.int32, (tt, LANE), 1)
    lf = _log_sigmoid(gb)
    bcum = _merge3(_dot(tri64[...], _split3(lf)), LANE)
    vcol = jnp.where((lane >= M_HEADS) & (lane < 2 * M_HEADS), bcum, gb)
    vrow = vcol.T

    rr = lax.broadcasted_iota(jnp.int32, (WIN, WIN), 0)
    cc = lax.broadcasted_iota(jnp.int32, (WIN, WIN), 1)
    valid = (cc <= rr) & ((cc >= CHUNK) == (rr >= CHUNK))
    first = lax.broadcasted_iota(jnp.int32, (WIN, 1), 0) < CHUNK
    mg = m_norm_g[...]
    m_outs = []
    for h in range(M_HEADS):
        hs = slice(h * M_DK, (h + 1) * M_DK)
        c_cur = c_st[h]
        n_cur = n_st[h]
        m_cur = m_st[h]
        win_outs = []
        for w in range(tt // WIN):
            rs = slice(w * WIN, (w + 1) * WIN)
            q, k, v = mq[rs, hs], mk[rs, hs], mv[rs, hs]
            li_col, b_col = vcol[rs, h:h + 1], vcol[rs, M_HEADS + h:M_HEADS + h + 1]
            li_row, b_row = vrow[h:h + 1, rs], vrow[M_HEADS + h:M_HEADS + h + 1, rs]
            dm = jnp.where(valid, b_col - b_row + li_row, -jnp.inf)
            md = jnp.max(dm, axis=1, keepdims=True)
            g0, g1 = b_col[CHUNK - 1:CHUNK, :], b_col[WIN - 1:WIN, :]
            a_col = jnp.where(first, g0, g1) - b_col + li_col
            am0 = jnp.max(a_col[0:CHUNK], axis=0, keepdims=True)
            am1 = jnp.max(a_col[CHUNK:WIN], axis=0, keepdims=True)
            wk = jnp.exp(a_col - jnp.where(first, am0, am1)) * k
            cprev, nprev, mprev = [], [], []
            for c, (g, am) in enumerate(((g0, am0), (g1, am1))):
                cs = slice(c * CHUNK, (c + 1) * CHUNK)
                cprev.append(c_cur)
                nprev.append(n_cur)
                mprev.append(m_cur)
                u_mat = _dot_tn(wk[cs], v[cs])
                u_vec = jnp.sum(wk[cs], axis=0, keepdims=True)
                m_new = jnp.maximum(g + m_cur, am)
                dec = jnp.exp(g + m_cur - m_new)
                inj = jnp.exp(am - m_new)
                c_cur = dec * c_cur + inj * u_mat
                n_cur = dec * n_cur + inj * u_vec
                m_cur = m_new
            inter = b_col + jnp.where(first, mprev[0], mprev[1])
            m_t = jnp.maximum(inter, md)
            sc = jnp.exp(inter - m_t)
            s_mat = _dot_nt(q, k) * jnp.exp(dm - m_t)
            sv = _dot(s_mat, v)
            qc = jnp.concatenate([_dot(q[0:CHUNK], cprev[0]), _dot(q[CHUNK:WIN], cprev[1])], axis=0)
            qn = jnp.sum(q * jnp.where(first, nprev[0], nprev[1]), axis=1, keepdims=True)
            num = sc * qc + sv
            den = sc * qn + jnp.sum(s_mat, axis=1, keepdims=True)
            hh = num * (1.0 / jnp.maximum(jnp.abs(den), jnp.exp(-m_t)))
            mu = jnp.mean(hh, axis=1, keepdims=True)
            hc = hh - mu
            var = jnp.mean(hc * hc, axis=1, keepdims=True)
            win_outs.append(hc * lax.rsqrt(var + LN_EPS))
        c_st[h] = c_cur
        n_st[h] = n_cur
        m_st[h] = m_cur
        hn = win_outs[0] if len(win_outs) == 1 else jnp.concatenate(win_outs, axis=0)
        m_outs.append(_sigmoid(mo[:, hs]) * (hn * mg[:, hs]))

    gw = G_HEADS * G_DK
    la = _log_sigmoid(_dot(gb, glw[...]) + glb[...]) * (1.0 / G_TAU)
    la3 = _split3(la)
    bc = _merge3(_dot(tri16[...], la3), gw)
    bend = _merge3(_dot(ones16[...], la3), gw)
    gq = proj[:, C_GQ:C_GK] * (G_DK ** -0.5)
    gk = proj[:, C_GK:C_GV]
    gv = proj[:, C_GV:C_GG]
    gg = proj[:, C_GG:C_GATE]
    qh = gq * jnp.exp(bc)
    kh = gk * jnp.exp(bend - bc)
    eg = jnp.exp(bend)

    nsub = tt // SUB
    gk3 = gk.reshape(nsub, SUB, gw)
    bc3 = bc.reshape(nsub, SUB, gw)
    p_acc = jnp.zeros((tt, LANE), _F32)
    for s in range(SUB):
        kb = jnp.broadcast_to(gk3[:, s:s + 1, :], (nsub, SUB, gw)).reshape(tt, gw)
        bb = jnp.broadcast_to(bc3[:, s:s + 1, :], (nsub, SUB, gw)).reshape(tt, gw)
        y = gq * kb * jnp.exp(jnp.minimum(bc - bb, 0.0))
        p_acc = p_acc + _dot(y, gsel[s])
    p_bf = p_acc.astype(_BF)
    r2 = lax.broadcasted_iota(jnp.int32, (tt, tt), 0)
    c2 = lax.broadcasted_iota(jnp.int32, (tt, tt), 1)
    sub_mask = (c2 <= r2) & ((c2 // SUB) == (r2 // SUB))
    o_diag = []
    for h in range(G_HEADS):
        a_full = jnp.dot(p_bf, rsel[h], preferred_element_type=_F32)
        a_h = jnp.where(sub_mask, a_full, 0.0)
        o_diag.append(_dot(a_h, gv[:, h * G_DV:(h + 1) * G_DV]))

    lane_lo = lax.broadcasted_iota(jnp.int32, (SUB, LANE), 1) < G_DK
    o_int = [[None] * nsub for _ in range(G_HEADS)]
    for p in range(G_HEADS // 2):
        ps = slice(p * LANE, (p + 1) * LANE)
        st = s_st[p]
        v0 = gv[:, (2 * p) * G_DV:(2 * p + 1) * G_DV]
        v1 = gv[:, (2 * p + 1) * G_DV:(2 * p + 2) * G_DV]
        for j in range(nsub):
            js = slice(j * SUB, (j + 1) * SUB)
            qj, kj = qh[js, ps], kh[js, ps]
            qq = jnp.concatenate([jnp.where(lane_lo, qj, 0.0), jnp.where(lane_lo, 0.0, qj)], axis=0)
            kk = jnp.concatenate([jnp.where(lane_lo, kj, 0.0), jnp.where(lane_lo, 0.0, kj)], axis=0)
            vv = jnp.concatenate([v0[js], v1[js]], axis=0)
            oj = _dot_nt(qq, st)
            o_int[2 * p][j] = oj[0:SUB]
            o_int[2 * p + 1][j] = oj[SUB:2 * SUB]
            st = eg[j * SUB:j * SUB + 1, ps] * st + _dot_tn(vv, kk)
        s_st[p] = st
    gn = g_norm_g[...]
    g_outs = []
    for h in range(G_HEADS):
        hs = slice(h * G_DV, (h + 1) * G_DV)
        og = o_diag[h] + jnp.concatenate(o_int[h], axis=0)
        rms = lax.rsqrt(jnp.mean(og * og, axis=1, keepdims=True) + LN_EPS)
        gate = gg[:, hs]
        g_outs.append(gate * _sigmoid(gate) * (og * rms * gn[:, hs]))

    y = jnp.concatenate(m_outs + g_outs, axis=1)
    mix = _dot(y, w_out[...])
    o_ref[0] = _layer_norm(ALPHA * h0 + mix, ln1_g[...], ln1_b[...])


def _mixer_constants(tt):
    r = np.arange(tt)
    tri64 = ((r[None, :] <= r[:, None]) & (r[None, :] // CHUNK == r[:, None] // CHUNK))
    same16 = (r[None, :] // SUB == r[:, None] // SUB)
    tri16 = (r[None, :] <= r[:, None]) & same16
    gw = G_HEADS * G_DK
    gsel = np.zeros((SUB, gw, LANE), np.float32)
    for s in range(SUB):
        for h in range(G_HEADS):
            gsel[s, h * G_DK:(h + 1) * G_DK, h * SUB + s] = 1.0
    rsel = np.zeros((G_HEADS, LANE, tt), np.float32)
    for h in range(G_HEADS):
        for s in range(SUB):
            rsel[h, h * SUB + s, s::SUB] = 1.0
    to_bf = lambda a: jnp.asarray(a.astype(np.float32), dtype=_BF)
    return to_bf(tri64), to_bf(tri16), to_bf(same16), to_bf(gsel), to_bf(rsel)


def _full(shape):
    nd = len(shape)
    return pl.BlockSpec(shape, lambda b, i, _nd=nd: (0,) * _nd)


def _mixer_call(x, lnin_g, lnin_b, w_in_r, conv_w, conv_b, gate_bias, m_norm_g, glw, glb, g_norm_g,
                w_out, ln1_g, ln1_b, tt):
    bsz, t, d = x.shape
    consts = _mixer_constants(tt)
    params = (lnin_g, lnin_b, w_in_r, conv_w, conv_b, gate_bias, m_norm_g, glw, glb, g_norm_g,
              w_out, ln1_g, ln1_b) + consts
    tile = pl.BlockSpec((1, tt, d), lambda b, i: (b, i, 0))
    return pl.pallas_call(
        functools.partial(_mixer_kernel, tt=tt),
        out_shape=jax.ShapeDtypeStruct((bsz, t, d), _F32),
        grid=(bsz, t // tt),
        in_specs=[tile] + [_full(p.shape) for p in params],
        out_specs=tile,
        scratch_shapes=[
            pltpu.VMEM((tt + 8, 2 * 512), _F32),
            pltpu.VMEM((M_HEADS, M_DK, M_DV), _F32),
            pltpu.VMEM((M_HEADS, 1, M_DK), _F32),
            pltpu.VMEM((M_HEADS, 1, 1), _F32),
            pltpu.VMEM((G_HEADS // 2, G_DV, LANE), _F32),
        ],
        compiler_params=pltpu.CompilerParams(
            dimension_semantics=("arbitrary", "arbitrary"), vmem_limit_bytes=VMEM_LIMIT),
        name="mixer",
    )(x, *params)


def _memkv_kernel(mem_ref, wk_ref, wv_ref, k_ref, v_ref):
    m = mem_ref[0]
    k_ref[0] = _dot(m, wk_ref[...]).astype(_BF)
    v_ref[0] = _dot(m, wv_ref[...]).astype(_BF)


def _memkv_call(mem, wk, wv):
    bsz, nm, d = mem.shape
    blk = pl.BlockSpec((1, nm, d), lambda b: (b, 0, 0))
    wspec = pl.BlockSpec((d, d), lambda b: (0, 0))
    return pl.pallas_call(
        _memkv_kernel,
        out_shape=(jax.ShapeDtypeStruct((bsz, nm, d), _BF), jax.ShapeDtypeStruct((bsz, nm, d), _BF)),
        grid=(bsz,),
        in_specs=[blk, wspec, wspec],
        out_specs=(blk, blk),
        compiler_params=pltpu.CompilerParams(
            dimension_semantics=("arbitrary",), vmem_limit_bytes=VMEM_LIMIT),
        name="memkv",
    )(mem, wk, wv)


def _attnmlp_kernel(h_ref, k_ref, v_ref, wq, wo, ln2_g, ln2_b, w1, w2, ln3_g, ln3_b, o_ref):
    h1 = h_ref[0]
    q = _dot(h1, wq[...])
    kmem = k_ref[0]
    vmem = v_ref[0]
    heads = []
    for h in range(X_HEADS):
        hs = slice(h * X_DH, (h + 1) * X_DH)
        s = _dot_nt(q[:, hs], kmem[:, hs]) * (X_DH ** -0.5)
        e = jnp.exp(s - jnp.max(s, axis=1, keepdims=True))
        pr = e * (1.0 / jnp.sum(e, axis=1, keepdims=True))
        heads.append(_dot(pr, vmem[:, hs]))
    xa = _dot(jnp.concatenate(heads, axis=1), wo[...])
    h2 = _layer_norm(ALPHA * h1 + xa, ln2_g[...], ln2_b[...])
    h2b = h2.astype(_BF)
    ff = jnp.zeros(h2.shape, _F32)
    for f in range(D_FF // FF_BLK):
        fs = slice(f * FF_BLK, (f + 1) * FF_BLK)
        hid = jnp.maximum(jnp.dot(h2b, w1[:, fs], preferred_element_type=_F32), 0.0)
        ff = ff + _dot(hid * hid, w2[fs, :])
    o_ref[0] = _layer_norm(ALPHA * h2 + ff, ln3_g[...], ln3_b[...])


def _attnmlp_call(h1, kmem, vmem, wq, wo, ln2_g, ln2_b, w1, w2, ln3_g, ln3_b, tm):
    bsz, t, d = h1.shape
    nm = kmem.shape[1]
    tile = pl.BlockSpec((1, tm, d), lambda b, i: (b, i, 0))
    kv = pl.BlockSpec((1, nm, d), lambda b, i: (b, 0, 0))
    params = (wq, wo, ln2_g, ln2_b, w1, w2, ln3_g, ln3_b)
    return pl.pallas_call(
        _attnmlp_kernel,
        out_shape=jax.ShapeDtypeStruct((bsz, t, d), _F32),
        grid=(bsz, t // tm),
        in_specs=[tile, kv, kv] + [_full(p.shape) for p in params],
        out_specs=tile,
        compiler_params=pltpu.CompilerParams(
            dimension_semantics=("arbitrary", "arbitrary"), vmem_limit_bytes=VMEM_LIMIT),
        name="attnmlp",
    )(h1, kmem, vmem, *params)


def _time_tile(t, target):
    tt = min(t, target)
    assert t % tt == 0 and tt % WIN == 0, (t, tt)
    return tt


def kernel(x, mem, ln_in_g, ln_in_b, w_in, conv_w, conv_b, m_i_bias, m_f_bias, m_norm_g, g_lr_w, g_lr_b, g_norm_g, w_out, ln1_g, ln1_b, x_wq, x_wk, x_wv, x_wo, ln2_g, ln2_b, w_ff1, w_ff2, ln3_g, ln3_b):
    assert w_in.shape[0] == DEPTH == 1
    row = lambda a: a.reshape(1, -1).astype(_F32)
    wi = w_in[0]
    zpad = jnp.zeros((D_MODEL, LANE - 2 * M_HEADS - G_RANK), wi.dtype)
    w_in_r = jnp.concatenate([
        wi[:, O_MQ:O_MI],
        wi[:, O_GQ:O_GLR],
        wi[:, O_MI:O_GQ],
        wi[:, O_GLR:O_GLR + G_RANK], zpad], axis=1).astype(_BF)
    gate_bias = jnp.concatenate([m_i_bias[0], m_f_bias[0],
                                 jnp.zeros((LANE - 2 * M_HEADS,), _F32)]).reshape(1, LANE)
    glw = jnp.zeros((LANE, G_HEADS * G_DK), _F32).at[2 * M_HEADS:2 * M_HEADS + G_RANK].set(g_lr_w[0]).astype(_BF)
    tt = _time_tile(x.shape[1], 256)
    h1 = _mixer_call(x, row(ln_in_g), row(ln_in_b), w_in_r, conv_w[0].astype(_F32), row(conv_b[0]), gate_bias,
                     row(m_norm_g[0]), glw, row(g_lr_b[0]), row(g_norm_g[0]), w_out[0].astype(_BF),
                     row(ln1_g[0]), row(ln1_b[0]), tt)
    kmem, vmem = _memkv_call(mem, x_wk[0].astype(_BF), x_wv[0].astype(_BF))
    tm = _time_tile(x.shape[1], 512)
    return _attnmlp_call(h1, kmem, vmem, x_wq[0].astype(_BF), x_wo[0].astype(_BF), row(ln2_g[0]), row(ln2_b[0]),
                         w_ff1[0].astype(_BF), w_ff2[0].astype(_BF), row(ln3_g[0]), row(ln3_b[0]), tm)
```

```python
import functools

import numpy as np
import jax
import jax.numpy as jnp
from jax import lax
from jax.experimental import pallas as pl
from jax.experimental.pallas import tpu as pltpu

D_MODEL = 1024
CHUNK = 64
SUB = 16
WIN = 2 * CHUNK
N_MEM = 256
M_HEADS, M_DK, M_DV = 4, 128, 128
G_HEADS, G_DK, G_DV = 4, 64, 128
G_RANK = 16
G_TAU = 16.0
X_HEADS = 4
X_DH = D_MODEL // X_HEADS
D_FF = 4 * D_MODEL
FF_BLK = 512
DEPTH = 1
ALPHA = (2.0 * DEPTH) ** 0.25
LN_EPS = 1e-5
LANE = 128

C_MQ, C_MK, C_MV, C_MO = 0, 512, 1024, 1536
C_GQ, C_GK, C_GV, C_GG = 2048, 2304, 2560, 3072
C_GATE = 3584
IN_COLS_R = C_GATE + LANE
O_MQ, O_MK, O_MV, O_MO, O_MI, O_MF = 0, 512, 1024, 1536, 2048, 2052
O_GQ, O_GK, O_GV, O_GG, O_GLR = 2056, 2312, 2568, 3080, 3592

VMEM_LIMIT = 56 * 1024 * 1024

_BF = jnp.bfloat16
_F32 = jnp.float32


def _dot(a, b):
    return jnp.dot(a.astype(_BF), b.astype(_BF), preferred_element_type=_F32)


def _dot_nt(a, b):
    return lax.dot_general(a.astype(_BF), b.astype(_BF), (((1,), (1,)), ((), ())),
                           preferred_element_type=_F32)


def _dot_tn(a, b):
    return lax.dot_general(a.astype(_BF), b.astype(_BF), (((0,), (0,)), ((), ())),
                           preferred_element_type=_F32)


def _split2(x):
    hi = x.astype(_BF)
    lo = (x - hi.astype(_F32)).astype(_BF)
    return jnp.concatenate([hi, lo], axis=1)


def _merge2(y, n):
    return y[:, 0:n] + y[:, n:2 * n]


def _layer_norm(x, g, b):
    mu = jnp.mean(x, axis=-1, keepdims=True)
    xc = x - mu
    var = jnp.mean(xc * xc, axis=-1, keepdims=True)
    return xc * lax.rsqrt(var + LN_EPS) * g + b


def _log_sigmoid(x):
    return -(jnp.maximum(-x, 0.0) + jnp.log(1.0 + jnp.exp(-jnp.abs(x))))


def _sigmoid(x):
    return 1.0 / (1.0 + jnp.exp(-x))


def _round_robin(chains):
    while chains:
        alive = []
        for ch in chains:
            try:
                next(ch)
                alive.append(ch)
            except StopIteration:
                pass
        chains = alive


def _mixer_kernel(x_ref, lnin_g, lnin_b, w_in, conv_w, conv_b, gate_bias, m_norm_g, glw, glb, g_norm_g,
                  w_out, ln1_g, ln1_b, tri64, tri16, ones16, gsel, rsel,
                  o_ref, uext, c_st, n_st, m_st, s_st, *, tt):
    i = pl.program_id(1)

    @pl.when(i == 0)
    def _():
        uext[0:8, :] = jnp.zeros((8, 2 * 512), _F32)
        c_st[...] = jnp.zeros(c_st.shape, _F32)
        n_st[...] = jnp.zeros(n_st.shape, _F32)
        m_st[...] = jnp.zeros(m_st.shape, _F32)
        s_st[...] = jnp.zeros(s_st.shape, _F32)

    h0 = _layer_norm(x_ref[0], lnin_g[...], lnin_b[...])
    proj = _dot(h0, w_in[...])

    uext[8:8 + tt, :] = proj[:, C_MQ:C_MV]
    cw = conv_w[...]
    conv = (uext[5:5 + tt, :] * cw[0:1, :] + uext[6:6 + tt, :] * cw[1:2, :]
            + uext[7:7 + tt, :] * cw[2:3, :] + uext[8:8 + tt, :] * cw[3:4, :]) + conv_b[...]
    uext[0:8, :] = uext[tt:tt + 8, :]
    qk = conv * _sigmoid(conv)
    mq = qk[:, 0:512]
    mk = qk[:, 512:1024] * (M_DK ** -0.5)
    mv = proj[:, C_MV:C_MO]
    mo = proj[:, C_MO:C_GQ]

    gb = proj[:, C_GATE:C_GATE + LANE] + gate_bias[...]
    lane = lax.broadcasted_iota(jnp.int32, (tt, LANE), 1)
    lf = _log_sigmoid(gb)
    bcum = _merge2(_dot(tri64[...], _split2(lf)), LANE)
    vcol = jnp.where((lane >= M_HEADS) & (lane < 2 * M_HEADS), bcum, gb)
    vrow = vcol.T

    gw = G_HEADS * G_DK
    la = _log_sigmoid(_dot(gb, glw[...]) + glb[...]) * (1.0 / G_TAU)
    la2 = _split2(la)
    bc = _merge2(_dot(tri16[...], la2), gw)
    bend = _merge2(_dot(ones16[...], la2), gw)
    gq = proj[:, C_GQ:C_GK] * (G_DK ** -0.5)
    gk = proj[:, C_GK:C_GV]
    gv = proj[:, C_GV:C_GG]
    gg = proj[:, C_GG:C_GATE]
    qh = gq * jnp.exp(bc)
    kh = gk * jnp.exp(bend - bc)
    eg = jnp.exp(bend)
    nsub = tt // SUB

    rr = lax.broadcasted_iota(jnp.int32, (WIN, WIN), 0)
    cc = lax.broadcasted_iota(jnp.int32, (WIN, WIN), 1)
    valid = (cc <= rr) & ((cc >= CHUNK) == (rr >= CHUNK))
    first = lax.broadcasted_iota(jnp.int32, (WIN, 1), 0) < CHUNK
    mg = m_norm_g[...]
    m_outs = [None] * M_HEADS

    def mlstm_head(h):
        hs = slice(h * M_DK, (h + 1) * M_DK)
        c_cur = c_st[h]
        n_cur = n_st[h]
        m_cur = m_st[h]
        win_outs = []
        for w in range(tt // WIN):
            rs = slice(w * WIN, (w + 1) * WIN)
            q, k, v = mq[rs, hs], mk[rs, hs], mv[rs, hs]
            li_col, b_col = vcol[rs, h:h + 1], vcol[rs, M_HEADS + h:M_HEADS + h + 1]
            li_row, b_row = vrow[h:h + 1, rs], vrow[M_HEADS + h:M_HEADS + h + 1, rs]
            dm = jnp.where(valid, b_col - b_row + li_row, -jnp.inf)
            md = jnp.max(dm, axis=1, keepdims=True)
            g0, g1 = b_col[CHUNK - 1:CHUNK, :], b_col[WIN - 1:WIN, :]
            a_col = jnp.where(first, g0, g1) - b_col + li_col
            yield
            am0 = jnp.max(a_col[0:CHUNK], axis=0, keepdims=True)
            am1 = jnp.max(a_col[CHUNK:WIN], axis=0, keepdims=True)
            wk = jnp.exp(a_col - jnp.where(first, am0, am1)) * k
            s_raw = _dot_nt(q, k)
            yield
            cprev, nprev, mprev = [], [], []
            for c, (g, am) in enumerate(((g0, am0), (g1, am1))):
                cs = slice(c * CHUNK, (c + 1) * CHUNK)
                cprev.append(c_cur)
                nprev.append(n_cur)
                mprev.append(m_cur)
                u_mat = _dot_tn(wk[cs], v[cs])
                u_vec = jnp.sum(wk[cs], axis=0, keepdims=True)
                m_new = jnp.maximum(g + m_cur, am)
                dec = jnp.exp(g + m_cur - m_new)
                inj = jnp.exp(am - m_new)
                c_cur = dec * c_cur + inj * u_mat
                n_cur = dec * n_cur + inj * u_vec
                m_cur = m_new
            yield
            inter = b_col + jnp.where(first, mprev[0], mprev[1])
            m_t = jnp.maximum(inter, md)
            sc = jnp.exp(inter - m_t)
            s_mat = s_raw * jnp.exp(dm - m_t)
            yield
            sv = _dot(s_mat, v)
            qc = jnp.concatenate([_dot(q[0:CHUNK], cprev[0]), _dot(q[CHUNK:WIN], cprev[1])], axis=0)
            qn = jnp.sum(q * jnp.where(first, nprev[0], nprev[1]), axis=1, keepdims=True)
            ssum = jnp.sum(s_mat, axis=1, keepdims=True)
            yield
            num = sc * qc + sv
            den = sc * qn + ssum
            hh = num * (1.0 / jnp.maximum(jnp.abs(den), jnp.exp(-m_t)))
            mu = jnp.mean(hh, axis=1, keepdims=True)
            yield
            hc = hh - mu
            var = jnp.mean(hc * hc, axis=1, keepdims=True)
            yield
            win_outs.append(hc * lax.rsqrt(var + LN_EPS))
        c_st[h] = c_cur
        n_st[h] = n_cur
        m_st[h] = m_cur
        hn = win_outs[0] if len(win_outs) == 1 else jnp.concatenate(win_outs, axis=0)
        m_outs[h] = _sigmoid(mo[:, hs]) * (hn * mg[:, hs])

    o_diag = [None] * G_HEADS

    def gla_diag():
        gk3 = gk.reshape(nsub, SUB, gw)
        bc3 = bc.reshape(nsub, SUB, gw)
        p_acc = jnp.zeros((tt, LANE), _F32)
        for s in range(SUB):
            kb = jnp.broadcast_to(gk3[:, s:s + 1, :], (nsub, SUB, gw)).reshape(tt, gw)
            bb = jnp.broadcast_to(bc3[:, s:s + 1, :], (nsub, SUB, gw)).reshape(tt, gw)
            y = gq * kb * jnp.exp(jnp.minimum(bc - bb, 0.0))
            p_acc = p_acc + _dot(y, gsel[s])
            yield
        p_bf = p_acc.astype(_BF)
        r2 = lax.broadcasted_iota(jnp.int32, (tt, tt), 0)
        c2 = lax.broadcasted_iota(jnp.int32, (tt, tt), 1)
        sub_mask = (c2 <= r2) & ((c2 // SUB) == (r2 // SUB))
        for h in range(G_HEADS):
            a_full = jnp.dot(p_bf, rsel[h], preferred_element_type=_F32)
            a_h = jnp.where(sub_mask, a_full, 0.0)
            o_diag[h] = _dot(a_h, gv[:, h * G_DV:(h + 1) * G_DV])
            yield

    lane_lo = lax.broadcasted_iota(jnp.int32, (SUB, LANE), 1) < G_DK
    o_int = [[None] * nsub for _ in range(G_HEADS)]

    def gla_rec(p):
        ps = slice(p * LANE, (p + 1) * LANE)
        st = s_st[p]
        v0 = gv[:, (2 * p) * G_DV:(2 * p + 1) * G_DV]
        v1 = gv[:, (2 * p + 1) * G_DV:(2 * p + 2) * G_DV]
        for j in range(nsub):
            js = slice(j * SUB, (j + 1) * SUB)
            qj, kj = qh[js, ps], kh[js, ps]
            qq = jnp.concatenate([jnp.where(lane_lo, qj, 0.0), jnp.where(lane_lo, 0.0, qj)], axis=0)
            kk = jnp.concatenate([jnp.where(lane_lo, kj, 0.0), jnp.where(lane_lo, 0.0, kj)], axis=0)
            vv = jnp.concatenate([v0[js], v1[js]], axis=0)
            oj = _dot_nt(qq, st)
            o_int[2 * p][j] = oj[0:SUB]
            o_int[2 * p + 1][j] = oj[SUB:2 * SUB]
            st = eg[j * SUB:j * SUB + 1, ps] * st + _dot_tn(vv, kk)
            yield
        s_st[p] = st

    _round_robin([mlstm_head(h) for h in range(M_HEADS)] + [gla_diag()]
                 + [gla_rec(p) for p in range(G_HEADS // 2)])

    gn = g_norm_g[...]
    g_outs = []
    for h in range(G_HEADS):
        hs = slice(h * G_DV, (h + 1) * G_DV)
        og = o_diag[h] + jnp.concatenate(o_int[h], axis=0)
        rms = lax.rsqrt(jnp.mean(og * og, axis=1, keepdims=True) + LN_EPS)
        gate = gg[:, hs]
        g_outs.append(gate * _sigmoid(gate) * (og * rms * gn[:, hs]))

    y = jnp.concatenate(m_outs + g_outs, axis=1)
    mix = _dot(y, w_out[...])
    o_ref[0] = _layer_norm(ALPHA * h0 + mix, ln1_g[...], ln1_b[...])


def _mixer_constants(tt):
    r = np.arange(tt)
    tri64 = ((r[None, :] <= r[:, None]) & (r[None, :] // CHUNK == r[:, None] // CHUNK))
    same16 = (r[None, :] // SUB == r[:, None] // SUB)
    tri16 = (r[None, :] <= r[:, None]) & same16
    gw = G_HEADS * G_DK
    gsel = np.zeros((SUB, gw, LANE), np.float32)
    for s in range(SUB):
        for h in range(G_HEADS):
            gsel[s, h * G_DK:(h + 1) * G_DK, h * SUB + s] = 1.0
    rsel = np.zeros((G_HEADS, LANE, tt), np.float32)
    for h in range(G_HEADS):
        for s in range(SUB):
            rsel[h, h * SUB + s, s::SUB] = 1.0
    to_bf = lambda a: jnp.asarray(a.astype(np.float32), dtype=_BF)
    return to_bf(tri64), to_bf(tri16), to_bf(same16), to_bf(gsel), to_bf(rsel)


def _full(shape):
    nd = len(shape)
    return pl.BlockSpec(shape, lambda b, i, _nd=nd: (0,) * _nd)


def _mixer_call(x, lnin_g, lnin_b, w_in_r, conv_w, conv_b, gate_bias, m_norm_g, glw, glb, g_norm_g,
                w_out, ln1_g, ln1_b, tt):
    bsz, t, d = x.shape
    consts = _mixer_constants(tt)
    params = (lnin_g, lnin_b, w_in_r, conv_w, conv_b, gate_bias, m_norm_g, glw, glb, g_norm_g,
              w_out, ln1_g, ln1_b) + consts
    tile = pl.BlockSpec((1, tt, d), lambda b, i: (b, i, 0))
    return pl.pallas_call(
        functools.partial(_mixer_kernel, tt=tt),
        out_shape=jax.ShapeDtypeStruct((bsz, t, d), _F32),
        grid=(bsz, t // tt),
        in_specs=[tile] + [_full(p.shape) for p in params],
        out_specs=tile,
        scratch_shapes=[
            pltpu.VMEM((tt + 8, 2 * 512), _F32),
            pltpu.VMEM((M_HEADS, M_DK, M_DV), _F32),
            pltpu.VMEM((M_HEADS, 1, M_DK), _F32),
            pltpu.VMEM((M_HEADS, 1, 1), _F32),
            pltpu.VMEM((G_HEADS // 2, G_DV, LANE), _F32),
        ],
        compiler_params=pltpu.CompilerParams(
            dimension_semantics=("arbitrary", "arbitrary"), vmem_limit_bytes=VMEM_LIMIT),
        name="mixer",
    )(x, *params)


def _memkv_kernel(mem_ref, wk_ref, wv_ref, k_ref, v_ref):
    m = mem_ref[0]
    k_ref[0] = _dot(m, wk_ref[...]).astype(_BF)
    v_ref[0] = _dot(m, wv_ref[...]).astype(_BF)


def _memkv_call(mem, wk, wv):
    bsz, nm, d = mem.shape
    blk = pl.BlockSpec((1, nm, d), lambda b: (b, 0, 0))
    wspec = pl.BlockSpec((d, d), lambda b: (0, 0))
    return pl.pallas_call(
        _memkv_kernel,
        out_shape=(jax.ShapeDtypeStruct((bsz, nm, d), _BF), jax.ShapeDtypeStruct((bsz, nm, d), _BF)),
        grid=(bsz,),
        in_specs=[blk, wspec, wspec],
        out_specs=(blk, blk),
        compiler_params=pltpu.CompilerParams(
            dimension_semantics=("arbitrary",), vmem_limit_bytes=VMEM_LIMIT),
        name="memkv",
    )(mem, wk, wv)


def _attnmlp_kernel(h_ref, k_ref, v_ref, wq, wo, ln2_g, ln2_b, w1, w2, ln3_g, ln3_b, o_ref):
    h1 = h_ref[0]
    q = _dot(h1, wq[...])
    kmem = k_ref[0]
    vmem = v_ref[0]
    heads = []
    for h in range(X_HEADS):
        hs = slice(h * X_DH, (h + 1) * X_DH)
        s = _dot_nt(q[:, hs], kmem[:, hs]) * (X_DH ** -0.5)
        e = jnp.exp(s - jnp.max(s, axis=1, keepdims=True))
        pr = e * (1.0 / jnp.sum(e, axis=1, keepdims=True))
        heads.append(_dot(pr, vmem[:, hs]))
    xa = _dot(jnp.concatenate(heads, axis=1), wo[...])
    h2 = _layer_norm(ALPHA * h1 + xa, ln2_g[...], ln2_b[...])
    h2b = h2.astype(_BF)
    ff = jnp.zeros(h2.shape, _F32)
    for f in range(D_FF // FF_BLK):
        fs = slice(f * FF_BLK, (f + 1) * FF_BLK)
        hid = jnp.maximum(jnp.dot(h2b, w1[:, fs], preferred_element_type=_F32), 0.0)
        ff = ff + _dot(hid * hid, w2[fs, :])
    o_ref[0] = _layer_norm(ALPHA * h2 + ff, ln3_g[...], ln3_b[...])


def _attnmlp_call(h1, kmem, vmem, wq, wo, ln2_g, ln2_b, w1, w2, ln3_g, ln3_b, tm):
    bsz, t, d = h1.shape
    nm = kmem.shape[1]
    tile = pl.BlockSpec((1, tm, d), lambda b, i: (b, i, 0))
    kv = pl.BlockSpec((1, nm, d), lambda b, i: (b, 0, 0))
    params = (wq, wo, ln2_g, ln2_b, w1, w2, ln3_g, ln3_b)
    return pl.pallas_call(
        _attnmlp_kernel,
        out_shape=jax.ShapeDtypeStruct((bsz, t, d), _F32),
        grid=(bsz, t // tm),
        in_specs=[tile, kv, kv] + [_full(p.shape) for p in params],
        out_specs=tile,
        compiler_params=pltpu.CompilerParams(
            dimension_semantics=("arbitrary", "arbitrary"), vmem_limit_bytes=VMEM_LIMIT),
        name="attnmlp",
    )(h1, kmem, vmem, *params)


def _time_tile(t, target):
    tt = min(t, target)
    assert t % tt == 0 and tt % WIN == 0, (t, tt)
    return tt


def kernel(x, mem, ln_in_g, ln_in_b, w_in, conv_w, conv_b, m_i_bias, m_f_bias, m_norm_g, g_lr_w, g_lr_b, g_norm_g, w_out, ln1_g, ln1_b, x_wq, x_wk, x_wv, x_wo, ln2_g, ln2_b, w_ff1, w_ff2, ln3_g, ln3_b):
    assert w_in.shape[0] == DEPTH == 1
    row = lambda a: a.reshape(1, -1).astype(_F32)
    wi = w_in[0]
    zpad = jnp.zeros((D_MODEL, LANE - 2 * M_HEADS - G_RANK), wi.dtype)
    w_in_r = jnp.concatenate([
        wi[:, O_MQ:O_MI],
        wi[:, O_GQ:O_GLR],
        wi[:, O_MI:O_GQ],
        wi[:, O_GLR:O_GLR + G_RANK], zpad], axis=1).astype(_BF)
    gate_bias = jnp.concatenate([m_i_bias[0], m_f_bias[0],
                                 jnp.zeros((LANE - 2 * M_HEADS,), _F32)]).reshape(1, LANE)
    glw = jnp.zeros((LANE, G_HEADS * G_DK), _F32).at[2 * M_HEADS:2 * M_HEADS + G_RANK].set(g_lr_w[0]).astype(_BF)
    tt = _time_tile(x.shape[1], 256)
    h1 = _mixer_call(x, row(ln_in_g), row(ln_in_b), w_in_r, conv_w[0].astype(_F32), row(conv_b[0]), gate_bias,
                     row(m_norm_g[0]), glw, row(g_lr_b[0]), row(g_norm_g[0]), w_out[0].astype(_BF),
                     row(ln1_g[0]), row(ln1_b[0]), tt)
    kmem, vmem = _memkv_call(mem, x_wk[0].astype(_BF), x_wv[0].astype(_BF))
    tm = _time_tile(x.shape[1], 512)
    return _attnmlp_call(h1, kmem, vmem, x_wq[0].astype(_BF), x_wo[0].astype(_BF), row(ln2_g[0]), row(ln2_b[0]),
                         w_ff1[0].astype(_BF), w_ff2[0].astype(_BF), row(ln3_g[0]), row(ln3_b[0]), tm)
```

```python
import functools

import numpy as np
import jax
import jax.numpy as jnp
from jax import lax
from jax.experimental import pallas as pl
from jax.experimental.pallas import tpu as pltpu

D_MODEL = 1024
CHUNK = 64
SUB = 16
WIN = 2 * CHUNK
N_MEM = 256
M_HEADS, M_DK, M_DV = 4, 128, 128
G_HEADS, G_DK, G_DV = 4, 64, 128
G_RANK = 16
G_TAU = 16.0
X_HEADS = 4
X_DH = D_MODEL // X_HEADS
D_FF = 4 * D_MODEL
FF_BLK = 512
IN_BLK = 512
DEPTH = 1
ALPHA = (2.0 * DEPTH) ** 0.25
LN_EPS = 1e-5
LOG2E = 1.4426950408889634
LANE = 128

C_MQ, C_MK, C_MV, C_MO = 0, 512, 1024, 1536
C_GQ, C_GK, C_GV, C_GG = 2048, 2304, 2560, 3072
C_GATE = 3584
IN_COLS_R = C_GATE + LANE
O_MQ, O_MK, O_MV, O_MO, O_MI, O_MF = 0, 512, 1024, 1536, 2048, 2052
O_GQ, O_GK, O_GV, O_GG, O_GLR = 2056, 2312, 2568, 3080, 3592

VMEM_LIMIT = 56 * 1024 * 1024

_BF = jnp.bfloat16
_F32 = jnp.float32


def _dot(a, b):
    return jnp.dot(a.astype(_BF), b.astype(_BF), preferred_element_type=_F32)


def _dot_nt(a, b):
    return lax.dot_general(a.astype(_BF), b.astype(_BF), (((1,), (1,)), ((), ())),
                           preferred_element_type=_F32)


def _dot_tn(a, b):
    return lax.dot_general(a.astype(_BF), b.astype(_BF), (((0,), (0,)), ((), ())),
                           preferred_element_type=_F32)


def _split2(x):
    hi = x.astype(_BF)
    lo = (x - hi.astype(_F32)).astype(_BF)
    return jnp.concatenate([hi, lo], axis=1)


def _merge2(y, n):
    return y[:, 0:n] + y[:, n:2 * n]


def _layer_norm(x, g, b):
    mu = jnp.mean(x, axis=-1, keepdims=True)
    xc = x - mu
    var = jnp.mean(xc * xc, axis=-1, keepdims=True)
    return xc * lax.rsqrt(var + LN_EPS) * g + b


def _log_sigmoid(x):
    return -(jnp.maximum(-x, 0.0) + jnp.log(1.0 + jnp.exp(-jnp.abs(x))))


def _sigmoid(x):
    return 1.0 / (1.0 + jnp.exp(-x))


def _round_robin(chains):
    while chains:
        alive = []
        for ch in chains:
            try:
                next(ch)
                alive.append(ch)
            except StopIteration:
                pass
        chains = alive


def _mixer_kernel(x_ref, lnin_g, lnin_b, w_in, conv_w, conv_b, gate_bias, m_norm_g, glw, glb, g_norm_g,
                  w_out, ln1_g, ln1_b, tri64, tri16, ones16, gsel, rsel,
                  o_ref, uext, c_st, n_st, m_st, s_st, proj_buf, h0_buf, *, tt, nt):
    g = pl.program_id(0)

    @pl.when(g == 0)
    def _():
        proj_buf[1] = jnp.zeros(proj_buf.shape[1:], _F32)
        h0_buf[1] = jnp.zeros(h0_buf.shape[1:], _F32)

    @pl.when((g == 0) | ((g - 1) % nt == 0))
    def _():
        uext[0:8, :] = jnp.zeros((8, 2 * 512), _F32)
        c_st[...] = jnp.zeros(c_st.shape, _F32)
        n_st[...] = jnp.zeros(n_st.shape, _F32)
        m_st[...] = jnp.zeros(m_st.shape, _F32)
        s_st[...] = jnp.zeros(s_st.shape, _F32)

    def step(slot_a):
        slot_b = 1 - slot_a

        def in_proj():
            h0n = _layer_norm(x_ref[0], lnin_g[...], lnin_b[...])
            h0_buf[slot_a] = h0n
            h0b = h0n.astype(_BF)
            yield
            for c0 in range(0, IN_COLS_R, IN_BLK):
                c1 = min(c0 + IN_BLK, IN_COLS_R)
                proj_buf[slot_a, :, c0:c1] = jnp.dot(h0b, w_in[:, c0:c1], preferred_element_type=_F32)
                yield

        pb = proj_buf.at[slot_b]
        h0 = h0_buf[slot_b]

        uext[8:8 + tt, :] = pb[:, C_MQ:C_MV]
        cw = conv_w[...]
        conv = (uext[5:5 + tt, :] * cw[0:1, :] + uext[6:6 + tt, :] * cw[1:2, :]
                + uext[7:7 + tt, :] * cw[2:3, :] + uext[8:8 + tt, :] * cw[3:4, :]) + conv_b[...]
        uext[0:8, :] = uext[tt:tt + 8, :]
        qk = conv * _sigmoid(conv)
        mq = qk[:, 0:512]
        mk = qk[:, 512:1024] * (M_DK ** -0.5)
        mv = pb[:, C_MV:C_MO]
        mo = pb[:, C_MO:C_GQ]

        gb = pb[:, C_GATE:C_GATE + LANE] + gate_bias[...]
        lane = lax.broadcasted_iota(jnp.int32, (tt, LANE), 1)
        lf = _log_sigmoid(gb)
        bcum = _merge2(_dot(tri64[...], _split2(lf)), LANE)
        vcol = jnp.where((lane >= M_HEADS) & (lane < 2 * M_HEADS), bcum, gb)
        vrow = vcol.T

        gw = G_HEADS * G_DK
        la = _log_sigmoid(_dot(gb, glw[...]) + glb[...]) * (1.0 / G_TAU)
        la2 = _split2(la)
        bc = _merge2(_dot(tri16[...], la2), gw)
        bend = _merge2(_dot(ones16[...], la2), gw)
        gq = pb[:, C_GQ:C_GK] * (G_DK ** -0.5)
        gk = pb[:, C_GK:C_GV]
        gv = pb[:, C_GV:C_GG]
        gg = pb[:, C_GG:C_GATE]
        qh = gq * jnp.exp(bc)
        kh = gk * jnp.exp(bend - bc)
        eg = jnp.exp(bend)
        bc2 = bc * LOG2E
        nsub = tt // SUB

        rr = lax.broadcasted_iota(jnp.int32, (WIN, WIN), 0)
        cc = lax.broadcasted_iota(jnp.int32, (WIN, WIN), 1)
        valid = (cc <= rr) & ((cc >= CHUNK) == (rr >= CHUNK))
        first = lax.broadcasted_iota(jnp.int32, (WIN, 1), 0) < CHUNK
        mg = m_norm_g[...]
        m_outs = [None] * M_HEADS

        def mlstm_head(h):
            hs = slice(h * M_DK, (h + 1) * M_DK)
            c_cur = c_st[h]
            n_cur = n_st[h]
            m_cur = m_st[h]
            win_outs = []
            for w in range(tt // WIN):
                rs = slice(w * WIN, (w + 1) * WIN)
                q, k, v = mq[rs, hs], mk[rs, hs], mv[rs, hs]
                li_col, b_col = vcol[rs, h:h + 1], vcol[rs, M_HEADS + h:M_HEADS + h + 1]
                li_row, b_row = vrow[h:h + 1, rs], vrow[M_HEADS + h:M_HEADS + h + 1, rs]
                dm = jnp.where(valid, b_col - b_row + li_row, -jnp.inf)
                md = jnp.max(dm, axis=1, keepdims=True)
                g0, g1 = b_col[CHUNK - 1:CHUNK, :], b_col[WIN - 1:WIN, :]
                a_col = jnp.where(first, g0, g1) - b_col + li_col
                yield
                am0 = jnp.max(a_col[0:CHUNK], axis=0, keepdims=True)
                am1 = jnp.max(a_col[CHUNK:WIN], axis=0, keepdims=True)
                wk = jnp.exp(a_col - jnp.where(first, am0, am1)) * k
                s_raw = _dot_nt(q, k)
                yield
                cprev, nprev, mprev = [], [], []
                for c, (gc, am) in enumerate(((g0, am0), (g1, am1))):
                    cs = slice(c * CHUNK, (c + 1) * CHUNK)
                    cprev.append(c_cur)
                    nprev.append(n_cur)
                    mprev.append(m_cur)
                    u_mat = _dot_tn(wk[cs], v[cs])
                    u_vec = jnp.sum(wk[cs], axis=0, keepdims=True)
                    m_new = jnp.maximum(gc + m_cur, am)
                    dec = jnp.exp(gc + m_cur - m_new)
                    inj = jnp.exp(am - m_new)
                    c_cur = dec * c_cur + inj * u_mat
                    n_cur = dec * n_cur + inj * u_vec
                    m_cur = m_new
                yield
                inter = b_col + jnp.where(first, mprev[0], mprev[1])
                m_t = jnp.maximum(inter, md)
                sc = jnp.exp(inter - m_t)
                s_mat = s_raw * jnp.exp(dm - m_t)
                yield
                sv = _dot(s_mat, v)
                qc = jnp.concatenate([_dot(q[0:CHUNK], cprev[0]), _dot(q[CHUNK:WIN], cprev[1])], axis=0)
                qn = jnp.sum(q * jnp.where(first, nprev[0], nprev[1]), axis=1, keepdims=True)
                ssum = jnp.sum(s_mat, axis=1, keepdims=True)
                yield
                num = sc * qc + sv
                den = sc * qn + ssum
                hh = num * (1.0 / jnp.maximum(jnp.abs(den), jnp.exp(-m_t)))
                mu = jnp.mean(hh, axis=1, keepdims=True)
                yield
                hc = hh - mu
                var = jnp.mean(hc * hc, axis=1, keepdims=True)
                yield
                win_outs.append(hc * lax.rsqrt(var + LN_EPS))
            c_st[h] = c_cur
            n_st[h] = n_cur
            m_st[h] = m_cur
            hn = win_outs[0] if len(win_outs) == 1 else jnp.concatenate(win_outs, axis=0)
            m_outs[h] = _sigmoid(mo[:, hs]) * (hn * mg[:, hs])

        o_diag = [None] * G_HEADS

        def gla_diag():
            gk3 = gk.reshape(nsub, SUB, gw)
            bc3 = bc2.reshape(nsub, SUB, gw)
            p_acc = jnp.zeros((tt, LANE), _F32)
            for s in range(SUB):
                kb = jnp.broadcast_to(gk3[:, s:s + 1, :], (nsub, SUB, gw)).reshape(tt, gw)
                bb = jnp.broadcast_to(bc3[:, s:s + 1, :], (nsub, SUB, gw)).reshape(tt, gw)
                y = gq * kb * jnp.exp2(jnp.minimum(bc2 - bb, 0.0))
                p_acc = p_acc + _dot(y, gsel[s])
                yield
            p_bf = p_acc.astype(_BF)
            r2 = lax.broadcasted_iota(jnp.int32, (WIN, WIN), 0)
            c2 = lax.broadcasted_iota(jnp.int32, (WIN, WIN), 1)
            sub_mask = (c2 <= r2) & ((c2 // SUB) == (r2 // SUB))
            for h in range(G_HEADS):
                outs = []
                for w in range(tt // WIN):
                    rs = slice(w * WIN, (w + 1) * WIN)
                    a_full = jnp.dot(p_bf[rs], rsel[h], preferred_element_type=_F32)
                    a_h = jnp.where(sub_mask, a_full, 0.0)
                    outs.append(_dot(a_h, gv[rs, h * G_DV:(h + 1) * G_DV]))
                o_diag[h] = outs[0] if len(outs) == 1 else jnp.concatenate(outs, axis=0)
                yield

        lane_lo = lax.broadcasted_iota(jnp.int32, (SUB, LANE), 1) < G_DK
        o_int = [[None] * nsub for _ in range(G_HEADS)]

        def gla_rec(p):
            ps = slice(p * LANE, (p + 1) * LANE)
            st = s_st[p]
            v0 = gv[:, (2 * p) * G_DV:(2 * p + 1) * G_DV]
            v1 = gv[:, (2 * p + 1) * G_DV:(2 * p + 2) * G_DV]
            for j in range(nsub):
                js = slice(j * SUB, (j + 1) * SUB)
                qj, kj = qh[js, ps], kh[js, ps]
                qq = jnp.concatenate([jnp.where(lane_lo, qj, 0.0), jnp.where(lane_lo, 0.0, qj)], axis=0)
                kk = jnp.concatenate([jnp.where(lane_lo, kj, 0.0), jnp.where(lane_lo, 0.0, kj)], axis=0)
                vv = jnp.concatenate([v0[js], v1[js]], axis=0)
                oj = _dot_nt(qq, st)
                o_int[2 * p][j] = oj[0:SUB]
                o_int[2 * p + 1][j] = oj[SUB:2 * SUB]
                st = eg[j * SUB:j * SUB + 1, ps] * st + _dot_tn(vv, kk)
                yield
            s_st[p] = st

        _round_robin([in_proj()] + [mlstm_head(h) for h in range(M_HEADS)] + [gla_diag()]
                     + [gla_rec(p) for p in range(G_HEADS // 2)])

        gn = g_norm_g[...]
        g_outs = []
        for h in range(G_HEADS):
            hs = slice(h * G_DV, (h + 1) * G_DV)
            og = o_diag[h] + jnp.concatenate(o_int[h], axis=0)
            rms = lax.rsqrt(jnp.mean(og * og, axis=1, keepdims=True) + LN_EPS)
            gate = gg[:, hs]
            g_outs.append(gate * _sigmoid(gate) * (og * rms * gn[:, hs]))

        y = jnp.concatenate(m_outs + g_outs, axis=1)
        mix = _dot(y, w_out[...])
        o_ref[0] = _layer_norm(ALPHA * h0 + mix, ln1_g[...], ln1_b[...])

    for par in range(2):
        pl.when(g % 2 == par)(functools.partial(step, par))


def _mixer_constants(tt):
    r = np.arange(tt)
    tri64 = ((r[None, :] <= r[:, None]) & (r[None, :] // CHUNK == r[:, None] // CHUNK))
    same16 = (r[None, :] // SUB == r[:, None] // SUB)
    tri16 = (r[None, :] <= r[:, None]) & same16
    gw = G_HEADS * G_DK
    gsel = np.zeros((SUB, gw, LANE), np.float32)
    for s in range(SUB):
        for h in range(G_HEADS):
            gsel[s, h * G_DK:(h + 1) * G_DK, h * SUB + s] = 1.0
    rsel = np.zeros((G_HEADS, LANE, WIN), np.float32)
    for h in range(G_HEADS):
        for s in range(SUB):
            rsel[h, h * SUB + s, s::SUB] = 1.0
    to_bf = lambda a: jnp.asarray(a.astype(np.float32), dtype=_BF)
    return to_bf(tri64), to_bf(tri16), to_bf(same16), to_bf(gsel), to_bf(rsel)


def _full(shape):
    nd = len(shape)
    return pl.BlockSpec(shape, lambda b, i, _nd=nd: (0,) * _nd)


def _mixer_call(x, lnin_g, lnin_b, w_in_r, conv_w, conv_b, gate_bias, m_norm_g, glw, glb, g_norm_g,
                w_out, ln1_g, ln1_b, tt):
    bsz, t, d = x.shape
    consts = _mixer_constants(tt)
    params = (lnin_g, lnin_b, w_in_r, conv_w, conv_b, gate_bias, m_norm_g, glw, glb, g_norm_g,
              w_out, ln1_g, ln1_b) + consts
    nt = t // tt
    ntiles = bsz * nt
    in_tile = pl.BlockSpec((1, tt, d), lambda g: (jnp.minimum(g, ntiles - 1) // nt, jnp.minimum(g, ntiles - 1) % nt, 0))
    out_tile = pl.BlockSpec((1, tt, d), lambda g: (jnp.maximum(g - 1, 0) // nt, jnp.maximum(g - 1, 0) % nt, 0))
    full = lambda shape: pl.BlockSpec(shape, lambda g, _nd=len(shape): (0,) * _nd)
    return pl.pallas_call(
        functools.partial(_mixer_kernel, tt=tt, nt=nt),
        out_shape=jax.ShapeDtypeStruct((bsz, t, d), _F32),
        grid=(ntiles + 1,),
        in_specs=[in_tile] + [full(p.shape) for p in params],
        out_specs=out_tile,
        scratch_shapes=[
            pltpu.VMEM((tt + 8, 2 * 512), _F32),
            pltpu.VMEM((M_HEADS, M_DK, M_DV), _F32),
            pltpu.VMEM((M_HEADS, 1, M_DK), _F32),
            pltpu.VMEM((M_HEADS, 1, 1), _F32),
            pltpu.VMEM((G_HEADS // 2, G_DV, LANE), _F32),
            pltpu.VMEM((2, tt, IN_COLS_R), _F32),
            pltpu.VMEM((2, tt, D_MODEL), _F32),
        ],
        compiler_params=pltpu.CompilerParams(
            dimension_semantics=("arbitrary",), vmem_limit_bytes=VMEM_LIMIT),
        name="mixer",
    )(x, *params)


def _memkv_kernel(mem_ref, wk_ref, wv_ref, k_ref, v_ref):
    m = mem_ref[0]
    k_ref[0] = _dot(m, wk_ref[...]).astype(_BF)
    v_ref[0] = _dot(m, wv_ref[...]).astype(_BF)


def _memkv_call(mem, wk, wv):
    bsz, nm, d = mem.shape
    blk = pl.BlockSpec((1, nm, d), lambda b: (b, 0, 0))
    wspec = pl.BlockSpec((d, d), lambda b: (0, 0))
    return pl.pallas_call(
        _memkv_kernel,
        out_shape=(jax.ShapeDtypeStruct((bsz, nm, d), _BF), jax.ShapeDtypeStruct((bsz, nm, d), _BF)),
        grid=(bsz,),
        in_specs=[blk, wspec, wspec],
        out_specs=(blk, blk),
        compiler_params=pltpu.CompilerParams(
            dimension_semantics=("arbitrary",), vmem_limit_bytes=VMEM_LIMIT),
        name="memkv",
    )(mem, wk, wv)


def _attnmlp_kernel(h_ref, k_ref, v_ref, wq, wo, ln2_g, ln2_b, w1, w2, ln3_g, ln3_b, o_ref):
    h1 = h_ref[0]
    q = _dot(h1, wq[...])
    kmem = k_ref[0]
    vmem = v_ref[0]
    heads = []
    for h in range(X_HEADS):
        hs = slice(h * X_DH, (h + 1) * X_DH)
        s = _dot_nt(q[:, hs], kmem[:, hs]) * (X_DH ** -0.5)
        e = jnp.exp(s - jnp.max(s, axis=1, keepdims=True))
        pr = e * (1.0 / jnp.sum(e, axis=1, keepdims=True))
        heads.append(_dot(pr, vmem[:, hs]))
    xa = _dot(jnp.concatenate(heads, axis=1), wo[...])
    h2 = _layer_norm(ALPHA * h1 + xa, ln2_g[...], ln2_b[...])
    h2b = h2.astype(_BF)
    ff = jnp.zeros(h2.shape, _F32)
    for f in range(D_FF // FF_BLK):
        fs = slice(f * FF_BLK, (f + 1) * FF_BLK)
        hid = jnp.maximum(jnp.dot(h2b, w1[:, fs], preferred_element_type=_F32), 0.0)
        ff = ff + _dot(hid * hid, w2[fs, :])
    o_ref[0] = _layer_norm(ALPHA * h2 + ff, ln3_g[...], ln3_b[...])


def _attnmlp_call(h1, kmem, vmem, wq, wo, ln2_g, ln2_b, w1, w2, ln3_g, ln3_b, tm):
    bsz, t, d = h1.shape
    nm = kmem.shape[1]
    tile = pl.BlockSpec((1, tm, d), lambda b, i: (b, i, 0))
    kv = pl.BlockSpec((1, nm, d), lambda b, i: (b, 0, 0))
    params = (wq, wo, ln2_g, ln2_b, w1, w2, ln3_g, ln3_b)
    return pl.pallas_call(
        _attnmlp_kernel,
        out_shape=jax.ShapeDtypeStruct((bsz, t, d), _F32),
        grid=(bsz, t // tm),
        in_specs=[tile, kv, kv] + [_full(p.shape) for p in params],
        out_specs=tile,
        compiler_params=pltpu.CompilerParams(
            dimension_semantics=("arbitrary", "arbitrary"), vmem_limit_bytes=VMEM_LIMIT),
        name="attnmlp",
    )(h1, kmem, vmem, *params)


def _time_tile(t, target):
    tt = min(t, target)
    assert t % tt == 0 and tt % WIN == 0, (t, tt)
    return tt


def kernel(x, mem, ln_in_g, ln_in_b, w_in, conv_w, conv_b, m_i_bias, m_f_bias, m_norm_g, g_lr_w, g_lr_b, g_norm_g, w_out, ln1_g, ln1_b, x_wq, x_wk, x_wv, x_wo, ln2_g, ln2_b, w_ff1, w_ff2, ln3_g, ln3_b):
    assert w_in.shape[0] == DEPTH == 1
    row = lambda a: a.reshape(1, -1).astype(_F32)
    wi = w_in[0]
    zpad = jnp.zeros((D_MODEL, LANE - 2 * M_HEADS - G_RANK), wi.dtype)
    w_in_r = jnp.concatenate([
        wi[:, O_MQ:O_MI],
        wi[:, O_GQ:O_GLR],
        wi[:, O_MI:O_GQ],
        wi[:, O_GLR:O_GLR + G_RANK], zpad], axis=1).astype(_BF)
    gate_bias = jnp.concatenate([m_i_bias[0], m_f_bias[0],
                                 jnp.zeros((LANE - 2 * M_HEADS,), _F32)]).reshape(1, LANE)
    glw = jnp.zeros((LANE, G_HEADS * G_DK), _F32).at[2 * M_HEADS:2 * M_HEADS + G_RANK].set(g_lr_w[0]).astype(_BF)
    tt = _time_tile(x.shape[1], 256)
    h1 = _mixer_call(x, row(ln_in_g), row(ln_in_b), w_in_r, conv_w[0].astype(_F32), row(conv_b[0]), gate_bias,
                     row(m_norm_g[0]), glw, row(g_lr_b[0]), row(g_norm_g[0]), w_out[0].astype(_BF),
                     row(ln1_g[0]), row(ln1_b[0]), tt)
    kmem, vmem = _memkv_call(mem, x_wk[0].astype(_BF), x_wv[0].astype(_BF))
    tm = _time_tile(x.shape[1], 512)
    return _attnmlp_call(h1, kmem, vmem, x_wq[0].astype(_BF), x_wo[0].astype(_BF), row(ln2_g[0]), row(ln2_b[0]),
                         w_ff1[0].astype(_BF), w_ff2[0].astype(_BF), row(ln3_g[0]), row(ln3_b[0]), tm)
```

```python
import functools

import numpy as np
import jax
import jax.numpy as jnp
from jax import lax
from jax.experimental import pallas as pl
from jax.experimental.pallas import tpu as pltpu

D_MODEL = 1024
CHUNK = 64
SUB = 16
WIN = 2 * CHUNK
N_MEM = 256
M_HEADS, M_DK, M_DV = 4, 128, 128
G_HEADS, G_DK, G_DV = 4, 64, 128
G_RANK = 16
G_TAU = 16.0
X_HEADS = 4
X_DH = D_MODEL // X_HEADS
D_FF = 4 * D_MODEL
FF_BLK = 512
IN_BLK = 512
DEPTH = 1
ALPHA = (2.0 * DEPTH) ** 0.25
LN_EPS = 1e-5
LOG2E = 1.4426950408889634
LANE = 128

C_MQ, C_MK, C_MV, C_MO = 0, 512, 1024, 1536
C_GQ, C_GK, C_GV, C_GG = 2048, 2304, 2560, 3072
C_GATE = 3584
IN_COLS_R = C_GATE + LANE
O_MQ, O_MK, O_MV, O_MO, O_MI, O_MF = 0, 512, 1024, 1536, 2048, 2052
O_GQ, O_GK, O_GV, O_GG, O_GLR = 2056, 2312, 2568, 3080, 3592

VMEM_LIMIT = 56 * 1024 * 1024

_BF = jnp.bfloat16
_F32 = jnp.float32


def _dot(a, b):
    return jnp.dot(a.astype(_BF), b.astype(_BF), preferred_element_type=_F32)


def _dot_nt(a, b):
    return lax.dot_general(a.astype(_BF), b.astype(_BF), (((1,), (1,)), ((), ())),
                           preferred_element_type=_F32)


def _dot_tn(a, b):
    return lax.dot_general(a.astype(_BF), b.astype(_BF), (((0,), (0,)), ((), ())),
                           preferred_element_type=_F32)


def _split2(x):
    hi = x.astype(_BF)
    lo = (x - hi.astype(_F32)).astype(_BF)
    return jnp.concatenate([hi, lo], axis=1)


def _merge2(y, n):
    return y[:, 0:n] + y[:, n:2 * n]


def _layer_norm(x, g, b):
    mu = jnp.mean(x, axis=-1, keepdims=True)
    xc = x - mu
    var = jnp.mean(xc * xc, axis=-1, keepdims=True)
    return xc * lax.rsqrt(var + LN_EPS) * g + b


def _log_sigmoid(x):
    return -(jnp.maximum(-x, 0.0) + jnp.log(1.0 + jnp.exp(-jnp.abs(x))))


def _sigmoid(x):
    return 1.0 / (1.0 + jnp.exp(-x))


def _round_robin(chains):
    while chains:
        alive = []
        for ch in chains:
            try:
                next(ch)
                alive.append(ch)
            except StopIteration:
                pass
        chains = alive


def _mixer_kernel(x_ref, lnin_g, lnin_b, w_m, w_g, w_gate, conv_w, conv_b, gate_bias, m_norm_g, glw, glb, g_norm_g,
                  w_out, ln1_g, ln1_b, tri64, tri16, ones16, gsel, rsel,
                  o_ref, uext, c_st, n_st, m_st, s_st, proj_buf, h0_buf, *, tt, nt):
    g = pl.program_id(0)

    @pl.when(g == 0)
    def _():
        proj_buf[1] = jnp.zeros(proj_buf.shape[1:], _F32)
        h0_buf[1] = jnp.zeros(h0_buf.shape[1:], _F32)

    @pl.when((g == 0) | ((g - 1) % nt == 0))
    def _():
        uext[0:8, :] = jnp.zeros((8, 2 * 512), _F32)
        c_st[...] = jnp.zeros(c_st.shape, _F32)
        n_st[...] = jnp.zeros(n_st.shape, _F32)
        m_st[...] = jnp.zeros(m_st.shape, _F32)
        s_st[...] = jnp.zeros(s_st.shape, _F32)

    def step(slot_a):
        slot_b = 1 - slot_a

        def in_proj():
            h0n = _layer_norm(x_ref[0], lnin_g[...], lnin_b[...])
            h0_buf[slot_a] = h0n
            h0b = h0n.astype(_BF)
            yield
            for w_ref, base in ((w_m, C_MQ), (w_g, C_GQ), (w_gate, C_GATE)):
                ncol = w_ref.shape[1]
                for c0 in range(0, ncol, IN_BLK):
                    c1 = min(c0 + IN_BLK, ncol)
                    proj_buf[slot_a, :, base + c0:base + c1] = jnp.dot(h0b, w_ref[:, c0:c1],
                                                                       preferred_element_type=_F32)
                    yield

        pb = proj_buf.at[slot_b]
        h0 = h0_buf[slot_b]

        uext[8:8 + tt, :] = pb[:, C_MQ:C_MV]
        cw = conv_w[...]
        conv = (uext[5:5 + tt, :] * cw[0:1, :] + uext[6:6 + tt, :] * cw[1:2, :]
                + uext[7:7 + tt, :] * cw[2:3, :] + uext[8:8 + tt, :] * cw[3:4, :]) + conv_b[...]
        uext[0:8, :] = uext[tt:tt + 8, :]
        qk = conv * _sigmoid(conv)
        mq = qk[:, 0:512]
        mk = qk[:, 512:1024] * (M_DK ** -0.5)
        mv = pb[:, C_MV:C_MO]
        mo = pb[:, C_MO:C_GQ]

        gb = pb[:, C_GATE:C_GATE + LANE] + gate_bias[...]
        lane = lax.broadcasted_iota(jnp.int32, (tt, LANE), 1)
        lf = _log_sigmoid(gb)
        bcum = _merge2(_dot(tri64[...], _split2(lf)), LANE)
        vcol = jnp.where((lane >= M_HEADS) & (lane < 2 * M_HEADS), bcum, gb)
        vrow = vcol.T

        gw = G_HEADS * G_DK
        la = _log_sigmoid(_dot(gb, glw[...]) + glb[...]) * (1.0 / G_TAU)
        la2 = _split2(la)
        bc = _merge2(_dot(tri16[...], la2), gw)
        bend = _merge2(_dot(ones16[...], la2), gw)
        gq = pb[:, C_GQ:C_GK] * (G_DK ** -0.5)
        gk = pb[:, C_GK:C_GV]
        gv = pb[:, C_GV:C_GG]
        gg = pb[:, C_GG:C_GATE]
        qh = gq * jnp.exp(bc)
        kh = gk * jnp.exp(bend - bc)
        eg = jnp.exp(bend)
        bc2 = bc * LOG2E
        nsub = tt // SUB

        rr = lax.broadcasted_iota(jnp.int32, (WIN, WIN), 0)
        cc = lax.broadcasted_iota(jnp.int32, (WIN, WIN), 1)
        valid = (cc <= rr) & ((cc >= CHUNK) == (rr >= CHUNK))
        first = lax.broadcasted_iota(jnp.int32, (WIN, 1), 0) < CHUNK
        mg = m_norm_g[...]
        m_outs = [None] * M_HEADS

        def mlstm_head(h):
            hs = slice(h * M_DK, (h + 1) * M_DK)
            c_cur = c_st[h]
            n_cur = n_st[h]
            m_cur = m_st[h]
            win_outs = []
            for w in range(tt // WIN):
                rs = slice(w * WIN, (w + 1) * WIN)
                q, k, v = mq[rs, hs], mk[rs, hs], mv[rs, hs]
                li_col, b_col = vcol[rs, h:h + 1], vcol[rs, M_HEADS + h:M_HEADS + h + 1]
                li_row, b_row = vrow[h:h + 1, rs], vrow[M_HEADS + h:M_HEADS + h + 1, rs]
                dm = jnp.where(valid, b_col - b_row + li_row, -jnp.inf)
                md = jnp.max(dm, axis=1, keepdims=True)
                g0, g1 = b_col[CHUNK - 1:CHUNK, :], b_col[WIN - 1:WIN, :]
                a_col = jnp.where(first, g0, g1) - b_col + li_col
                yield
                am0 = jnp.max(a_col[0:CHUNK], axis=0, keepdims=True)
                am1 = jnp.max(a_col[CHUNK:WIN], axis=0, keepdims=True)
                wk = jnp.exp(a_col - jnp.where(first, am0, am1)) * k
                s_raw = _dot_nt(q, k)
                yield
                cprev, nprev, mprev = [], [], []
                for c, (gc, am) in enumerate(((g0, am0), (g1, am1))):
                    cs = slice(c * CHUNK, (c + 1) * CHUNK)
                    cprev.append(c_cur)
                    nprev.append(n_cur)
                    mprev.append(m_cur)
                    u_mat = _dot_tn(wk[cs], v[cs])
                    u_vec = jnp.sum(wk[cs], axis=0, keepdims=True)
                    m_new = jnp.maximum(gc + m_cur, am)
                    dec = jnp.exp(gc + m_cur - m_new)
                    inj = jnp.exp(am - m_new)
                    c_cur = dec * c_cur + inj * u_mat
                    n_cur = dec * n_cur + inj * u_vec
                    m_cur = m_new
                yield
                inter = b_col + jnp.where(first, mprev[0], mprev[1])
                m_t = jnp.maximum(inter, md)
                sc = jnp.exp(inter - m_t)
                s_mat = s_raw * jnp.exp(dm - m_t)
                yield
                sv = _dot(s_mat, v)
                qc = jnp.concatenate([_dot(q[0:CHUNK], cprev[0]), _dot(q[CHUNK:WIN], cprev[1])], axis=0)
                qn = jnp.sum(q * jnp.where(first, nprev[0], nprev[1]), axis=1, keepdims=True)
                ssum = jnp.sum(s_mat, axis=1, keepdims=True)
                yield
                num = sc * qc + sv
                den = sc * qn + ssum
                hh = num * (1.0 / jnp.maximum(jnp.abs(den), jnp.exp(-m_t)))
                mu = jnp.mean(hh, axis=1, keepdims=True)
                yield
                hc = hh - mu
                var = jnp.mean(hc * hc, axis=1, keepdims=True)
                yield
                win_outs.append(hc * lax.rsqrt(var + LN_EPS))
            c_st[h] = c_cur
            n_st[h] = n_cur
            m_st[h] = m_cur
            hn = win_outs[0] if len(win_outs) == 1 else jnp.concatenate(win_outs, axis=0)
            m_outs[h] = _sigmoid(mo[:, hs]) * (hn * mg[:, hs])

        o_diag = [None] * G_HEADS

        def gla_diag():
            gk3 = gk.reshape(nsub, SUB, gw)
            bc3 = bc2.reshape(nsub, SUB, gw)
            p_acc = jnp.zeros((tt, LANE), _F32)
            for s in range(SUB):
                kb = jnp.broadcast_to(gk3[:, s:s + 1, :], (nsub, SUB, gw)).reshape(tt, gw)
                bb = jnp.broadcast_to(bc3[:, s:s + 1, :], (nsub, SUB, gw)).reshape(tt, gw)
                y = gq * kb * jnp.exp2(jnp.minimum(bc2 - bb, 0.0))
                p_acc = p_acc + _dot(y, gsel[s])
                yield
            p_bf = p_acc.astype(_BF)
            r2 = lax.broadcasted_iota(jnp.int32, (WIN, WIN), 0)
            c2 = lax.broadcasted_iota(jnp.int32, (WIN, WIN), 1)
            sub_mask = (c2 <= r2) & ((c2 // SUB) == (r2 // SUB))
            for h in range(G_HEADS):
                outs = []
                for w in range(tt // WIN):
                    rs = slice(w * WIN, (w + 1) * WIN)
                    a_full = jnp.dot(p_bf[rs], rsel[h], preferred_element_type=_F32)
                    a_h = jnp.where(sub_mask, a_full, 0.0)
                    outs.append(_dot(a_h, gv[rs, h * G_DV:(h + 1) * G_DV]))
                o_diag[h] = outs[0] if len(outs) == 1 else jnp.concatenate(outs, axis=0)
                yield

        lane_lo = lax.broadcasted_iota(jnp.int32, (SUB, LANE), 1) < G_DK
        o_int = [[None] * nsub for _ in range(G_HEADS)]

        def gla_rec(p):
            ps = slice(p * LANE, (p + 1) * LANE)
            st = s_st[p]
            v0 = gv[:, (2 * p) * G_DV:(2 * p + 1) * G_DV]
            v1 = gv[:, (2 * p + 1) * G_DV:(2 * p + 2) * G_DV]
            for j in range(nsub):
                js = slice(j * SUB, (j + 1) * SUB)
                qj, kj = qh[js, ps], kh[js, ps]
                qq = jnp.concatenate([jnp.where(lane_lo, qj, 0.0), jnp.where(lane_lo, 0.0, qj)], axis=0)
                kk = jnp.concatenate([jnp.where(lane_lo, kj, 0.0), jnp.where(lane_lo, 0.0, kj)], axis=0)
                vv = jnp.concatenate([v0[js], v1[js]], axis=0)
                oj = _dot_nt(qq, st)
                o_int[2 * p][j] = oj[0:SUB]
                o_int[2 * p + 1][j] = oj[SUB:2 * SUB]
                st = eg[j * SUB:j * SUB + 1, ps] * st + _dot_tn(vv, kk)
                yield
            s_st[p] = st

        _round_robin([in_proj()] + [mlstm_head(h) for h in range(M_HEADS)] + [gla_diag()]
                     + [gla_rec(p) for p in range(G_HEADS // 2)])

        gn = g_norm_g[...]
        g_outs = []
        for h in range(G_HEADS):
            hs = slice(h * G_DV, (h + 1) * G_DV)
            og = o_diag[h] + jnp.concatenate(o_int[h], axis=0)
            rms = lax.rsqrt(jnp.mean(og * og, axis=1, keepdims=True) + LN_EPS)
            gate = gg[:, hs]
            g_outs.append(gate * _sigmoid(gate) * (og * rms * gn[:, hs]))

        y = jnp.concatenate(m_outs + g_outs, axis=1)
        mix = _dot(y, w_out[...])
        o_ref[0] = _layer_norm(ALPHA * h0 + mix, ln1_g[...], ln1_b[...])

    for par in range(2):
        pl.when(g % 2 == par)(functools.partial(step, par))


def _mixer_constants(tt):
    r = np.arange(tt)
    tri64 = ((r[None, :] <= r[:, None]) & (r[None, :] // CHUNK == r[:, None] // CHUNK))
    same16 = (r[None, :] // SUB == r[:, None] // SUB)
    tri16 = (r[None, :] <= r[:, None]) & same16
    gw = G_HEADS * G_DK
    gsel = np.zeros((SUB, gw, LANE), np.float32)
    for s in range(SUB):
        for h in range(G_HEADS):
            gsel[s, h * G_DK:(h + 1) * G_DK, h * SUB + s] = 1.0
    rsel = np.zeros((G_HEADS, LANE, WIN), np.float32)
    for h in range(G_HEADS):
        for s in range(SUB):
            rsel[h, h * SUB + s, s::SUB] = 1.0
    to_bf = lambda a: jnp.asarray(a.astype(np.float32), dtype=_BF)
    return to_bf(tri64), to_bf(tri16), to_bf(same16), to_bf(gsel), to_bf(rsel)


def _full(shape):
    nd = len(shape)
    return pl.BlockSpec(shape, lambda b, i, _nd=nd: (0,) * _nd)


def _mixer_call(x, lnin_g, lnin_b, w_m, w_g, w_gate, conv_w, conv_b, gate_bias, m_norm_g, glw, glb, g_norm_g,
                w_out, ln1_g, ln1_b, tt):
    bsz, t, d = x.shape
    consts = _mixer_constants(tt)
    params = (lnin_g, lnin_b, w_m, w_g, w_gate, conv_w, conv_b, gate_bias, m_norm_g, glw, glb, g_norm_g,
              w_out, ln1_g, ln1_b) + consts
    nt = t // tt
    ntiles = bsz * nt
    in_tile = pl.BlockSpec((1, tt, d), lambda g: (jnp.minimum(g, ntiles - 1) // nt, jnp.minimum(g, ntiles - 1) % nt, 0))
    out_tile = pl.BlockSpec((1, tt, d), lambda g: (jnp.maximum(g - 1, 0) // nt, jnp.maximum(g - 1, 0) % nt, 0))
    full = lambda shape: pl.BlockSpec(shape, lambda g, _nd=len(shape): (0,) * _nd)
    return pl.pallas_call(
        functools.partial(_mixer_kernel, tt=tt, nt=nt),
        out_shape=jax.ShapeDtypeStruct((bsz, t, d), _F32),
        grid=(ntiles + 1,),
        in_specs=[in_tile] + [full(p.shape) for p in params],
        out_specs=out_tile,
        scratch_shapes=[
            pltpu.VMEM((tt + 8, 2 * 512), _F32),
            pltpu.VMEM((M_HEADS, M_DK, M_DV), _F32),
            pltpu.VMEM((M_HEADS, 1, M_DK), _F32),
            pltpu.VMEM((M_HEADS, 1, 1), _F32),
            pltpu.VMEM((G_HEADS // 2, G_DV, LANE), _F32),
            pltpu.VMEM((2, tt, IN_COLS_R), _F32),
            pltpu.VMEM((2, tt, D_MODEL), _F32),
        ],
        compiler_params=pltpu.CompilerParams(
            dimension_semantics=("arbitrary",), vmem_limit_bytes=VMEM_LIMIT),
        name="mixer",
    )(x, *params)


def _memkv_kernel(mem_ref, wk_ref, wv_ref, wq_ref, wo_ref, wqk_ref, vo_ref):
    m = mem_ref[0]
    k = _dot(m, wk_ref[...])
    v = _dot(m, wv_ref[...])
    for h in range(X_HEADS):
        hs = slice(h * X_DH, (h + 1) * X_DH)
        ms = slice(h * N_MEM, (h + 1) * N_MEM)
        wqk_ref[0, :, ms] = (_dot_nt(wq_ref[:, hs], k[:, hs]) * (X_DH ** -0.5)).astype(_BF)
        vo_ref[0, ms, :] = _dot(v[:, hs], wo_ref[hs, :]).astype(_BF)


def _memkv_call(mem, wk, wv, wq, wo):
    bsz, nm, d = mem.shape
    blk = pl.BlockSpec((1, nm, d), lambda b: (b, 0, 0))
    wspec = pl.BlockSpec((d, d), lambda b: (0, 0))
    return pl.pallas_call(
        _memkv_kernel,
        out_shape=(jax.ShapeDtypeStruct((bsz, d, X_HEADS * nm), _BF),
                   jax.ShapeDtypeStruct((bsz, X_HEADS * nm, d), _BF)),
        grid=(bsz,),
        in_specs=[blk, wspec, wspec, wspec, wspec],
        out_specs=(pl.BlockSpec((1, d, X_HEADS * nm), lambda b: (b, 0, 0)),
                   pl.BlockSpec((1, X_HEADS * nm, d), lambda b: (b, 0, 0))),
        compiler_params=pltpu.CompilerParams(
            dimension_semantics=("arbitrary",), vmem_limit_bytes=VMEM_LIMIT),
        name="memkv",
    )(mem, wk, wv, wq, wo)


def _attnmlp_kernel(h_ref, wqk_ref, vo_ref, ln2_g, ln2_b, w1, w2, ln3_g, ln3_b, o_ref):
    h1 = h_ref[0]
    s_all = _dot(h1, wqk_ref[0])
    probs = []
    for h in range(X_HEADS):
        s = s_all[:, h * N_MEM:(h + 1) * N_MEM]
        e = jnp.exp(s - jnp.max(s, axis=1, keepdims=True))
        probs.append((e * (1.0 / jnp.sum(e, axis=1, keepdims=True))).astype(_BF))
    xa = jnp.dot(jnp.concatenate(probs, axis=1), vo_ref[0], preferred_element_type=_F32)
    h2 = _layer_norm(ALPHA * h1 + xa, ln2_g[...], ln2_b[...])
    h2b = h2.astype(_BF)
    ff = jnp.zeros(h2.shape, _F32)
    for f in range(D_FF // FF_BLK):
        fs = slice(f * FF_BLK, (f + 1) * FF_BLK)
        hid = jnp.maximum(jnp.dot(h2b, w1[:, fs], preferred_element_type=_F32), 0.0)
        ff = ff + _dot(hid * hid, w2[fs, :])
    o_ref[0] = _layer_norm(ALPHA * h2 + ff, ln3_g[...], ln3_b[...])


def _attnmlp_call(h1, wqk, vo, ln2_g, ln2_b, w1, w2, ln3_g, ln3_b, tm):
    bsz, t, d = h1.shape
    tile = pl.BlockSpec((1, tm, d), lambda b, i: (b, i, 0))
    per_batch = lambda a: pl.BlockSpec((1,) + a.shape[1:], lambda b, i: (b, 0, 0))
    params = (ln2_g, ln2_b, w1, w2, ln3_g, ln3_b)
    return pl.pallas_call(
        _attnmlp_kernel,
        out_shape=jax.ShapeDtypeStruct((bsz, t, d), _F32),
        grid=(bsz, t // tm),
        in_specs=[tile, per_batch(wqk), per_batch(vo)] + [_full(p.shape) for p in params],
        out_specs=tile,
        compiler_params=pltpu.CompilerParams(
            dimension_semantics=("arbitrary", "arbitrary"), vmem_limit_bytes=VMEM_LIMIT),
        name="attnmlp",
    )(h1, wqk, vo, *params)


def _time_tile(t, target):
    tt = min(t, target)
    assert t % tt == 0 and tt % WIN == 0, (t, tt)
    return tt


def kernel(x, mem, ln_in_g, ln_in_b, w_in, conv_w, conv_b, m_i_bias, m_f_bias, m_norm_g, g_lr_w, g_lr_b, g_norm_g, w_out, ln1_g, ln1_b, x_wq, x_wk, x_wv, x_wo, ln2_g, ln2_b, w_ff1, w_ff2, ln3_g, ln3_b):
    assert w_in.shape[0] == DEPTH == 1
    row = lambda a: a.reshape(1, -1).astype(_F32)
    wi = w_in[0]
    w_m = wi[:, O_MQ:O_MI].astype(_BF)
    w_g = wi[:, O_GQ:O_GLR].astype(_BF)
    w_gate = jnp.concatenate([wi[:, O_MI:O_GQ], wi[:, O_GLR:O_GLR + G_RANK],
                              jnp.zeros((D_MODEL, LANE - 2 * M_HEADS - G_RANK), wi.dtype)], axis=1).astype(_BF)
    gate_bias = jnp.concatenate([m_i_bias[0], m_f_bias[0],
                                 jnp.zeros((LANE - 2 * M_HEADS,), _F32)]).reshape(1, LANE)
    glw = jnp.zeros((LANE, G_HEADS * G_DK), _F32).at[2 * M_HEADS:2 * M_HEADS + G_RANK].set(g_lr_w[0]).astype(_BF)
    tt = _time_tile(x.shape[1], 256)
    h1 = _mixer_call(x, row(ln_in_g), row(ln_in_b), w_m, w_g, w_gate, conv_w[0].astype(_F32), row(conv_b[0]),
                     gate_bias, row(m_norm_g[0]), glw, row(g_lr_b[0]), row(g_norm_g[0]), w_out[0].astype(_BF),
                     row(ln1_g[0]), row(ln1_b[0]), tt)
    wqk, vo = _memkv_call(mem, x_wk[0].astype(_BF), x_wv[0].astype(_BF), x_wq[0].astype(_BF), x_wo[0].astype(_BF))
    tm = _time_tile(x.shape[1], 512)
    return _attnmlp_call(h1, wqk, vo, row(ln2_g[0]), row(ln2_b[0]),
                         w_ff1[0].astype(_BF), w_ff2[0].astype(_BF), row(ln3_g[0]), row(ln3_b[0]), tm)
```

```python
import functools

import numpy as np
import jax
import jax.numpy as jnp
from jax import lax
from jax.experimental import pallas as pl
from jax.experimental.pallas import tpu as pltpu

D_MODEL = 1024
CHUNK = 64
SUB = 16
WIN = 2 * CHUNK
N_MEM = 256
M_HEADS, M_DK, M_DV = 4, 128, 128
G_HEADS, G_DK, G_DV = 4, 64, 128
G_RANK = 16
G_TAU = 16.0
X_HEADS = 4
X_DH = D_MODEL // X_HEADS
D_FF = 4 * D_MODEL
FF_BLK = 512
IN_BLK = 512
DEPTH = 1
ALPHA = (2.0 * DEPTH) ** 0.25
LN_EPS = 1e-5
LOG2E = 1.4426950408889634
LANE = 128

C_MQ, C_MK, C_MV, C_MO = 0, 512, 1024, 1536
C_GQ, C_GK, C_GV, C_GG = 2048, 2304, 2560, 3072
C_GATE = 3584
IN_COLS_R = C_GATE + LANE
O_MQ, O_MK, O_MV, O_MO, O_MI, O_MF = 0, 512, 1024, 1536, 2048, 2052
O_GQ, O_GK, O_GV, O_GG, O_GLR = 2056, 2312, 2568, 3080, 3592

VMEM_LIMIT = 56 * 1024 * 1024

_BF = jnp.bfloat16
_F32 = jnp.float32


def _dot(a, b):
    return jnp.dot(a.astype(_BF), b.astype(_BF), preferred_element_type=_F32)


def _dot_nt(a, b):
    return lax.dot_general(a.astype(_BF), b.astype(_BF), (((1,), (1,)), ((), ())),
                           preferred_element_type=_F32)


def _dot_tn(a, b):
    return lax.dot_general(a.astype(_BF), b.astype(_BF), (((0,), (0,)), ((), ())),
                           preferred_element_type=_F32)


def _split2(x):
    hi = x.astype(_BF)
    lo = (x - hi.astype(_F32)).astype(_BF)
    return jnp.concatenate([hi, lo], axis=1)


def _merge2(y, n):
    return y[:, 0:n] + y[:, n:2 * n]


def _layer_norm(x, g, b):
    mu = jnp.mean(x, axis=-1, keepdims=True)
    xc = x - mu
    var = jnp.mean(xc * xc, axis=-1, keepdims=True)
    return xc * lax.rsqrt(var + LN_EPS) * g + b


def _log_sigmoid(x):
    return -(jnp.maximum(-x, 0.0) + jnp.log(1.0 + jnp.exp(-jnp.abs(x))))


def _sigmoid(x):
    return 1.0 / (1.0 + jnp.exp(-x))


def _round_robin(chains):
    while chains:
        alive = []
        for ch in chains:
            try:
                next(ch)
                alive.append(ch)
            except StopIteration:
                pass
        chains = alive


def _mixer_kernel(x_ref, lnin_g, lnin_b, w_m, w_g, w_gate, conv_w, conv_b, gate_bias, m_norm_g, glw, glb, g_norm_g,
                  w_out, ln1_g, ln1_b, tri64, tri16, ones16, gsel, rsel,
                  o_ref, uext, c_st, n_st, m_st, s_st, proj_buf, h0_buf, *, tt, nt):
    g = pl.program_id(0)

    @pl.when(g == 0)
    def _():
        proj_buf[1] = jnp.zeros(proj_buf.shape[1:], _F32)
        h0_buf[1] = jnp.zeros(h0_buf.shape[1:], _F32)

    @pl.when((g == 0) | ((g - 1) % nt == 0))
    def _():
        uext[0:8, :] = jnp.zeros((8, 2 * 512), _F32)
        c_st[...] = jnp.zeros(c_st.shape, _F32)
        n_st[...] = jnp.zeros(n_st.shape, _F32)
        m_st[...] = jnp.zeros(m_st.shape, _F32)
        s_st[...] = jnp.zeros(s_st.shape, _F32)

    def step(slot_a):
        slot_b = 1 - slot_a

        def in_proj():
            h0n = _layer_norm(x_ref[0], lnin_g[...], lnin_b[...])
            h0_buf[slot_a] = h0n
            h0b = h0n.astype(_BF)
            yield
            for w_ref, base in ((w_m, C_MQ), (w_g, C_GQ), (w_gate, C_GATE)):
                ncol = w_ref.shape[1]
                for c0 in range(0, ncol, IN_BLK):
                    c1 = min(c0 + IN_BLK, ncol)
                    proj_buf[slot_a, :, base + c0:base + c1] = jnp.dot(h0b, w_ref[:, c0:c1],
                                                                       preferred_element_type=_F32)
                    yield

        pb = proj_buf.at[slot_b]
        h0 = h0_buf[slot_b]

        uext[8:8 + tt, :] = pb[:, C_MQ:C_MV]
        cw = conv_w[...]
        conv = (uext[5:5 + tt, :] * cw[0:1, :] + uext[6:6 + tt, :] * cw[1:2, :]
                + uext[7:7 + tt, :] * cw[2:3, :] + uext[8:8 + tt, :] * cw[3:4, :]) + conv_b[...]
        uext[0:8, :] = uext[tt:tt + 8, :]
        qk = conv * _sigmoid(conv)
        mq = qk[:, 0:512]
        mk = qk[:, 512:1024] * (M_DK ** -0.5)
        mv = pb[:, C_MV:C_MO]
        mo = pb[:, C_MO:C_GQ]

        gb = pb[:, C_GATE:C_GATE + LANE] + gate_bias[...]
        lane = lax.broadcasted_iota(jnp.int32, (tt, LANE), 1)
        lf = _log_sigmoid(gb)
        bcum = _merge2(_dot(tri64[...], _split2(lf)), LANE)
        vcol = jnp.where((lane >= M_HEADS) & (lane < 2 * M_HEADS), bcum, gb)
        vrow = vcol.T

        gw = G_HEADS * G_DK
        la = _log_sigmoid(_dot(gb, glw[...]) + glb[...]) * (1.0 / G_TAU)
        la2 = _split2(la)
        bc = _merge2(_dot(tri16[...], la2), gw)
        bend = _merge2(_dot(ones16[...], la2), gw)
        gq = pb[:, C_GQ:C_GK] * (G_DK ** -0.5)
        gk = pb[:, C_GK:C_GV]
        gv = pb[:, C_GV:C_GG]
        gg = pb[:, C_GG:C_GATE]
        qh = gq * jnp.exp(bc)
        kh = gk * jnp.exp(bend - bc)
        eg = jnp.exp(bend)
        bc2 = bc * LOG2E
        nsub = tt // SUB

        rr = lax.broadcasted_iota(jnp.int32, (WIN, WIN), 0)
        cc = lax.broadcasted_iota(jnp.int32, (WIN, WIN), 1)
        valid = (cc <= rr) & ((cc >= CHUNK) == (rr >= CHUNK))
        first = lax.broadcasted_iota(jnp.int32, (WIN, 1), 0) < CHUNK
        mg = m_norm_g[...]
        m_outs = [None] * M_HEADS

        def mlstm_head(h):
            hs = slice(h * M_DK, (h + 1) * M_DK)
            c_cur = c_st[h]
            n_cur = n_st[h]
            m_cur = m_st[h]
            win_outs = []
            for w in range(tt // WIN):
                rs = slice(w * WIN, (w + 1) * WIN)
                q, k, v = mq[rs, hs], mk[rs, hs], mv[rs, hs]
                li_col, b_col = vcol[rs, h:h + 1], vcol[rs, M_HEADS + h:M_HEADS + h + 1]
                li_row, b_row = vrow[h:h + 1, rs], vrow[M_HEADS + h:M_HEADS + h + 1, rs]
                dm = jnp.where(valid, b_col - b_row + li_row, -jnp.inf)
                md = jnp.max(dm, axis=1, keepdims=True)
                g0, g1 = b_col[CHUNK - 1:CHUNK, :], b_col[WIN - 1:WIN, :]
                a_col = jnp.where(first, g0, g1) - b_col + li_col
                yield
                am0 = jnp.max(a_col[0:CHUNK], axis=0, keepdims=True)
                am1 = jnp.max(a_col[CHUNK:WIN], axis=0, keepdims=True)
                wk = jnp.exp(a_col - jnp.where(first, am0, am1)) * k
                s_raw = _dot_nt(q, k)
                yield
                cprev, nprev, mprev = [], [], []
                for c, (gc, am) in enumerate(((g0, am0), (g1, am1))):
                    cs = slice(c * CHUNK, (c + 1) * CHUNK)
                    cprev.append(c_cur)
                    nprev.append(n_cur)
                    mprev.append(m_cur)
                    u_mat = _dot_tn(wk[cs], v[cs])
                    u_vec = jnp.sum(wk[cs], axis=0, keepdims=True)
                    m_new = jnp.maximum(gc + m_cur, am)
                    dec = jnp.exp(gc + m_cur - m_new)
                    inj = jnp.exp(am - m_new)
                    c_cur = dec * c_cur + inj * u_mat
                    n_cur = dec * n_cur + inj * u_vec
                    m_cur = m_new
                yield
                inter = b_col + jnp.where(first, mprev[0], mprev[1])
                m_t = jnp.maximum(inter, md)
                sc = jnp.exp(inter - m_t)
                s_mat = s_raw * jnp.exp(dm - m_t)
                yield
                sv = _dot(s_mat, v)
                qc = jnp.concatenate([_dot(q[0:CHUNK], cprev[0]), _dot(q[CHUNK:WIN], cprev[1])], axis=0)
                qn = jnp.sum(q * jnp.where(first, nprev[0], nprev[1]), axis=1, keepdims=True)
                ssum = jnp.sum(s_mat, axis=1, keepdims=True)
                yield
                num = sc * qc + sv
                den = sc * qn + ssum
                hh = num * (1.0 / jnp.maximum(jnp.abs(den), jnp.exp(-m_t)))
                mu = jnp.mean(hh, axis=1, keepdims=True)
                yield
                hc = hh - mu
                var = jnp.mean(hc * hc, axis=1, keepdims=True)
                yield
                win_outs.append(hc * lax.rsqrt(var + LN_EPS))
            c_st[h] = c_cur
            n_st[h] = n_cur
            m_st[h] = m_cur
            hn = win_outs[0] if len(win_outs) == 1 else jnp.concatenate(win_outs, axis=0)
            m_outs[h] = _sigmoid(mo[:, hs]) * (hn * mg[:, hs])

        o_diag = [None] * G_HEADS

        def gla_diag():
            gk3 = gk.reshape(nsub, SUB, gw)
            bc3 = bc2.reshape(nsub, SUB, gw)
            half = SUB // 2
            hi = lambda a: a.reshape(nsub, 2, half, a.shape[-1])[:, 1].reshape(nsub * half, a.shape[-1])
            gq_hi, bc2_hi = hi(gq), hi(bc2)
            p_acc = jnp.zeros((tt, LANE), _F32)
            p_hi = jnp.zeros((tt // 2, LANE), _F32)
            for s in range(SUB):
                if s < half:
                    kb = jnp.broadcast_to(gk3[:, s:s + 1, :], (nsub, SUB, gw)).reshape(tt, gw)
                    bb = jnp.broadcast_to(bc3[:, s:s + 1, :], (nsub, SUB, gw)).reshape(tt, gw)
                    y = gq * kb * jnp.exp2(jnp.minimum(bc2 - bb, 0.0))
                    p_acc = p_acc + _dot(y, gsel[s])
                else:
                    kb = jnp.broadcast_to(gk3[:, s:s + 1, :], (nsub, half, gw)).reshape(tt // 2, gw)
                    bb = jnp.broadcast_to(bc3[:, s:s + 1, :], (nsub, half, gw)).reshape(tt // 2, gw)
                    y = gq_hi * kb * jnp.exp2(jnp.minimum(bc2_hi - bb, 0.0))
                    p_hi = p_hi + _dot(y, gsel[s])
                yield
            p4 = p_acc.reshape(nsub, 2, half, LANE)
            p_acc = jnp.concatenate([p4[:, 0:1], p4[:, 1:2] + p_hi.reshape(nsub, 1, half, LANE)],
                                    axis=1).reshape(tt, LANE)
            p_bf = p_acc.astype(_BF)
            r2 = lax.broadcasted_iota(jnp.int32, (WIN, WIN), 0)
            c2 = lax.broadcasted_iota(jnp.int32, (WIN, WIN), 1)
            sub_mask = (c2 <= r2) & ((c2 // SUB) == (r2 // SUB))
            for h in range(G_HEADS):
                outs = []
                for w in range(tt // WIN):
                    rs = slice(w * WIN, (w + 1) * WIN)
                    a_full = jnp.dot(p_bf[rs], rsel[h], preferred_element_type=_F32)
                    a_h = jnp.where(sub_mask, a_full, 0.0)
                    outs.append(_dot(a_h, gv[rs, h * G_DV:(h + 1) * G_DV]))
                o_diag[h] = outs[0] if len(outs) == 1 else jnp.concatenate(outs, axis=0)
                yield

        lane_lo = lax.broadcasted_iota(jnp.int32, (SUB, LANE), 1) < G_DK
        o_int = [[None] * nsub for _ in range(G_HEADS)]

        def gla_rec(p):
            ps = slice(p * LANE, (p + 1) * LANE)
            st = s_st[p]
            v0 = gv[:, (2 * p) * G_DV:(2 * p + 1) * G_DV]
            v1 = gv[:, (2 * p + 1) * G_DV:(2 * p + 2) * G_DV]
            for j in range(nsub):
                js = slice(j * SUB, (j + 1) * SUB)
                qj, kj = qh[js, ps], kh[js, ps]
                qq = jnp.concatenate([jnp.where(lane_lo, qj, 0.0), jnp.where(lane_lo, 0.0, qj)], axis=0)
                kk = jnp.concatenate([jnp.where(lane_lo, kj, 0.0), jnp.where(lane_lo, 0.0, kj)], axis=0)
                vv = jnp.concatenate([v0[js], v1[js]], axis=0)
                oj = _dot_nt(qq, st)
                o_int[2 * p][j] = oj[0:SUB]
                o_int[2 * p + 1][j] = oj[SUB:2 * SUB]
                st = eg[j * SUB:j * SUB + 1, ps] * st + _dot_tn(vv, kk)
                yield
            s_st[p] = st

        _round_robin([in_proj()] + [mlstm_head(h) for h in range(M_HEADS)] + [gla_diag()]
                     + [gla_rec(p) for p in range(G_HEADS // 2)])

        gn = g_norm_g[...]
        g_outs = []
        for h in range(G_HEADS):
            hs = slice(h * G_DV, (h + 1) * G_DV)
            og = o_diag[h] + jnp.concatenate(o_int[h], axis=0)
            rms = lax.rsqrt(jnp.mean(og * og, axis=1, keepdims=True) + LN_EPS)
            gate = gg[:, hs]
            g_outs.append(gate * _sigmoid(gate) * (og * rms * gn[:, hs]))

        y = jnp.concatenate(m_outs + g_outs, axis=1)
        mix = _dot(y, w_out[...])
        o_ref[0] = _layer_norm(ALPHA * h0 + mix, ln1_g[...], ln1_b[...])

    for par in range(2):
        pl.when(g % 2 == par)(functools.partial(step, par))


def _mixer_constants(tt):
    r = np.arange(tt)
    tri64 = ((r[None, :] <= r[:, None]) & (r[None, :] // CHUNK == r[:, None] // CHUNK))
    same16 = (r[None, :] // SUB == r[:, None] // SUB)
    tri16 = (r[None, :] <= r[:, None]) & same16
    gw = G_HEADS * G_DK
    gsel = np.zeros((SUB, gw, LANE), np.float32)
    for s in range(SUB):
        for h in range(G_HEADS):
            gsel[s, h * G_DK:(h + 1) * G_DK, h * SUB + s] = 1.0
    rsel = np.zeros((G_HEADS, LANE, WIN), np.float32)
    for h in range(G_HEADS):
        for s in range(SUB):
            rsel[h, h * SUB + s, s::SUB] = 1.0
    to_bf = lambda a: jnp.asarray(a.astype(np.float32), dtype=_BF)
    return to_bf(tri64), to_bf(tri16), to_bf(same16), to_bf(gsel), to_bf(rsel)


def _full(shape):
    nd = len(shape)
    return pl.BlockSpec(shape, lambda b, i, _nd=nd: (0,) * _nd)


def _mixer_call(x, lnin_g, lnin_b, w_m, w_g, w_gate, conv_w, conv_b, gate_bias, m_norm_g, glw, glb, g_norm_g,
                w_out, ln1_g, ln1_b, tt):
    bsz, t, d = x.shape
    consts = _mixer_constants(tt)
    params = (lnin_g, lnin_b, w_m, w_g, w_gate, conv_w, conv_b, gate_bias, m_norm_g, glw, glb, g_norm_g,
              w_out, ln1_g, ln1_b) + consts
    nt = t // tt
    ntiles = bsz * nt
    in_tile = pl.BlockSpec((1, tt, d), lambda g: (jnp.minimum(g, ntiles - 1) // nt, jnp.minimum(g, ntiles - 1) % nt, 0))
    out_tile = pl.BlockSpec((1, tt, d), lambda g: (jnp.maximum(g - 1, 0) // nt, jnp.maximum(g - 1, 0) % nt, 0))
    full = lambda shape: pl.BlockSpec(shape, lambda g, _nd=len(shape): (0,) * _nd)
    return pl.pallas_call(
        functools.partial(_mixer_kernel, tt=tt, nt=nt),
        out_shape=jax.ShapeDtypeStruct((bsz, t, d), _F32),
        grid=(ntiles + 1,),
        in_specs=[in_tile] + [full(p.shape) for p in params],
        out_specs=out_tile,
        scratch_shapes=[
            pltpu.VMEM((tt + 8, 2 * 512), _F32),
            pltpu.VMEM((M_HEADS, M_DK, M_DV), _F32),
            pltpu.VMEM((M_HEADS, 1, M_DK), _F32),
            pltpu.VMEM((M_HEADS, 1, 1), _F32),
            pltpu.VMEM((G_HEADS // 2, G_DV, LANE), _F32),
            pltpu.VMEM((2, tt, IN_COLS_R), _F32),
            pltpu.VMEM((2, tt, D_MODEL), _F32),
        ],
        compiler_params=pltpu.CompilerParams(
            dimension_semantics=("arbitrary",), vmem_limit_bytes=VMEM_LIMIT),
        name="mixer",
    )(x, *params)


def _memkv_kernel(mem_ref, wk_ref, wv_ref, wq_ref, wo_ref, wqk_ref, vo_ref):
    m = mem_ref[0]
    k = _dot(m, wk_ref[...])
    v = _dot(m, wv_ref[...])
    for h in range(X_HEADS):
        hs = slice(h * X_DH, (h + 1) * X_DH)
        ms = slice(h * N_MEM, (h + 1) * N_MEM)
        wqk_ref[0, :, ms] = (_dot_nt(wq_ref[:, hs], k[:, hs]) * (X_DH ** -0.5)).astype(_BF)
        vo_ref[0, ms, :] = _dot(v[:, hs], wo_ref[hs, :]).astype(_BF)


def _memkv_call(mem, wk, wv, wq, wo):
    bsz, nm, d = mem.shape
    blk = pl.BlockSpec((1, nm, d), lambda b: (b, 0, 0))
    wspec = pl.BlockSpec((d, d), lambda b: (0, 0))
    return pl.pallas_call(
        _memkv_kernel,
        out_shape=(jax.ShapeDtypeStruct((bsz, d, X_HEADS * nm), _BF),
                   jax.ShapeDtypeStruct((bsz, X_HEADS * nm, d), _BF)),
        grid=(bsz,),
        in_specs=[blk, wspec, wspec, wspec, wspec],
        out_specs=(pl.BlockSpec((1, d, X_HEADS * nm), lambda b: (b, 0, 0)),
                   pl.BlockSpec((1, X_HEADS * nm, d), lambda b: (b, 0, 0))),
        compiler_params=pltpu.CompilerParams(
            dimension_semantics=("arbitrary",), vmem_limit_bytes=VMEM_LIMIT),
        name="memkv",
    )(mem, wk, wv, wq, wo)


def _attnmlp_kernel(h_ref, wqk_ref, vo_ref, ln2_g, ln2_b, w1, w2, ln3_g, ln3_b, o_ref):
    h1 = h_ref[0]
    s_all = _dot(h1, wqk_ref[0])
    probs = []
    for h in range(X_HEADS):
        s = s_all[:, h * N_MEM:(h + 1) * N_MEM]
        e = jnp.exp(s - jnp.max(s, axis=1, keepdims=True))
        probs.append((e * (1.0 / jnp.sum(e, axis=1, keepdims=True))).astype(_BF))
    xa = jnp.dot(jnp.concatenate(probs, axis=1), vo_ref[0], preferred_element_type=_F32)
    h2 = _layer_norm(ALPHA * h1 + xa, ln2_g[...], ln2_b[...])
    h2b = h2.astype(_BF)
    ff = jnp.zeros(h2.shape, _F32)
    for f in range(D_FF // FF_BLK):
        fs = slice(f * FF_BLK, (f + 1) * FF_BLK)
        hid = jnp.maximum(jnp.dot(h2b, w1[:, fs], preferred_element_type=_F32), 0.0)
        ff = ff + _dot(hid * hid, w2[fs, :])
    o_ref[0] = _layer_norm(ALPHA * h2 + ff, ln3_g[...], ln3_b[...])


def _attnmlp_call(h1, wqk, vo, ln2_g, ln2_b, w1, w2, ln3_g, ln3_b, tm):
    bsz, t, d = h1.shape
    tile = pl.BlockSpec((1, tm, d), lambda b, i: (b, i, 0))
    per_batch = lambda a: pl.BlockSpec((1,) + a.shape[1:], lambda b, i: (b, 0, 0))
    params = (ln2_g, ln2_b, w1, w2, ln3_g, ln3_b)
    return pl.pallas_call(
        _attnmlp_kernel,
        out_shape=jax.ShapeDtypeStruct((bsz, t, d), _F32),
        grid=(bsz, t // tm),
        in_specs=[tile, per_batch(wqk), per_batch(vo)] + [_full(p.shape) for p in params],
        out_specs=tile,
        compiler_params=pltpu.CompilerParams(
            dimension_semantics=("arbitrary", "arbitrary"), vmem_limit_bytes=VMEM_LIMIT),
        name="attnmlp",
    )(h1, wqk, vo, *params)


def _time_tile(t, target):
    tt = min(t, target)
    assert t % tt == 0 and tt % WIN == 0, (t, tt)
    return tt


def kernel(x, mem, ln_in_g, ln_in_b, w_in, conv_w, conv_b, m_i_bias, m_f_bias, m_norm_g, g_lr_w, g_lr_b, g_norm_g, w_out, ln1_g, ln1_b, x_wq, x_wk, x_wv, x_wo, ln2_g, ln2_b, w_ff1, w_ff2, ln3_g, ln3_b):
    assert w_in.shape[0] == DEPTH == 1
    row = lambda a: a.reshape(1, -1).astype(_F32)
    wi = w_in[0]
    w_m = wi[:, O_MQ:O_MI].astype(_BF)
    w_g = wi[:, O_GQ:O_GLR].astype(_BF)
    w_gate = jnp.concatenate([wi[:, O_MI:O_GQ], wi[:, O_GLR:O_GLR + G_RANK],
                              jnp.zeros((D_MODEL, LANE - 2 * M_HEADS - G_RANK), wi.dtype)], axis=1).astype(_BF)
    gate_bias = jnp.concatenate([m_i_bias[0], m_f_bias[0],
                                 jnp.zeros((LANE - 2 * M_HEADS,), _F32)]).reshape(1, LANE)
    glw = jnp.zeros((LANE, G_HEADS * G_DK), _F32).at[2 * M_HEADS:2 * M_HEADS + G_RANK].set(g_lr_w[0]).astype(_BF)
    tt = _time_tile(x.shape[1], 256)
    h1 = _mixer_call(x, row(ln_in_g), row(ln_in_b), w_m, w_g, w_gate, conv_w[0].astype(_F32), row(conv_b[0]),
                     gate_bias, row(m_norm_g[0]), glw, row(g_lr_b[0]), row(g_norm_g[0]), w_out[0].astype(_BF),
                     row(ln1_g[0]), row(ln1_b[0]), tt)
    wqk, vo = _memkv_call(mem, x_wk[0].astype(_BF), x_wv[0].astype(_BF), x_wq[0].astype(_BF), x_wo[0].astype(_BF))
    tm = _time_tile(x.shape[1], 512)
    return _attnmlp_call(h1, wqk, vo, row(ln2_g[0]), row(ln2_b[0]),
                         w_ff1[0].astype(_BF), w_ff2[0].astype(_BF), row(ln3_g[0]), row(ln3_b[0]), tm)
```

```python
import functools

import numpy as np
import jax
import jax.numpy as jnp
from jax import lax
from jax.experimental import pallas as pl
from jax.experimental.pallas import tpu as pltpu

D_MODEL = 1024
CHUNK = 64
SUB = 16
WIN = 2 * CHUNK
N_MEM = 256
M_HEADS, M_DK, M_DV = 4, 128, 128
G_HEADS, G_DK, G_DV = 4, 64, 128
G_RANK = 16
G_TAU = 16.0
X_HEADS = 4
X_DH = D_MODEL // X_HEADS
D_FF = 4 * D_MODEL
FF_BLK = 512
IN_BLK = 512
DEPTH = 1
ALPHA = (2.0 * DEPTH) ** 0.25
LN_EPS = 1e-5
LOG2E = 1.4426950408889634
LANE = 128

C_MQ, C_MK, C_MV, C_MO = 0, 512, 1024, 1536
C_GQ, C_GK, C_GV, C_GG = 2048, 2304, 2560, 3072
C_GATE = 3584
IN_COLS_R = C_GATE + LANE
O_MQ, O_MK, O_MV, O_MO, O_MI, O_MF = 0, 512, 1024, 1536, 2048, 2052
O_GQ, O_GK, O_GV, O_GG, O_GLR = 2056, 2312, 2568, 3080, 3592

VMEM_LIMIT = 56 * 1024 * 1024

_BF = jnp.bfloat16
_F32 = jnp.float32


def _dot(a, b):
    return jnp.dot(a.astype(_BF), b.astype(_BF), preferred_element_type=_F32)


def _dot_nt(a, b):
    return lax.dot_general(a.astype(_BF), b.astype(_BF), (((1,), (1,)), ((), ())),
                           preferred_element_type=_F32)


def _dot_tn(a, b):
    return lax.dot_general(a.astype(_BF), b.astype(_BF), (((0,), (0,)), ((), ())),
                           preferred_element_type=_F32)


def _split2(x):
    hi = x.astype(_BF)
    lo = (x - hi.astype(_F32)).astype(_BF)
    return jnp.concatenate([hi, lo], axis=1)


def _merge2(y, n):
    return y[:, 0:n] + y[:, n:2 * n]


def _layer_norm(x, g, b):
    mu = jnp.mean(x, axis=-1, keepdims=True)
    xc = x - mu
    var = jnp.mean(xc * xc, axis=-1, keepdims=True)
    return xc * lax.rsqrt(var + LN_EPS) * g + b


def _log_sigmoid(x):
    return -(jnp.maximum(-x, 0.0) + jnp.log(1.0 + jnp.exp(-jnp.abs(x))))


def _sigmoid(x):
    return 1.0 / (1.0 + jnp.exp(-x))


def _round_robin(chains):
    while chains:
        alive = []
        for ch in chains:
            try:
                next(ch)
                alive.append(ch)
            except StopIteration:
                pass
        chains = alive


def _mixer_kernel(x_ref, lnin_g, lnin_b, w_m, w_g, w_gate, conv_w, conv_b, gate_bias, m_norm_g, glw, glb, g_norm_g,
                  w_out, ln1_g, ln1_b, tri64, tri16, ones16, gsel, rsel,
                  o_ref, uext, c_st, n_st, m_st, s_st, proj_buf, h0_buf, *, tt, nt):
    g = pl.program_id(0)

    @pl.when(g == 0)
    def _():
        proj_buf[1] = jnp.zeros(proj_buf.shape[1:], _F32)
        h0_buf[1] = jnp.zeros(h0_buf.shape[1:], _F32)

    @pl.when((g == 0) | ((g - 1) % nt == 0))
    def _():
        uext[0:8, :] = jnp.zeros((8, 2 * 512), _F32)
        c_st[...] = jnp.zeros(c_st.shape, _F32)
        n_st[...] = jnp.zeros(n_st.shape, _F32)
        m_st[...] = jnp.zeros(m_st.shape, _F32)
        s_st[...] = jnp.zeros(s_st.shape, _F32)

    def step(slot_a):
        slot_b = 1 - slot_a

        def in_proj():
            h0n = _layer_norm(x_ref[0], lnin_g[...], lnin_b[...])
            h0_buf[slot_a] = h0n
            h0b = h0n.astype(_BF)
            yield
            for w_ref, base in ((w_m, C_MQ), (w_g, C_GQ), (w_gate, C_GATE)):
                ncol = w_ref.shape[1]
                for c0 in range(0, ncol, IN_BLK):
                    c1 = min(c0 + IN_BLK, ncol)
                    proj_buf[slot_a, :, base + c0:base + c1] = jnp.dot(h0b, w_ref[:, c0:c1],
                                                                       preferred_element_type=_F32)
                    yield

        pb = proj_buf.at[slot_b]
        h0 = h0_buf[slot_b]

        uext[8:8 + tt, :] = pb[:, C_MQ:C_MV]
        cw = conv_w[...]
        conv = (uext[5:5 + tt, :] * cw[0:1, :] + uext[6:6 + tt, :] * cw[1:2, :]
                + uext[7:7 + tt, :] * cw[2:3, :] + uext[8:8 + tt, :] * cw[3:4, :]) + conv_b[...]
        uext[0:8, :] = uext[tt:tt + 8, :]
        qk = conv * _sigmoid(conv)
        mq = qk[:, 0:512]
        mk = qk[:, 512:1024] * (M_DK ** -0.5)
        mv = pb[:, C_MV:C_MO]
        mo = pb[:, C_MO:C_GQ]

        gb = pb[:, C_GATE:C_GATE + LANE] + gate_bias[...]
        lane = lax.broadcasted_iota(jnp.int32, (tt, LANE), 1)
        lf = _log_sigmoid(gb)
        bcum = _merge2(_dot(tri64[...], _split2(lf)), LANE)
        vcol = jnp.where((lane >= M_HEADS) & (lane < 2 * M_HEADS), bcum, gb)
        vrow = vcol.T

        gw = G_HEADS * G_DK
        la = _log_sigmoid(_dot(gb, glw[...]) + glb[...]) * (1.0 / G_TAU)
        la2 = _split2(la)
        bc = _merge2(_dot(tri16[...], la2), gw)
        bend = _merge2(_dot(ones16[...], la2), gw)
        gq = pb[:, C_GQ:C_GK] * (G_DK ** -0.5)
        gk = pb[:, C_GK:C_GV]
        gv = pb[:, C_GV:C_GG]
        gg = pb[:, C_GG:C_GATE]
        qh = gq * jnp.exp(bc)
        kh = gk * jnp.exp(bend - bc)
        eg = jnp.exp(bend)
        bc2 = bc * LOG2E
        nsub = tt // SUB

        rr = lax.broadcasted_iota(jnp.int32, (WIN, WIN), 0)
        cc = lax.broadcasted_iota(jnp.int32, (WIN, WIN), 1)
        valid = (cc <= rr) & ((cc >= CHUNK) == (rr >= CHUNK))
        first = lax.broadcasted_iota(jnp.int32, (WIN, 1), 0) < CHUNK
        mg = m_norm_g[...]
        m_outs = [None] * M_HEADS

        def mlstm_head(h):
            hs = slice(h * M_DK, (h + 1) * M_DK)
            c_cur = c_st[h]
            n_cur = n_st[h]
            m_cur = m_st[h]
            win_outs = []
            for w in range(tt // WIN):
                rs = slice(w * WIN, (w + 1) * WIN)
                q, k, v = mq[rs, hs], mk[rs, hs], mv[rs, hs]
                li_col, b_col = vcol[rs, h:h + 1], vcol[rs, M_HEADS + h:M_HEADS + h + 1]
                li_row, b_row = vrow[h:h + 1, rs], vrow[M_HEADS + h:M_HEADS + h + 1, rs]
                dm = jnp.where(valid, b_col - b_row + li_row, -jnp.inf)
                md = jnp.max(dm, axis=1, keepdims=True)
                g0, g1 = b_col[CHUNK - 1:CHUNK, :], b_col[WIN - 1:WIN, :]
                a_col = jnp.where(first, g0, g1) - b_col + li_col
                yield
                am0 = jnp.max(a_col[0:CHUNK], axis=0, keepdims=True)
                am1 = jnp.max(a_col[CHUNK:WIN], axis=0, keepdims=True)
                wk = jnp.exp(a_col - jnp.where(first, am0, am1)) * k
                s_raw = _dot_nt(q, k)
                yield
                cprev, nprev, mprev = [], [], []
                for c, (gc, am) in enumerate(((g0, am0), (g1, am1))):
                    cs = slice(c * CHUNK, (c + 1) * CHUNK)
                    cprev.append(c_cur)
                    nprev.append(n_cur)
                    mprev.append(m_cur)
                    u_mat = _dot_tn(wk[cs], v[cs])
                    u_vec = jnp.sum(wk[cs], axis=0, keepdims=True)
                    m_new = jnp.maximum(gc + m_cur, am)
                    dec = jnp.exp(gc + m_cur - m_new)
                    inj = jnp.exp(am - m_new)
                    c_cur = dec * c_cur + inj * u_mat
                    n_cur = dec * n_cur + inj * u_vec
                    m_cur = m_new
                yield
                inter = b_col + jnp.where(first, mprev[0], mprev[1])
                m_t = jnp.maximum(inter, md)
                sc = jnp.exp(inter - m_t)
                s_mat = s_raw * jnp.exp(dm - m_t)
                yield
                sv = _dot(s_mat, v)
                qc = jnp.concatenate([_dot(q[0:CHUNK], cprev[0]), _dot(q[CHUNK:WIN], cprev[1])], axis=0)
                qn = jnp.sum(q * jnp.where(first, nprev[0], nprev[1]), axis=1, keepdims=True)
                ssum = jnp.sum(s_mat, axis=1, keepdims=True)
                yield
                num = sc * qc + sv
                den = sc * qn + ssum
                hh = num * (1.0 / jnp.maximum(jnp.abs(den), jnp.exp(-m_t)))
                mu = jnp.mean(hh, axis=1, keepdims=True)
                yield
                hc = hh - mu
                var = jnp.mean(hc * hc, axis=1, keepdims=True)
                yield
                win_outs.append(hc * lax.rsqrt(var + LN_EPS))
            c_st[h] = c_cur
            n_st[h] = n_cur
            m_st[h] = m_cur
            hn = win_outs[0] if len(win_outs) == 1 else jnp.concatenate(win_outs, axis=0)
            m_outs[h] = _sigmoid(mo[:, hs]) * (hn * mg[:, hs])

        o_diag = [None] * G_HEADS

        def gla_diag():
            gk3 = gk.reshape(nsub, SUB, gw)
            bc3 = bc2.reshape(nsub, SUB, gw)
            half = SUB // 2
            hi = lambda a: a.reshape(nsub, 2, half, a.shape[-1])[:, 1].reshape(nsub * half, a.shape[-1])
            gq_hi, bc2_hi = hi(gq), hi(bc2)
            p_acc = jnp.zeros((tt, LANE), _F32)
            p_hi = jnp.zeros((tt // 2, LANE), _F32)
            for s in range(SUB):
                if s < half:
                    kb = jnp.broadcast_to(gk3[:, s:s + 1, :], (nsub, SUB, gw)).reshape(tt, gw)
                    bb = jnp.broadcast_to(bc3[:, s:s + 1, :], (nsub, SUB, gw)).reshape(tt, gw)
                    y = gq * kb * jnp.exp2(jnp.minimum(bc2 - bb, 0.0))
                    p_acc = p_acc + _dot(y, gsel[s])
                else:
                    kb = jnp.broadcast_to(gk3[:, s:s + 1, :], (nsub, half, gw)).reshape(tt // 2, gw)
                    bb = jnp.broadcast_to(bc3[:, s:s + 1, :], (nsub, half, gw)).reshape(tt // 2, gw)
                    y = gq_hi * kb * jnp.exp2(jnp.minimum(bc2_hi - bb, 0.0))
                    p_hi = p_hi + _dot(y, gsel[s])
                yield
            p4 = p_acc.reshape(nsub, 2, half, LANE)
            p_acc = jnp.concatenate([p4[:, 0:1], p4[:, 1:2] + p_hi.reshape(nsub, 1, half, LANE)],
                                    axis=1).reshape(tt, LANE)
            p_bf = p_acc.astype(_BF)
            r2 = lax.broadcasted_iota(jnp.int32, (WIN, WIN), 0)
            c2 = lax.broadcasted_iota(jnp.int32, (WIN, WIN), 1)
            sub_mask = (c2 <= r2) & ((c2 // SUB) == (r2 // SUB))
            for h in range(G_HEADS):
                outs = []
                for w in range(tt // WIN):
                    rs = slice(w * WIN, (w + 1) * WIN)
                    a_full = jnp.dot(p_bf[rs], rsel[h], preferred_element_type=_F32)
                    a_h = jnp.where(sub_mask, a_full, 0.0)
                    outs.append(_dot(a_h, gv[rs, h * G_DV:(h + 1) * G_DV]))
                o_diag[h] = outs[0] if len(outs) == 1 else jnp.concatenate(outs, axis=0)
                yield

        lane_lo = lax.broadcasted_iota(jnp.int32, (SUB, LANE), 1) < G_DK
        o_int = [[None] * nsub for _ in range(G_HEADS)]

        def gla_rec(p):
            ps = slice(p * LANE, (p + 1) * LANE)
            st = s_st[p]
            v0 = gv[:, (2 * p) * G_DV:(2 * p + 1) * G_DV]
            v1 = gv[:, (2 * p + 1) * G_DV:(2 * p + 2) * G_DV]
            for j in range(nsub):
                js = slice(j * SUB, (j + 1) * SUB)
                qj, kj = qh[js, ps], kh[js, ps]
                qq = jnp.concatenate([jnp.where(lane_lo, qj, 0.0), jnp.where(lane_lo, 0.0, qj)], axis=0)
                kk = jnp.concatenate([jnp.where(lane_lo, kj, 0.0), jnp.where(lane_lo, 0.0, kj)], axis=0)
                vv = jnp.concatenate([v0[js], v1[js]], axis=0)
                oj = _dot_nt(qq, st)
                o_int[2 * p][j] = oj[0:SUB]
                o_int[2 * p + 1][j] = oj[SUB:2 * SUB]
                st = eg[j * SUB:j * SUB + 1, ps] * st + _dot_tn(vv, kk)
                yield
            s_st[p] = st

        _round_robin([in_proj()] + [mlstm_head(h) for h in range(M_HEADS)] + [gla_diag()]
                     + [gla_rec(p) for p in range(G_HEADS // 2)])

        gn = g_norm_g[...]
        g_outs = []
        for h in range(G_HEADS):
            hs = slice(h * G_DV, (h + 1) * G_DV)
            og = o_diag[h] + jnp.concatenate(o_int[h], axis=0)
            rms = lax.rsqrt(jnp.mean(og * og, axis=1, keepdims=True) + LN_EPS)
            gate = gg[:, hs]
            g_outs.append(gate * _sigmoid(gate) * (og * rms * gn[:, hs]))

        y = jnp.concatenate(m_outs + g_outs, axis=1)
        mix = _dot(y, w_out[...])
        o_ref[0] = _layer_norm(ALPHA * h0 + mix, ln1_g[...], ln1_b[...])

    for par in range(2):
        pl.when(g % 2 == par)(functools.partial(step, par))


def _mixer_constants(tt):
    r = np.arange(tt)
    tri64 = ((r[None, :] <= r[:, None]) & (r[None, :] // CHUNK == r[:, None] // CHUNK))
    same16 = (r[None, :] // SUB == r[:, None] // SUB)
    tri16 = (r[None, :] <= r[:, None]) & same16
    gw = G_HEADS * G_DK
    gsel = np.zeros((SUB, gw, LANE), np.float32)
    for s in range(SUB):
        for h in range(G_HEADS):
            gsel[s, h * G_DK:(h + 1) * G_DK, h * SUB + s] = 1.0
    rsel = np.zeros((G_HEADS, LANE, WIN), np.float32)
    for h in range(G_HEADS):
        for s in range(SUB):
            rsel[h, h * SUB + s, s::SUB] = 1.0
    to_bf = lambda a: jnp.asarray(a.astype(np.float32), dtype=_BF)
    return to_bf(tri64), to_bf(tri16), to_bf(same16), to_bf(gsel), to_bf(rsel)


def _full(shape):
    nd = len(shape)
    return pl.BlockSpec(shape, lambda b, i, _nd=nd: (0,) * _nd)


def _mixer_call(x, lnin_g, lnin_b, w_m, w_g, w_gate, conv_w, conv_b, gate_bias, m_norm_g, glw, glb, g_norm_g,
                w_out, ln1_g, ln1_b, tt):
    bsz, t, d = x.shape
    consts = _mixer_constants(tt)
    params = (lnin_g, lnin_b, w_m, w_g, w_gate, conv_w, conv_b, gate_bias, m_norm_g, glw, glb, g_norm_g,
              w_out, ln1_g, ln1_b) + consts
    nt = t // tt
    ntiles = bsz * nt
    in_tile = pl.BlockSpec((1, tt, d), lambda g: (jnp.minimum(g, ntiles - 1) // nt, jnp.minimum(g, ntiles - 1) % nt, 0))
    out_tile = pl.BlockSpec((1, tt, d), lambda g: (jnp.maximum(g - 1, 0) // nt, jnp.maximum(g - 1, 0) % nt, 0))
    full = lambda shape: pl.BlockSpec(shape, lambda g, _nd=len(shape): (0,) * _nd)
    return pl.pallas_call(
        functools.partial(_mixer_kernel, tt=tt, nt=nt),
        out_shape=jax.ShapeDtypeStruct((bsz, t, d), _F32),
        grid=(ntiles + 1,),
        in_specs=[in_tile] + [full(p.shape) for p in params],
        out_specs=out_tile,
        scratch_shapes=[
            pltpu.VMEM((tt + 8, 2 * 512), _F32),
            pltpu.VMEM((M_HEADS, M_DK, M_DV), _F32),
            pltpu.VMEM((M_HEADS, 1, M_DK), _F32),
            pltpu.VMEM((M_HEADS, 1, 1), _F32),
            pltpu.VMEM((G_HEADS // 2, G_DV, LANE), _F32),
            pltpu.VMEM((2, tt, IN_COLS_R), _F32),
            pltpu.VMEM((2, tt, D_MODEL), _F32),
        ],
        compiler_params=pltpu.CompilerParams(
            dimension_semantics=("arbitrary",), vmem_limit_bytes=VMEM_LIMIT),
        name="mixer",
    )(x, *params)


def _memkv_kernel(mem_ref, wk_ref, wv_ref, wq_ref, wo_ref, wqk_ref, vo_ref):
    m = mem_ref[0]
    k = _dot(m, wk_ref[...])
    v = _dot(m, wv_ref[...])
    for h in range(X_HEADS):
        hs = slice(h * X_DH, (h + 1) * X_DH)
        ms = slice(h * N_MEM, (h + 1) * N_MEM)
        wqk_ref[0, :, ms] = (_dot_nt(wq_ref[:, hs], k[:, hs]) * (X_DH ** -0.5)).astype(_BF)
        vo_ref[0, ms, :] = _dot(v[:, hs], wo_ref[hs, :]).astype(_BF)


def _memkv_call(mem, wk, wv, wq, wo):
    bsz, nm, d = mem.shape
    blk = pl.BlockSpec((1, nm, d), lambda b: (b, 0, 0))
    wspec = pl.BlockSpec((d, d), lambda b: (0, 0))
    return pl.pallas_call(
        _memkv_kernel,
        out_shape=(jax.ShapeDtypeStruct((bsz, d, X_HEADS * nm), _BF),
                   jax.ShapeDtypeStruct((bsz, X_HEADS * nm, d), _BF)),
        grid=(bsz,),
        in_specs=[blk, wspec, wspec, wspec, wspec],
        out_specs=(pl.BlockSpec((1, d, X_HEADS * nm), lambda b: (b, 0, 0)),
                   pl.BlockSpec((1, X_HEADS * nm, d), lambda b: (b, 0, 0))),
        compiler_params=pltpu.CompilerParams(
            dimension_semantics=("arbitrary",), vmem_limit_bytes=VMEM_LIMIT),
        name="memkv",
    )(mem, wk, wv, wq, wo)


def _attnmlp_kernel(h_ref, wqk_ref, vo_ref, ln2_g, ln2_b, w1, w2, ln3_g, ln3_b, o_ref):
    h1 = h_ref[0]
    s_all = _dot(h1, wqk_ref[0])
    probs = []
    for h in range(X_HEADS):
        s = s_all[:, h * N_MEM:(h + 1) * N_MEM]
        e = jnp.exp(s - jnp.max(s, axis=1, keepdims=True))
        probs.append((e * (1.0 / jnp.sum(e, axis=1, keepdims=True))).astype(_BF))
    xa = jnp.dot(jnp.concatenate(probs, axis=1), vo_ref[0], preferred_element_type=_F32)
    h2 = _layer_norm(ALPHA * h1 + xa, ln2_g[...], ln2_b[...])
    h2b = h2.astype(_BF)
    ff = jnp.zeros(h2.shape, _F32)
    for f in range(D_FF // FF_BLK):
        fs = slice(f * FF_BLK, (f + 1) * FF_BLK)
        hid = jnp.maximum(jnp.dot(h2b, w1[:, fs], preferred_element_type=_F32), 0.0)
        ff = ff + _dot(hid * hid, w2[fs, :])
    o_ref[0] = _layer_norm(ALPHA * h2 + ff, ln3_g[...], ln3_b[...])


def _attnmlp_call(h1, wqk, vo, ln2_g, ln2_b, w1, w2, ln3_g, ln3_b, tm):
    bsz, t, d = h1.shape
    tile = pl.BlockSpec((1, tm, d), lambda b, i: (b, i, 0))
    per_batch = lambda a: pl.BlockSpec((1,) + a.shape[1:], lambda b, i: (b, 0, 0))
    params = (ln2_g, ln2_b, w1, w2, ln3_g, ln3_b)
    return pl.pallas_call(
        _attnmlp_kernel,
        out_shape=jax.ShapeDtypeStruct((bsz, t, d), _F32),
        grid=(bsz, t // tm),
        in_specs=[tile, per_batch(wqk), per_batch(vo)] + [_full(p.shape) for p in params],
        out_specs=tile,
        compiler_params=pltpu.CompilerParams(
            dimension_semantics=("arbitrary", "arbitrary"), vmem_limit_bytes=VMEM_LIMIT),
        name="attnmlp",
    )(h1, wqk, vo, *params)


def _time_tile(t, target):
    tt = min(t, target)
    assert t % tt == 0 and tt % WIN == 0, (t, tt)
    return tt


def kernel(x, mem, ln_in_g, ln_in_b, w_in, conv_w, conv_b, m_i_bias, m_f_bias, m_norm_g, g_lr_w, g_lr_b, g_norm_g, w_out, ln1_g, ln1_b, x_wq, x_wk, x_wv, x_wo, ln2_g, ln2_b, w_ff1, w_ff2, ln3_g, ln3_b):
    assert w_in.shape[0] == DEPTH == 1
    row = lambda a: a.reshape(1, -1).astype(_F32)
    wi = w_in[0]
    w_m = wi[:, O_MQ:O_MI].astype(_BF)
    w_g = wi[:, O_GQ:O_GLR].astype(_BF)
    w_gate = jnp.concatenate([wi[:, O_MI:O_GQ], wi[:, O_GLR:O_GLR + G_RANK],
                              jnp.zeros((D_MODEL, LANE - 2 * M_HEADS - G_RANK), wi.dtype)], axis=1).astype(_BF)
    gate_bias = jnp.concatenate([m_i_bias[0], m_f_bias[0],
                                 jnp.zeros((LANE - 2 * M_HEADS,), _F32)]).reshape(1, LANE)
    glw = jnp.zeros((LANE, G_HEADS * G_DK), _F32).at[2 * M_HEADS:2 * M_HEADS + G_RANK].set(g_lr_w[0]).astype(_BF)
    tt = _time_tile(x.shape[1], 512)
    h1 = _mixer_call(x, row(ln_in_g), row(ln_in_b), w_m, w_g, w_gate, conv_w[0].astype(_F32), row(conv_b[0]),
                     gate_bias, row(m_norm_g[0]), glw, row(g_lr_b[0]), row(g_norm_g[0]), w_out[0].astype(_BF),
                     row(ln1_g[0]), row(ln1_b[0]), tt)
    wqk, vo = _memkv_call(mem, x_wk[0].astype(_BF), x_wv[0].astype(_BF), x_wq[0].astype(_BF), x_wo[0].astype(_BF))
    tm = _time_tile(x.shape[1], 512)
    return _attnmlp_call(h1, wqk, vo, row(ln2_g[0]), row(ln2_b[0]),
                         w_ff1[0].astype(_BF), w_ff2[0].astype(_BF), row(ln3_g[0]), row(ln3_b[0]), tm)
```

```python
import functools

import numpy as np
import jax
import jax.numpy as jnp
from jax import lax
from jax.experimental import pallas as pl
from jax.experimental.pallas import tpu as pltpu

D_MODEL = 1024
CHUNK = 64
SUB = 16
WIN = 2 * CHUNK
N_MEM = 256
M_HEADS, M_DK, M_DV = 4, 128, 128
G_HEADS, G_DK, G_DV = 4, 64, 128
G_RANK = 16
G_TAU = 16.0
X_HEADS = 4
X_DH = D_MODEL // X_HEADS
D_FF = 4 * D_MODEL
FF_BLK = 512
IN_BLK = 512
DEPTH = 1
ALPHA = (2.0 * DEPTH) ** 0.25
LN_EPS = 1e-5
LOG2E = 1.4426950408889634
LANE = 128

C_MQ, C_MK, C_MV, C_MO = 0, 512, 1024, 1536
C_GQ, C_GK, C_GV, C_GG = 2048, 2304, 2560, 3072
C_GATE = 3584
IN_COLS_R = C_GATE + LANE
O_MQ, O_MK, O_MV, O_MO, O_MI, O_MF = 0, 512, 1024, 1536, 2048, 2052
O_GQ, O_GK, O_GV, O_GG, O_GLR = 2056, 2312, 2568, 3080, 3592

VMEM_LIMIT = 56 * 1024 * 1024

_BF = jnp.bfloat16
_F32 = jnp.float32


def _dot(a, b):
    return jnp.dot(a.astype(_BF), b.astype(_BF), preferred_element_type=_F32)


def _dot_nt(a, b):
    return lax.dot_general(a.astype(_BF), b.astype(_BF), (((1,), (1,)), ((), ())),
                           preferred_element_type=_F32)


def _dot_tn(a, b):
    return lax.dot_general(a.astype(_BF), b.astype(_BF), (((0,), (0,)), ((), ())),
                           preferred_element_type=_F32)


def _split2(x):
    hi = x.astype(_BF)
    lo = (x - hi.astype(_F32)).astype(_BF)
    return jnp.concatenate([hi, lo], axis=1)


def _merge2(y, n):
    return y[:, 0:n] + y[:, n:2 * n]


def _layer_norm(x, g, b):
    mu = jnp.mean(x, axis=-1, keepdims=True)
    xc = x - mu
    var = jnp.mean(xc * xc, axis=-1, keepdims=True)
    return xc * lax.rsqrt(var + LN_EPS) * g + b


def _log_sigmoid(x):
    return -(jnp.maximum(-x, 0.0) + jnp.log(1.0 + jnp.exp(-jnp.abs(x))))


def _sigmoid(x):
    return 1.0 / (1.0 + jnp.exp(-x))


def _round_robin(chains):
    while chains:
        alive = []
        for ch in chains:
            try:
                next(ch)
                alive.append(ch)
            except StopIteration:
                pass
        chains = alive


def _mixer_kernel(x_ref, lnin_g, lnin_b, w_m, w_g, w_gate, conv_w, conv_b, gate_bias, m_norm_g, glw, glb, g_norm_g,
                  w_out, ln1_g, ln1_b, tri64, tri16, ones16, gsel, rsel,
                  o_ref, uext, c_st, n_st, m_st, s_st, proj_buf, h0_buf, *, tt, nt):
    g = pl.program_id(0)

    @pl.when(g == 0)
    def _():
        proj_buf[1] = jnp.zeros(proj_buf.shape[1:], _F32)
        h0_buf[1] = jnp.zeros(h0_buf.shape[1:], _F32)

    @pl.when((g == 0) | ((g - 1) % nt == 0))
    def _():
        uext[0:8, :] = jnp.zeros((8, 2 * 512), _F32)
        c_st[...] = jnp.zeros(c_st.shape, _F32)
        n_st[...] = jnp.zeros(n_st.shape, _F32)
        m_st[...] = jnp.zeros(m_st.shape, _F32)
        s_st[...] = jnp.zeros(s_st.shape, _F32)

    def step(slot_a):
        slot_b = 1 - slot_a

        def in_proj():
            h0n = _layer_norm(x_ref[0], lnin_g[...], lnin_b[...])
            h0_buf[slot_a] = h0n
            h0b = h0n.astype(_BF)
            yield
            for w_ref, base in ((w_m, C_MQ), (w_g, C_GQ), (w_gate, C_GATE)):
                ncol = w_ref.shape[1]
                for c0 in range(0, ncol, IN_BLK):
                    c1 = min(c0 + IN_BLK, ncol)
                    proj_buf[slot_a, :, base + c0:base + c1] = jnp.dot(h0b, w_ref[:, c0:c1],
                                                                       preferred_element_type=_F32)
                    yield

        pb = proj_buf.at[slot_b]
        h0 = h0_buf[slot_b]

        uext[8:8 + tt, :] = pb[:, C_MQ:C_MV]
        cw = conv_w[...]
        conv = (uext[5:5 + tt, :] * cw[0:1, :] + uext[6:6 + tt, :] * cw[1:2, :]
                + uext[7:7 + tt, :] * cw[2:3, :] + uext[8:8 + tt, :] * cw[3:4, :]) + conv_b[...]
        uext[0:8, :] = uext[tt:tt + 8, :]
        qk = conv * _sigmoid(conv)
        mq = qk[:, 0:512]
        mk = qk[:, 512:1024] * (M_DK ** -0.5)
        mv = pb[:, C_MV:C_MO]
        mo = pb[:, C_MO:C_GQ]

        gb = pb[:, C_GATE:C_GATE + LANE] + gate_bias[...]
        lane = lax.broadcasted_iota(jnp.int32, (tt, LANE), 1)
        lf = _log_sigmoid(gb)
        bcum = _merge2(_dot(tri64[...], _split2(lf)), LANE)
        vcol = jnp.where((lane >= M_HEADS) & (lane < 2 * M_HEADS), bcum, gb) * LOG2E
        vrow = vcol.T

        gw = G_HEADS * G_DK
        la = _log_sigmoid(_dot(gb, glw[...]) + glb[...]) * (1.0 / G_TAU)
        la2 = _split2(la)
        bc = _merge2(_dot(tri16[...], la2), gw)
        bend = _merge2(_dot(ones16[...], la2), gw)
        gq = pb[:, C_GQ:C_GK] * (G_DK ** -0.5)
        gk = pb[:, C_GK:C_GV]
        gv = pb[:, C_GV:C_GG]
        gg = pb[:, C_GG:C_GATE]
        bc2 = bc * LOG2E
        bend2 = bend * LOG2E
        qh = gq * jnp.exp2(bc2)
        kh = gk * jnp.exp2(bend2 - bc2)
        eg = jnp.exp2(bend2)
        nsub = tt // SUB

        rr = lax.broadcasted_iota(jnp.int32, (WIN, WIN), 0)
        cc = lax.broadcasted_iota(jnp.int32, (WIN, WIN), 1)
        valid = (cc <= rr) & ((cc >= CHUNK) == (rr >= CHUNK))
        first = lax.broadcasted_iota(jnp.int32, (WIN, 1), 0) < CHUNK
        mg = m_norm_g[...]
        m_outs = [None] * M_HEADS

        def mlstm_head(h):
            hs = slice(h * M_DK, (h + 1) * M_DK)
            c_cur = c_st[h]
            n_cur = n_st[h]
            m_cur = m_st[h]
            win_outs = []
            for w in range(tt // WIN):
                rs = slice(w * WIN, (w + 1) * WIN)
                q, k, v = mq[rs, hs], mk[rs, hs], mv[rs, hs]
                li_col, b_col = vcol[rs, h:h + 1], vcol[rs, M_HEADS + h:M_HEADS + h + 1]
                li_row, b_row = vrow[h:h + 1, rs], vrow[M_HEADS + h:M_HEADS + h + 1, rs]
                dm = jnp.where(valid, b_col - b_row + li_row, -jnp.inf)
                md = jnp.max(dm, axis=1, keepdims=True)
                g0, g1 = b_col[CHUNK - 1:CHUNK, :], b_col[WIN - 1:WIN, :]
                a_col = jnp.where(first, g0, g1) - b_col + li_col
                yield
                am0 = jnp.max(a_col[0:CHUNK], axis=0, keepdims=True)
                am1 = jnp.max(a_col[CHUNK:WIN], axis=0, keepdims=True)
                wk = jnp.exp2(a_col - jnp.where(first, am0, am1)) * k
                s_raw = _dot_nt(q, k)
                yield
                cprev, nprev, mprev = [], [], []
                for c, (gc, am) in enumerate(((g0, am0), (g1, am1))):
                    cs = slice(c * CHUNK, (c + 1) * CHUNK)
                    cprev.append(c_cur)
                    nprev.append(n_cur)
                    mprev.append(m_cur)
                    u_mat = _dot_tn(wk[cs], v[cs])
                    u_vec = jnp.sum(wk[cs], axis=0, keepdims=True)
                    m_new = jnp.maximum(gc + m_cur, am)
                    dec = jnp.exp2(gc + m_cur - m_new)
                    inj = jnp.exp2(am - m_new)
                    c_cur = dec * c_cur + inj * u_mat
                    n_cur = dec * n_cur + inj * u_vec
                    m_cur = m_new
                yield
                inter = b_col + jnp.where(first, mprev[0], mprev[1])
                m_t = jnp.maximum(inter, md)
                sc = jnp.exp2(inter - m_t)
                s_mat = s_raw * jnp.exp2(dm - m_t)
                yield
                sv = _dot(s_mat, v)
                qc = jnp.concatenate([_dot(q[0:CHUNK], cprev[0]), _dot(q[CHUNK:WIN], cprev[1])], axis=0)
                qn = jnp.sum(q * jnp.where(first, nprev[0], nprev[1]), axis=1, keepdims=True)
                ssum = jnp.sum(s_mat, axis=1, keepdims=True)
                yield
                num = sc * qc + sv
                den = sc * qn + ssum
                hh = num * (1.0 / jnp.maximum(jnp.abs(den), jnp.exp2(-m_t)))
                mu = jnp.mean(hh, axis=1, keepdims=True)
                yield
                hc = hh - mu
                var = jnp.mean(hc * hc, axis=1, keepdims=True)
                yield
                win_outs.append(hc * lax.rsqrt(var + LN_EPS))
            c_st[h] = c_cur
            n_st[h] = n_cur
            m_st[h] = m_cur
            hn = win_outs[0] if len(win_outs) == 1 else jnp.concatenate(win_outs, axis=0)
            m_outs[h] = _sigmoid(mo[:, hs]) * (hn * mg[:, hs])

        o_diag = [None] * G_HEADS

        def gla_diag():
            gk3 = gk.reshape(nsub, SUB, gw)
            bc3 = bc2.reshape(nsub, SUB, gw)
            half = SUB // 2
            hi = lambda a: a.reshape(nsub, 2, half, a.shape[-1])[:, 1].reshape(nsub * half, a.shape[-1])
            gq_hi, bc2_hi = hi(gq), hi(bc2)
            p_acc = jnp.zeros((tt, LANE), _F32)
            p_hi = jnp.zeros((tt // 2, LANE), _F32)
            for s in range(SUB):
                if s < half:
                    kb = jnp.broadcast_to(gk3[:, s:s + 1, :], (nsub, SUB, gw)).reshape(tt, gw)
                    bb = jnp.broadcast_to(bc3[:, s:s + 1, :], (nsub, SUB, gw)).reshape(tt, gw)
                    y = gq * kb * jnp.exp2(jnp.minimum(bc2 - bb, 0.0))
                    p_acc = p_acc + _dot(y, gsel[s])
                else:
                    kb = jnp.broadcast_to(gk3[:, s:s + 1, :], (nsub, half, gw)).reshape(tt // 2, gw)
                    bb = jnp.broadcast_to(bc3[:, s:s + 1, :], (nsub, half, gw)).reshape(tt // 2, gw)
                    y = gq_hi * kb * jnp.exp2(jnp.minimum(bc2_hi - bb, 0.0))
                    p_hi = p_hi + _dot(y, gsel[s])
                yield
            p4 = p_acc.reshape(nsub, 2, half, LANE)
            p_acc = jnp.concatenate([p4[:, 0:1], p4[:, 1:2] + p_hi.reshape(nsub, 1, half, LANE)],
                                    axis=1).reshape(tt, LANE)
            p_bf = p_acc.astype(_BF)
            r2 = lax.broadcasted_iota(jnp.int32, (WIN, WIN), 0)
            c2 = lax.broadcasted_iota(jnp.int32, (WIN, WIN), 1)
            sub_mask = (c2 <= r2) & ((c2 // SUB) == (r2 // SUB))
            for h in range(G_HEADS):
                outs = []
                for w in range(tt // WIN):
                    rs = slice(w * WIN, (w + 1) * WIN)
                    a_full = jnp.dot(p_bf[rs], rsel[h], preferred_element_type=_F32)
                    a_h = jnp.where(sub_mask, a_full, 0.0)
                    outs.append(_dot(a_h, gv[rs, h * G_DV:(h + 1) * G_DV]))
                o_diag[h] = outs[0] if len(outs) == 1 else jnp.concatenate(outs, axis=0)
                yield

        head_of_lane = lax.broadcasted_iota(jnp.int32, (SUB, gw), 1) // G_DK
        o_int = [[None] * nsub for _ in range(G_HEADS)]

        def gla_rec():
            st = s_st[...]
            for j in range(nsub):
                js = slice(j * SUB, (j + 1) * SUB)
                qj, kj = qh[js], kh[js]
                qq = jnp.concatenate([jnp.where(head_of_lane == h, qj, 0.0) for h in range(G_HEADS)], axis=0)
                kk = jnp.concatenate([jnp.where(head_of_lane == h, kj, 0.0) for h in range(G_HEADS)], axis=0)
                vv = jnp.concatenate([gv[js, h * G_DV:(h + 1) * G_DV] for h in range(G_HEADS)], axis=0)
                oj = _dot_nt(qq, st)
                for h in range(G_HEADS):
                    o_int[h][j] = oj[h * SUB:(h + 1) * SUB]
                st = eg[j * SUB:j * SUB + 1, :] * st + _dot_tn(vv, kk)
                yield
            s_st[...] = st

        _round_robin([in_proj(), gla_diag(), gla_rec()] + [mlstm_head(h) for h in range(M_HEADS)])

        gn = g_norm_g[...]
        g_outs = []
        for h in range(G_HEADS):
            hs = slice(h * G_DV, (h + 1) * G_DV)
            og = o_diag[h] + jnp.concatenate(o_int[h], axis=0)
            rms = lax.rsqrt(jnp.mean(og * og, axis=1, keepdims=True) + LN_EPS)
            gate = gg[:, hs]
            g_outs.append(gate * _sigmoid(gate) * (og * rms * gn[:, hs]))

        y = jnp.concatenate(m_outs + g_outs, axis=1)
        mix = _dot(y, w_out[...])
        o_ref[0] = _layer_norm(ALPHA * h0 + mix, ln1_g[...], ln1_b[...])

    for par in range(2):
        pl.when(g % 2 == par)(functools.partial(step, par))


def _mixer_constants(tt):
    r = np.arange(tt)
    tri64 = ((r[None, :] <= r[:, None]) & (r[None, :] // CHUNK == r[:, None] // CHUNK))
    same16 = (r[None, :] // SUB == r[:, None] // SUB)
    tri16 = (r[None, :] <= r[:, None]) & same16
    gw = G_HEADS * G_DK
    gsel = np.zeros((SUB, gw, LANE), np.float32)
    for s in range(SUB):
        for h in range(G_HEADS):
            gsel[s, h * G_DK:(h + 1) * G_DK, h * SUB + s] = 1.0
    rsel = np.zeros((G_HEADS, LANE, WIN), np.float32)
    for h in range(G_HEADS):
        for s in range(SUB):
            rsel[h, h * SUB + s, s::SUB] = 1.0
    to_bf = lambda a: jnp.asarray(a.astype(np.float32), dtype=_BF)
    return to_bf(tri64), to_bf(tri16), to_bf(same16), to_bf(gsel), to_bf(rsel)


def _full(shape):
    nd = len(shape)
    return pl.BlockSpec(shape, lambda b, i, _nd=nd: (0,) * _nd)


def _mixer_call(x, lnin_g, lnin_b, w_m, w_g, w_gate, conv_w, conv_b, gate_bias, m_norm_g, glw, glb, g_norm_g,
                w_out, ln1_g, ln1_b, tt):
    bsz, t, d = x.shape
    consts = _mixer_constants(tt)
    params = (lnin_g, lnin_b, w_m, w_g, w_gate, conv_w, conv_b, gate_bias, m_norm_g, glw, glb, g_norm_g,
              w_out, ln1_g, ln1_b) + consts
    nt = t // tt
    ntiles = bsz * nt
    in_tile = pl.BlockSpec((1, tt, d), lambda g: (jnp.minimum(g, ntiles - 1) // nt, jnp.minimum(g, ntiles - 1) % nt, 0))
    out_tile = pl.BlockSpec((1, tt, d), lambda g: (jnp.maximum(g - 1, 0) // nt, jnp.maximum(g - 1, 0) % nt, 0))
    full = lambda shape: pl.BlockSpec(shape, lambda g, _nd=len(shape): (0,) * _nd)
    return pl.pallas_call(
        functools.partial(_mixer_kernel, tt=tt, nt=nt),
        out_shape=jax.ShapeDtypeStruct((bsz, t, d), _F32),
        grid=(ntiles + 1,),
        in_specs=[in_tile] + [full(p.shape) for p in params],
        out_specs=out_tile,
        scratch_shapes=[
            pltpu.VMEM((tt + 8, 2 * 512), _F32),
            pltpu.VMEM((M_HEADS, M_DK, M_DV), _F32),
            pltpu.VMEM((M_HEADS, 1, M_DK), _F32),
            pltpu.VMEM((M_HEADS, 1, 1), _F32),
            pltpu.VMEM((G_DV, G_HEADS * G_DK), _F32),
            pltpu.VMEM((2, tt, IN_COLS_R), _F32),
            pltpu.VMEM((2, tt, D_MODEL), _F32),
        ],
        compiler_params=pltpu.CompilerParams(
            dimension_semantics=("arbitrary",), vmem_limit_bytes=VMEM_LIMIT),
        name="mixer",
    )(x, *params)


def _memkv_kernel(mem_ref, wk_ref, wv_ref, wq_ref, wo_ref, wqk_ref, vo_ref):
    m = mem_ref[0]
    k = _dot(m, wk_ref[...])
    v = _dot(m, wv_ref[...])
    for h in range(X_HEADS):
        hs = slice(h * X_DH, (h + 1) * X_DH)
        ms = slice(h * N_MEM, (h + 1) * N_MEM)
        wqk_ref[0, :, ms] = (_dot_nt(wq_ref[:, hs], k[:, hs]) * (X_DH ** -0.5)).astype(_BF)
        vo_ref[0, ms, :] = _dot(v[:, hs], wo_ref[hs, :]).astype(_BF)


def _memkv_call(mem, wk, wv, wq, wo):
    bsz, nm, d = mem.shape
    blk = pl.BlockSpec((1, nm, d), lambda b: (b, 0, 0))
    wspec = pl.BlockSpec((d, d), lambda b: (0, 0))
    return pl.pallas_call(
        _memkv_kernel,
        out_shape=(jax.ShapeDtypeStruct((bsz, d, X_HEADS * nm), _BF),
                   jax.ShapeDtypeStruct((bsz, X_HEADS * nm, d), _BF)),
        grid=(bsz,),
        in_specs=[blk, wspec, wspec, wspec, wspec],
        out_specs=(pl.BlockSpec((1, d, X_HEADS * nm), lambda b: (b, 0, 0)),
                   pl.BlockSpec((1, X_HEADS * nm, d), lambda b: (b, 0, 0))),
        compiler_params=pltpu.CompilerParams(
            dimension_semantics=("arbitrary",), vmem_limit_bytes=VMEM_LIMIT),
        name="memkv",
    )(mem, wk, wv, wq, wo)


def _attnmlp_kernel(h_ref, wqk_ref, vo_ref, ln2_g, ln2_b, w1, w2, ln3_g, ln3_b, o_ref):
    h1 = h_ref[0]
    s_all = _dot(h1, wqk_ref[0])
    probs = []
    for h in range(X_HEADS):
        s = s_all[:, h * N_MEM:(h + 1) * N_MEM]
        e = jnp.exp(s - jnp.max(s, axis=1, keepdims=True))
        probs.append((e * (1.0 / jnp.sum(e, axis=1, keepdims=True))).astype(_BF))
    xa = jnp.dot(jnp.concatenate(probs, axis=1), vo_ref[0], preferred_element_type=_F32)
    h2 = _layer_norm(ALPHA * h1 + xa, ln2_g[...], ln2_b[...])
    h2b = h2.astype(_BF)
    ff = jnp.zeros(h2.shape, _F32)
    for f in range(D_FF // FF_BLK):
        fs = slice(f * FF_BLK, (f + 1) * FF_BLK)
        hid = jnp.maximum(jnp.dot(h2b, w1[:, fs], preferred_element_type=_F32), 0.0)
        ff = ff + _dot(hid * hid, w2[fs, :])
    o_ref[0] = _layer_norm(ALPHA * h2 + ff, ln3_g[...], ln3_b[...])


def _attnmlp_call(h1, wqk, vo, ln2_g, ln2_b, w1, w2, ln3_g, ln3_b, tm):
    bsz, t, d = h1.shape
    tile = pl.BlockSpec((1, tm, d), lambda b, i: (b, i, 0))
    per_batch = lambda a: pl.BlockSpec((1,) + a.shape[1:], lambda b, i: (b, 0, 0))
    params = (ln2_g, ln2_b, w1, w2, ln3_g, ln3_b)
    return pl.pallas_call(
        _attnmlp_kernel,
        out_shape=jax.ShapeDtypeStruct((bsz, t, d), _F32),
        grid=(bsz, t // tm),
        in_specs=[tile, per_batch(wqk), per_batch(vo)] + [_full(p.shape) for p in params],
        out_specs=tile,
        compiler_params=pltpu.CompilerParams(
            dimension_semantics=("arbitrary", "arbitrary"), vmem_limit_bytes=VMEM_LIMIT),
        name="attnmlp",
    )(h1, wqk, vo, *params)


def _time_tile(t, target):
    tt = min(t, target)
    assert t % tt == 0 and tt % WIN == 0, (t, tt)
    return tt


def kernel(x, mem, ln_in_g, ln_in_b, w_in, conv_w, conv_b, m_i_bias, m_f_bias, m_norm_g, g_lr_w, g_lr_b, g_norm_g, w_out, ln1_g, ln1_b, x_wq, x_wk, x_wv, x_wo, ln2_g, ln2_b, w_ff1, w_ff2, ln3_g, ln3_b):
    assert w_in.shape[0] == DEPTH == 1
    row = lambda a: a.reshape(1, -1).astype(_F32)
    wi = w_in[0]
    w_m = wi[:, O_MQ:O_MI].astype(_BF)
    w_g = wi[:, O_GQ:O_GLR].astype(_BF)
    w_gate = jnp.concatenate([wi[:, O_MI:O_GQ], wi[:, O_GLR:O_GLR + G_RANK],
                              jnp.zeros((D_MODEL, LANE - 2 * M_HEADS - G_RANK), wi.dtype)], axis=1).astype(_BF)
    gate_bias = jnp.concatenate([m_i_bias[0], m_f_bias[0],
                                 jnp.zeros((LANE - 2 * M_HEADS,), _F32)]).reshape(1, LANE)
    glw = jnp.zeros((LANE, G_HEADS * G_DK), _F32).at[2 * M_HEADS:2 * M_HEADS + G_RANK].set(g_lr_w[0]).astype(_BF)
    tt = _time_tile(x.shape[1], 256)
    h1 = _mixer_call(x, row(ln_in_g), row(ln_in_b), w_m, w_g, w_gate, conv_w[0].astype(_F32), row(conv_b[0]),
                     gate_bias, row(m_norm_g[0]), glw, row(g_lr_b[0]), row(g_norm_g[0]), w_out[0].astype(_BF),
                     row(ln1_g[0]), row(ln1_b[0]), tt)
    wqk, vo = _memkv_call(mem, x_wk[0].astype(_BF), x_wv[0].astype(_BF), x_wq[0].astype(_BF), x_wo[0].astype(_BF))
    tm = _time_tile(x.shape[1], 1024)
    return _attnmlp_call(h1, wqk, vo, row(ln2_g[0]), row(ln2_b[0]),
                         w_ff1[0].astype(_BF), w_ff2[0].astype(_BF), row(ln3_g[0]), row(ln3_b[0]), tm)
```

```python
import functools

import numpy as np
import jax
import jax.numpy as jnp
from jax import lax
from jax.experimental import pallas as pl
from jax.experimental.pallas import tpu as pltpu

D_MODEL = 1024
CHUNK = 64
SUB = 16
WIN = 2 * CHUNK
N_MEM = 256
M_HEADS, M_DK, M_DV = 4, 128, 128
G_HEADS, G_DK, G_DV = 4, 64, 128
G_RANK = 16
G_TAU = 16.0
X_HEADS = 4
X_DH = D_MODEL // X_HEADS
D_FF = 4 * D_MODEL
FF_BLK = 512
IN_BLK = 512
ROW_SPLIT = 2
ROW_DELAY = 6
DEPTH = 1
ALPHA = (2.0 * DEPTH) ** 0.25
LN_EPS = 1e-5
LOG2E = 1.4426950408889634
LANE = 128

C_MQ, C_MK, C_MV, C_MO = 0, 512, 1024, 1536
C_GQ, C_GK, C_GV, C_GG = 2048, 2304, 2560, 3072
C_GATE = 3584
IN_COLS_R = C_GATE + LANE
O_MQ, O_MK, O_MV, O_MO, O_MI, O_MF = 0, 512, 1024, 1536, 2048, 2052
O_GQ, O_GK, O_GV, O_GG, O_GLR = 2056, 2312, 2568, 3080, 3592

VMEM_LIMIT = 56 * 1024 * 1024

_BF = jnp.bfloat16
_F32 = jnp.float32


def _dot(a, b):
    return jnp.dot(a.astype(_BF), b.astype(_BF), preferred_element_type=_F32)


def _dot_nt(a, b):
    return lax.dot_general(a.astype(_BF), b.astype(_BF), (((1,), (1,)), ((), ())),
                           preferred_element_type=_F32)


def _dot_tn(a, b):
    return lax.dot_general(a.astype(_BF), b.astype(_BF), (((0,), (0,)), ((), ())),
                           preferred_element_type=_F32)


def _split2(x):
    hi = x.astype(_BF)
    lo = (x - hi.astype(_F32)).astype(_BF)
    return jnp.concatenate([hi, lo], axis=1)


def _merge2(y, n):
    return y[:, 0:n] + y[:, n:2 * n]


def _layer_norm(x, g, b):
    mu = jnp.mean(x, axis=-1, keepdims=True)
    xc = x - mu
    var = jnp.mean(xc * xc, axis=-1, keepdims=True)
    return xc * lax.rsqrt(var + LN_EPS) * g + b


def _log_sigmoid(x):
    return -(jnp.maximum(-x, 0.0) + jnp.log(1.0 + jnp.exp(-jnp.abs(x))))


def _sigmoid(x):
    return 1.0 / (1.0 + jnp.exp(-x))


def _round_robin(chains):
    while chains:
        alive = []
        for ch in chains:
            try:
                next(ch)
                alive.append(ch)
            except StopIteration:
                pass
        chains = alive


def _mixer_kernel(x_ref, lnin_g, lnin_b, w_m, w_g, w_gate, conv_w, conv_b, gate_bias, m_norm_g, glw, glb, g_norm_g,
                  w_out, ln1_g, ln1_b, tri64, tri16, ones16, gsel, rsel,
                  o_ref, uext, c_st, n_st, m_st, s_st, proj_buf, h0_buf, *, tt, nt):
    g = pl.program_id(0)

    @pl.when(g == 0)
    def _():
        proj_buf[1] = jnp.zeros(proj_buf.shape[1:], _F32)
        h0_buf[1] = jnp.zeros(h0_buf.shape[1:], _F32)

    @pl.when((g == 0) | ((g - 1) % nt == 0))
    def _():
        uext[0:8, :] = jnp.zeros((8, 2 * 512), _F32)
        c_st[...] = jnp.zeros(c_st.shape, _F32)
        n_st[...] = jnp.zeros(n_st.shape, _F32)
        m_st[...] = jnp.zeros(m_st.shape, _F32)
        s_st[...] = jnp.zeros(s_st.shape, _F32)

    def step(slot_a):
        slot_b = 1 - slot_a

        def in_proj():
            h0n = _layer_norm(x_ref[0], lnin_g[...], lnin_b[...])
            h0_buf[slot_a] = h0n
            h0b = h0n.astype(_BF)
            yield
            for w_ref, base in ((w_m, C_MQ), (w_g, C_GQ), (w_gate, C_GATE)):
                ncol = w_ref.shape[1]
                for c0 in range(0, ncol, IN_BLK):
                    c1 = min(c0 + IN_BLK, ncol)
                    proj_buf[slot_a, :, base + c0:base + c1] = jnp.dot(h0b, w_ref[:, c0:c1],
                                                                       preferred_element_type=_F32)
                    yield

        pb = proj_buf.at[slot_b]
        h0 = h0_buf[slot_b]

        uext[8:8 + tt, :] = pb[:, C_MQ:C_MV]
        cw = conv_w[...]
        conv = (uext[5:5 + tt, :] * cw[0:1, :] + uext[6:6 + tt, :] * cw[1:2, :]
                + uext[7:7 + tt, :] * cw[2:3, :] + uext[8:8 + tt, :] * cw[3:4, :]) + conv_b[...]
        uext[0:8, :] = uext[tt:tt + 8, :]
        qk = conv * _sigmoid(conv)
        mq = qk[:, 0:512]
        mk = qk[:, 512:1024] * (M_DK ** -0.5)
        mv = pb[:, C_MV:C_MO]
        mo = pb[:, C_MO:C_GQ]

        gb = pb[:, C_GATE:C_GATE + LANE] + gate_bias[...]
        lane = lax.broadcasted_iota(jnp.int32, (tt, LANE), 1)
        lf = _log_sigmoid(gb)
        bcum = _merge2(_dot(tri64[...], _split2(lf)), LANE)
        vcol = jnp.where((lane >= M_HEADS) & (lane < 2 * M_HEADS), bcum, gb) * LOG2E
        vrow = vcol.T

        gw = G_HEADS * G_DK
        la = _log_sigmoid(_dot(gb, glw[...]) + glb[...]) * (1.0 / G_TAU)
        la2 = _split2(la)
        bc = _merge2(_dot(tri16[...], la2), gw)
        bend = _merge2(_dot(ones16[...], la2), gw)
        gq = pb[:, C_GQ:C_GK] * (G_DK ** -0.5)
        gk = pb[:, C_GK:C_GV]
        gv = pb[:, C_GV:C_GG]
        gg = pb[:, C_GG:C_GATE]
        bc2 = bc * LOG2E
        bend2 = bend * LOG2E
        qh = gq * jnp.exp2(bc2)
        kh = gk * jnp.exp2(bend2 - bc2)
        eg = jnp.exp2(bend2)
        nsub = tt // SUB

        rr = lax.broadcasted_iota(jnp.int32, (WIN, WIN), 0)
        cc = lax.broadcasted_iota(jnp.int32, (WIN, WIN), 1)
        valid = (cc <= rr) & ((cc >= CHUNK) == (rr >= CHUNK))
        first = lax.broadcasted_iota(jnp.int32, (WIN, 1), 0) < CHUNK
        mg = m_norm_g[...]
        m_outs = [None] * M_HEADS

        def mlstm_head(h):
            hs = slice(h * M_DK, (h + 1) * M_DK)
            c_cur = c_st[h]
            n_cur = n_st[h]
            m_cur = m_st[h]
            win_outs = []
            for w in range(tt // WIN):
                rs = slice(w * WIN, (w + 1) * WIN)
                q, k, v = mq[rs, hs], mk[rs, hs], mv[rs, hs]
                li_col, b_col = vcol[rs, h:h + 1], vcol[rs, M_HEADS + h:M_HEADS + h + 1]
                li_row, b_row = vrow[h:h + 1, rs], vrow[M_HEADS + h:M_HEADS + h + 1, rs]
                dm = jnp.where(valid, b_col - b_row + li_row, -jnp.inf)
                md = jnp.max(dm, axis=1, keepdims=True)
                g0, g1 = b_col[CHUNK - 1:CHUNK, :], b_col[WIN - 1:WIN, :]
                a_col = jnp.where(first, g0, g1) - b_col + li_col
                yield
                am0 = jnp.max(a_col[0:CHUNK], axis=0, keepdims=True)
                am1 = jnp.max(a_col[CHUNK:WIN], axis=0, keepdims=True)
                wk = jnp.exp2(a_col - jnp.where(first, am0, am1)) * k
                s_raw = _dot_nt(q, k)
                yield
                cprev, nprev, mprev = [], [], []
                for c, (gc, am) in enumerate(((g0, am0), (g1, am1))):
                    cs = slice(c * CHUNK, (c + 1) * CHUNK)
                    cprev.append(c_cur)
                    nprev.append(n_cur)
                    mprev.append(m_cur)
                    u_mat = _dot_tn(wk[cs], v[cs])
                    u_vec = jnp.sum(wk[cs], axis=0, keepdims=True)
                    m_new = jnp.maximum(gc + m_cur, am)
                    dec = jnp.exp2(gc + m_cur - m_new)
                    inj = jnp.exp2(am - m_new)
                    c_cur = dec * c_cur + inj * u_mat
                    n_cur = dec * n_cur + inj * u_vec
                    m_cur = m_new
                yield
                inter = b_col + jnp.where(first, mprev[0], mprev[1])
                m_t = jnp.maximum(inter, md)
                sc = jnp.exp2(inter - m_t)
                s_mat = s_raw * jnp.exp2(dm - m_t)
                yield
                sv = _dot(s_mat, v)
                qc = jnp.concatenate([_dot(q[0:CHUNK], cprev[0]), _dot(q[CHUNK:WIN], cprev[1])], axis=0)
                qn = jnp.sum(q * jnp.where(first, nprev[0], nprev[1]), axis=1, keepdims=True)
                ssum = jnp.sum(s_mat, axis=1, keepdims=True)
                yield
                num = sc * qc + sv
                den = sc * qn + ssum
                hh = num * (1.0 / jnp.maximum(jnp.abs(den), jnp.exp2(-m_t)))
                mu = jnp.mean(hh, axis=1, keepdims=True)
                yield
                hc = hh - mu
                var = jnp.mean(hc * hc, axis=1, keepdims=True)
                yield
                win_outs.append(hc * lax.rsqrt(var + LN_EPS))
            c_st[h] = c_cur
            n_st[h] = n_cur
            m_st[h] = m_cur
            hn = win_outs[0] if len(win_outs) == 1 else jnp.concatenate(win_outs, axis=0)
            m_outs[h] = _sigmoid(mo[:, hs]) * (hn * mg[:, hs])

        o_diag = [None] * G_HEADS

        def gla_diag():
            gk3 = gk.reshape(nsub, SUB, gw)
            bc3 = bc2.reshape(nsub, SUB, gw)
            half = SUB // 2
            hi = lambda a: a.reshape(nsub, 2, half, a.shape[-1])[:, 1].reshape(nsub * half, a.shape[-1])
            gq_hi, bc2_hi = hi(gq), hi(bc2)
            p_acc = jnp.zeros((tt, LANE), _F32)
            p_hi = jnp.zeros((tt // 2, LANE), _F32)
            for s in range(SUB):
                if s < half:
                    kb = jnp.broadcast_to(gk3[:, s:s + 1, :], (nsub, SUB, gw)).reshape(tt, gw)
                    bb = jnp.broadcast_to(bc3[:, s:s + 1, :], (nsub, SUB, gw)).reshape(tt, gw)
                    y = gq * kb * jnp.exp2(jnp.minimum(bc2 - bb, 0.0))
                    p_acc = p_acc + _dot(y, gsel[s])
                else:
                    kb = jnp.broadcast_to(gk3[:, s:s + 1, :], (nsub, half, gw)).reshape(tt // 2, gw)
                    bb = jnp.broadcast_to(bc3[:, s:s + 1, :], (nsub, half, gw)).reshape(tt // 2, gw)
                    y = gq_hi * kb * jnp.exp2(jnp.minimum(bc2_hi - bb, 0.0))
                    p_hi = p_hi + _dot(y, gsel[s])
                yield
            p4 = p_acc.reshape(nsub, 2, half, LANE)
            p_acc = jnp.concatenate([p4[:, 0:1], p4[:, 1:2] + p_hi.reshape(nsub, 1, half, LANE)],
                                    axis=1).reshape(tt, LANE)
            p_bf = p_acc.astype(_BF)
            r2 = lax.broadcasted_iota(jnp.int32, (WIN, WIN), 0)
            c2 = lax.broadcasted_iota(jnp.int32, (WIN, WIN), 1)
            sub_mask = (c2 <= r2) & ((c2 // SUB) == (r2 // SUB))
            for h in range(G_HEADS):
                outs = []
                for w in range(tt // WIN):
                    rs = slice(w * WIN, (w + 1) * WIN)
                    a_full = jnp.dot(p_bf[rs], rsel[h], preferred_element_type=_F32)
                    a_h = jnp.where(sub_mask, a_full, 0.0)
                    outs.append(_dot(a_h, gv[rs, h * G_DV:(h + 1) * G_DV]))
                o_diag[h] = outs[0] if len(outs) == 1 else jnp.concatenate(outs, axis=0)
                yield

        head_of_lane = lax.broadcasted_iota(jnp.int32, (SUB, gw), 1) // G_DK
        o_int = [[None] * nsub for _ in range(G_HEADS)]

        def gla_rec():
            st = s_st[...]
            for j in range(nsub):
                js = slice(j * SUB, (j + 1) * SUB)
                qj, kj = qh[js], kh[js]
                qq = jnp.concatenate([jnp.where(head_of_lane == h, qj, 0.0) for h in range(G_HEADS)], axis=0)
                kk = jnp.concatenate([jnp.where(head_of_lane == h, kj, 0.0) for h in range(G_HEADS)], axis=0)
                vv = jnp.concatenate([gv[js, h * G_DV:(h + 1) * G_DV] for h in range(G_HEADS)], axis=0)
                oj = _dot_nt(qq, st)
                for h in range(G_HEADS):
                    o_int[h][j] = oj[h * SUB:(h + 1) * SUB]
                st = eg[j * SUB:j * SUB + 1, :] * st + _dot_tn(vv, kk)
                yield
            s_st[...] = st

        _round_robin([in_proj(), gla_diag(), gla_rec()] + [mlstm_head(h) for h in range(M_HEADS)])

        gn = g_norm_g[...]
        g_outs = []
        for h in range(G_HEADS):
            hs = slice(h * G_DV, (h + 1) * G_DV)
            og = o_diag[h] + jnp.concatenate(o_int[h], axis=0)
            rms = lax.rsqrt(jnp.mean(og * og, axis=1, keepdims=True) + LN_EPS)
            gate = gg[:, hs]
            g_outs.append(gate * _sigmoid(gate) * (og * rms * gn[:, hs]))

        y = jnp.concatenate(m_outs + g_outs, axis=1)
        mix = _dot(y, w_out[...])
        o_ref[0] = _layer_norm(ALPHA * h0 + mix, ln1_g[...], ln1_b[...])

    for par in range(2):
        pl.when(g % 2 == par)(functools.partial(step, par))


def _mixer_constants(tt):
    r = np.arange(tt)
    tri64 = ((r[None, :] <= r[:, None]) & (r[None, :] // CHUNK == r[:, None] // CHUNK))
    same16 = (r[None, :] // SUB == r[:, None] // SUB)
    tri16 = (r[None, :] <= r[:, None]) & same16
    gw = G_HEADS * G_DK
    gsel = np.zeros((SUB, gw, LANE), np.float32)
    for s in range(SUB):
        for h in range(G_HEADS):
            gsel[s, h * G_DK:(h + 1) * G_DK, h * SUB + s] = 1.0
    rsel = np.zeros((G_HEADS, LANE, WIN), np.float32)
    for h in range(G_HEADS):
        for s in range(SUB):
            rsel[h, h * SUB + s, s::SUB] = 1.0
    to_bf = lambda a: jnp.asarray(a.astype(np.float32), dtype=_BF)
    return to_bf(tri64), to_bf(tri16), to_bf(same16), to_bf(gsel), to_bf(rsel)


def _full(shape):
    nd = len(shape)
    return pl.BlockSpec(shape, lambda b, i, _nd=nd: (0,) * _nd)


def _mixer_call(x, lnin_g, lnin_b, w_m, w_g, w_gate, conv_w, conv_b, gate_bias, m_norm_g, glw, glb, g_norm_g,
                w_out, ln1_g, ln1_b, tt):
    bsz, t, d = x.shape
    consts = _mixer_constants(tt)
    params = (lnin_g, lnin_b, w_m, w_g, w_gate, conv_w, conv_b, gate_bias, m_norm_g, glw, glb, g_norm_g,
              w_out, ln1_g, ln1_b) + consts
    nt = t // tt
    ntiles = bsz * nt
    in_tile = pl.BlockSpec((1, tt, d), lambda g: (jnp.minimum(g, ntiles - 1) // nt, jnp.minimum(g, ntiles - 1) % nt, 0))
    out_tile = pl.BlockSpec((1, tt, d), lambda g: (jnp.maximum(g - 1, 0) // nt, jnp.maximum(g - 1, 0) % nt, 0))
    full = lambda shape: pl.BlockSpec(shape, lambda g, _nd=len(shape): (0,) * _nd)
    return pl.pallas_call(
        functools.partial(_mixer_kernel, tt=tt, nt=nt),
        out_shape=jax.ShapeDtypeStruct((bsz, t, d), _F32),
        grid=(ntiles + 1,),
        in_specs=[in_tile] + [full(p.shape) for p in params],
        out_specs=out_tile,
        scratch_shapes=[
            pltpu.VMEM((tt + 8, 2 * 512), _F32),
            pltpu.VMEM((M_HEADS, M_DK, M_DV), _F32),
            pltpu.VMEM((M_HEADS, 1, M_DK), _F32),
            pltpu.VMEM((M_HEADS, 1, 1), _F32),
            pltpu.VMEM((G_DV, G_HEADS * G_DK), _F32),
            pltpu.VMEM((2, tt, IN_COLS_R), _F32),
            pltpu.VMEM((2, tt, D_MODEL), _F32),
        ],
        compiler_params=pltpu.CompilerParams(
            dimension_semantics=("arbitrary",), vmem_limit_bytes=VMEM_LIMIT),
        name="mixer",
    )(x, *params)


def _memkv_kernel(mem_ref, wk_ref, wv_ref, wq_ref, wo_ref, wqk_ref, vo_ref):
    m = mem_ref[0]
    k = _dot(m, wk_ref[...])
    v = _dot(m, wv_ref[...])
    for h in range(X_HEADS):
        hs = slice(h * X_DH, (h + 1) * X_DH)
        ms = slice(h * N_MEM, (h + 1) * N_MEM)
        wqk_ref[0, :, ms] = (_dot_nt(wq_ref[:, hs], k[:, hs]) * (X_DH ** -0.5)).astype(_BF)
        vo_ref[0, ms, :] = _dot(v[:, hs], wo_ref[hs, :]).astype(_BF)


def _memkv_call(mem, wk, wv, wq, wo):
    bsz, nm, d = mem.shape
    blk = pl.BlockSpec((1, nm, d), lambda b: (b, 0, 0))
    wspec = pl.BlockSpec((d, d), lambda b: (0, 0))
    return pl.pallas_call(
        _memkv_kernel,
        out_shape=(jax.ShapeDtypeStruct((bsz, d, X_HEADS * nm), _BF),
                   jax.ShapeDtypeStruct((bsz, X_HEADS * nm, d), _BF)),
        grid=(bsz,),
        in_specs=[blk, wspec, wspec, wspec, wspec],
        out_specs=(pl.BlockSpec((1, d, X_HEADS * nm), lambda b: (b, 0, 0)),
                   pl.BlockSpec((1, X_HEADS * nm, d), lambda b: (b, 0, 0))),
        compiler_params=pltpu.CompilerParams(
            dimension_semantics=("arbitrary",), vmem_limit_bytes=VMEM_LIMIT),
        name="memkv",
    )(mem, wk, wv, wq, wo)


def _attnmlp_kernel(h_ref, wqk_ref, vo_ref, ln2_g, ln2_b, w1, w2, ln3_g, ln3_b, o_ref):
    rows = h_ref.shape[1] // ROW_SPLIT

    def chain(r, wait):
        for _ in range(wait):
            yield
        rs = slice(r * rows, (r + 1) * rows)
        h1 = h_ref[0, rs, :]
        s_all = _dot(h1, wqk_ref[0])
        yield
        probs = []
        for h in range(X_HEADS):
            s = s_all[:, h * N_MEM:(h + 1) * N_MEM]
            e = jnp.exp(s - jnp.max(s, axis=1, keepdims=True))
            probs.append((e * (1.0 / jnp.sum(e, axis=1, keepdims=True))).astype(_BF))
            yield
        xa = jnp.dot(jnp.concatenate(probs, axis=1), vo_ref[0], preferred_element_type=_F32)
        yield
        h2 = _layer_norm(ALPHA * h1 + xa, ln2_g[...], ln2_b[...])
        h2b = h2.astype(_BF)
        yield
        ff = jnp.zeros(h2.shape, _F32)
        for f in range(D_FF // FF_BLK):
            fs = slice(f * FF_BLK, (f + 1) * FF_BLK)
            hid = jnp.maximum(jnp.dot(h2b, w1[:, fs], preferred_element_type=_F32), 0.0)
            ff = ff + _dot(hid * hid, w2[fs, :])
            yield
        o_ref[0, rs, :] = _layer_norm(ALPHA * h2 + ff, ln3_g[...], ln3_b[...])

    _round_robin([chain(r, r * ROW_DELAY) for r in range(ROW_SPLIT)])


def _attnmlp_call(h1, wqk, vo, ln2_g, ln2_b, w1, w2, ln3_g, ln3_b, tm):
    bsz, t, d = h1.shape
    tile = pl.BlockSpec((1, tm, d), lambda b, i: (b, i, 0))
    per_batch = lambda a: pl.BlockSpec((1,) + a.shape[1:], lambda b, i: (b, 0, 0))
    params = (ln2_g, ln2_b, w1, w2, ln3_g, ln3_b)
    return pl.pallas_call(
        _attnmlp_kernel,
        out_shape=jax.ShapeDtypeStruct((bsz, t, d), _F32),
        grid=(bsz, t // tm),
        in_specs=[tile, per_batch(wqk), per_batch(vo)] + [_full(p.shape) for p in params],
        out_specs=tile,
        compiler_params=pltpu.CompilerParams(
            dimension_semantics=("arbitrary", "arbitrary"), vmem_limit_bytes=VMEM_LIMIT),
        name="attnmlp",
    )(h1, wqk, vo, *params)


def _time_tile(t, target):
    tt = min(t, target)
    assert t % tt == 0 and tt % WIN == 0, (t, tt)
    return tt


def kernel(x, mem, ln_in_g, ln_in_b, w_in, conv_w, conv_b, m_i_bias, m_f_bias, m_norm_g, g_lr_w, g_lr_b, g_norm_g, w_out, ln1_g, ln1_b, x_wq, x_wk, x_wv, x_wo, ln2_g, ln2_b, w_ff1, w_ff2, ln3_g, ln3_b):
    assert w_in.shape[0] == DEPTH == 1
    row = lambda a: a.reshape(1, -1).astype(_F32)
    wi = w_in[0]
    w_m = wi[:, O_MQ:O_MI].astype(_BF)
    w_g = wi[:, O_GQ:O_GLR].astype(_BF)
    w_gate = jnp.concatenate([wi[:, O_MI:O_GQ], wi[:, O_GLR:O_GLR + G_RANK],
                              jnp.zeros((D_MODEL, LANE - 2 * M_HEADS - G_RANK), wi.dtype)], axis=1).astype(_BF)
    gate_bias = jnp.concatenate([m_i_bias[0], m_f_bias[0],
                                 jnp.zeros((LANE - 2 * M_HEADS,), _F32)]).reshape(1, LANE)
    glw = jnp.zeros((LANE, G_HEADS * G_DK), _F32).at[2 * M_HEADS:2 * M_HEADS + G_RANK].set(g_lr_w[0]).astype(_BF)
    tt = _time_tile(x.shape[1], 256)
    h1 = _mixer_call(x, row(ln_in_g), row(ln_in_b), w_m, w_g, w_gate, conv_w[0].astype(_F32), row(conv_b[0]),
                     gate_bias, row(m_norm_g[0]), glw, row(g_lr_b[0]), row(g_norm_g[0]), w_out[0].astype(_BF),
                     row(ln1_g[0]), row(ln1_b[0]), tt)
    wqk, vo = _memkv_call(mem, x_wk[0].astype(_BF), x_wv[0].astype(_BF), x_wq[0].astype(_BF), x_wo[0].astype(_BF))
    tm = _time_tile(x.shape[1], 1024)
    return _attnmlp_call(h1, wqk, vo, row(ln2_g[0]), row(ln2_b[0]),
                         w_ff1[0].astype(_BF), w_ff2[0].astype(_BF), row(ln3_g[0]), row(ln3_b[0]), tm)
```

```python
import functools

import numpy as np
import jax
import jax.numpy as jnp
from jax import lax
from jax.experimental import pallas as pl
from jax.experimental.pallas import tpu as pltpu

D_MODEL = 1024
CHUNK = 64
SUB = 16
WIN = 2 * CHUNK
N_MEM = 256
M_HEADS, M_DK, M_DV = 4, 128, 128
M_QK = M_HEADS * M_DK
G_HEADS, G_DK, G_DV = 4, 64, 128
G_RANK = 16
G_TAU = 16.0
X_HEADS = 4
X_DH = D_MODEL // X_HEADS
D_FF = 4 * D_MODEL
MIXER_ROWS = 256
ATTN_ROWS = 1024
FF_BLK = 512
IN_BLK = 512
ROW_SPLIT = 2
ROW_DELAY = 6
DEPTH = 1
ALPHA = (2.0 * DEPTH) ** 0.25
LN_EPS = 1e-5
LOG2E = 1.4426950408889634
LANE = 128

C_MQ, C_MK, C_MV, C_MO = 0, 512, 1024, 1536
C_GQ, C_GK, C_GV, C_GG = 2048, 2304, 2560, 3072
C_GATE = 3584
IN_COLS_R = C_GATE + LANE
O_MQ, O_MK, O_MV, O_MO, O_MI, O_MF = 0, 512, 1024, 1536, 2048, 2052
O_GQ, O_GK, O_GV, O_GG, O_GLR = 2056, 2312, 2568, 3080, 3592

VMEM_LIMIT = 56 * 1024 * 1024

_BF = jnp.bfloat16
_F32 = jnp.float32


def _dot(a, b):
    return jnp.dot(a.astype(_BF), b.astype(_BF), preferred_element_type=_F32)


def _dot_nt(a, b):
    return lax.dot_general(a.astype(_BF), b.astype(_BF), (((1,), (1,)), ((), ())),
                           preferred_element_type=_F32)


def _dot_tn(a, b):
    return lax.dot_general(a.astype(_BF), b.astype(_BF), (((0,), (0,)), ((), ())),
                           preferred_element_type=_F32)


def _split2(x):
    hi = x.astype(_BF)
    lo = (x - hi.astype(_F32)).astype(_BF)
    return jnp.concatenate([hi, lo], axis=1)


def _merge2(y, n):
    return y[:, 0:n] + y[:, n:2 * n]


def _layer_norm(x, g, b):
    mu = jnp.mean(x, axis=-1, keepdims=True)
    xc = x - mu
    var = jnp.mean(xc * xc, axis=-1, keepdims=True)
    return xc * lax.rsqrt(var + LN_EPS) * g + b


def _log_sigmoid(x):
    return -(jnp.maximum(-x, 0.0) + jnp.log(1.0 + jnp.exp(-jnp.abs(x))))


def _sigmoid(x):
    return 1.0 / (1.0 + jnp.exp(-x))


def _round_robin(chains):
    while chains:
        alive = []
        for ch in chains:
            try:
                next(ch)
                alive.append(ch)
            except StopIteration:
                pass
        chains = alive


def _mixer_kernel(x_ref, lnin_g, lnin_b, w_m, w_g, w_gate, conv_w, conv_b, gate_bias, m_norm_g, glw, glb, g_norm_g,
                  w_out, ln1_g, ln1_b, tri64, tri16, ones16, gsel, rsel,
                  o_ref, uext, c_st, n_st, m_st, s_st, proj_buf, h0_buf, *, tt, nt):
    g = pl.program_id(0)

    @pl.when(g == 0)
    def _():
        proj_buf[1] = jnp.zeros(proj_buf.shape[1:], _F32)
        h0_buf[1] = jnp.zeros(h0_buf.shape[1:], _F32)

    @pl.when((g == 0) | ((g - 1) % nt == 0))
    def _():
        uext[0:8, :] = jnp.zeros((8, 2 * M_QK), _F32)
        c_st[...] = jnp.zeros(c_st.shape, _F32)
        n_st[...] = jnp.zeros(n_st.shape, _F32)
        m_st[...] = jnp.zeros(m_st.shape, _F32)
        s_st[...] = jnp.zeros(s_st.shape, _F32)

    def step(slot_a):
        slot_b = 1 - slot_a

        def in_proj():
            h0n = _layer_norm(x_ref[0], lnin_g[...], lnin_b[...])
            h0_buf[slot_a] = h0n
            h0b = h0n.astype(_BF)
            yield
            for w_ref, base in ((w_m, C_MQ), (w_g, C_GQ), (w_gate, C_GATE)):
                ncol = w_ref.shape[1]
                for c0 in range(0, ncol, IN_BLK):
                    c1 = min(c0 + IN_BLK, ncol)
                    proj_buf[slot_a, :, base + c0:base + c1] = jnp.dot(h0b, w_ref[:, c0:c1],
                                                                       preferred_element_type=_F32)
                    yield

        pb = proj_buf.at[slot_b]
        h0 = h0_buf[slot_b]

        uext[8:8 + tt, :] = pb[:, C_MQ:C_MV]
        cw = conv_w[...]
        conv = (uext[5:5 + tt, :] * cw[0:1, :] + uext[6:6 + tt, :] * cw[1:2, :]
                + uext[7:7 + tt, :] * cw[2:3, :] + uext[8:8 + tt, :] * cw[3:4, :]) + conv_b[...]
        uext[0:8, :] = uext[tt:tt + 8, :]
        qk = conv * _sigmoid(conv)
        mq = qk[:, 0:M_QK]
        mk = qk[:, M_QK:2 * M_QK] * (M_DK ** -0.5)
        mv = pb[:, C_MV:C_MO]
        mo = pb[:, C_MO:C_GQ]

        gb = pb[:, C_GATE:C_GATE + LANE] + gate_bias[...]
        lane = lax.broadcasted_iota(jnp.int32, (tt, LANE), 1)
        lf = _log_sigmoid(gb)
        bcum = _merge2(_dot(tri64[...], _split2(lf)), LANE)
        vcol = jnp.where((lane >= M_HEADS) & (lane < 2 * M_HEADS), bcum, gb) * LOG2E
        vrow = vcol.T

        gw = G_HEADS * G_DK
        la = _log_sigmoid(_dot(gb, glw[...]) + glb[...]) * (1.0 / G_TAU)
        la2 = _split2(la)
        bc = _merge2(_dot(tri16[...], la2), gw)
        bend = _merge2(_dot(ones16[...], la2), gw)
        gq = pb[:, C_GQ:C_GK] * (G_DK ** -0.5)
        gk = pb[:, C_GK:C_GV]
        gv = pb[:, C_GV:C_GG]
        gg = pb[:, C_GG:C_GATE]
        bc2 = bc * LOG2E
        bend2 = bend * LOG2E
        qh = gq * jnp.exp2(bc2)
        kh = gk * jnp.exp2(bend2 - bc2)
        eg = jnp.exp2(bend2)
        nsub = tt // SUB

        rr = lax.broadcasted_iota(jnp.int32, (WIN, WIN), 0)
        cc = lax.broadcasted_iota(jnp.int32, (WIN, WIN), 1)
        valid = (cc <= rr) & ((cc >= CHUNK) == (rr >= CHUNK))
        first = lax.broadcasted_iota(jnp.int32, (WIN, 1), 0) < CHUNK
        mg = m_norm_g[...]
        m_outs = [None] * M_HEADS

        def mlstm_head(h):
            hs = slice(h * M_DK, (h + 1) * M_DK)
            c_cur = c_st[h]
            n_cur = n_st[h]
            m_cur = m_st[h]
            win_outs = []
            for w in range(tt // WIN):
                rs = slice(w * WIN, (w + 1) * WIN)
                q, k, v = mq[rs, hs], mk[rs, hs], mv[rs, hs]
                li_col, b_col = vcol[rs, h:h + 1], vcol[rs, M_HEADS + h:M_HEADS + h + 1]
                li_row, b_row = vrow[h:h + 1, rs], vrow[M_HEADS + h:M_HEADS + h + 1, rs]
                dm = jnp.where(valid, b_col - b_row + li_row, -jnp.inf)
                md = jnp.max(dm, axis=1, keepdims=True)
                g0, g1 = b_col[CHUNK - 1:CHUNK, :], b_col[WIN - 1:WIN, :]
                a_col = jnp.where(first, g0, g1) - b_col + li_col
                yield
                am0 = jnp.max(a_col[0:CHUNK], axis=0, keepdims=True)
                am1 = jnp.max(a_col[CHUNK:WIN], axis=0, keepdims=True)
                wk = jnp.exp2(a_col - jnp.where(first, am0, am1)) * k
                s_raw = _dot_nt(q, k)
                yield
                cprev, nprev, mprev = [], [], []
                for c, (gc, am) in enumerate(((g0, am0), (g1, am1))):
                    cs = slice(c * CHUNK, (c + 1) * CHUNK)
                    cprev.append(c_cur)
                    nprev.append(n_cur)
                    mprev.append(m_cur)
                    u_mat = _dot_tn(wk[cs], v[cs])
                    u_vec = jnp.sum(wk[cs], axis=0, keepdims=True)
                    m_new = jnp.maximum(gc + m_cur, am)
                    dec = jnp.exp2(gc + m_cur - m_new)
                    inj = jnp.exp2(am - m_new)
                    c_cur = dec * c_cur + inj * u_mat
                    n_cur = dec * n_cur + inj * u_vec
                    m_cur = m_new
                yield
                inter = b_col + jnp.where(first, mprev[0], mprev[1])
                m_t = jnp.maximum(inter, md)
                sc = jnp.exp2(inter - m_t)
                s_mat = s_raw * jnp.exp2(dm - m_t)
                yield
                sv = _dot(s_mat, v)
                qc = jnp.concatenate([_dot(q[0:CHUNK], cprev[0]), _dot(q[CHUNK:WIN], cprev[1])], axis=0)
                qn = jnp.sum(q * jnp.where(first, nprev[0], nprev[1]), axis=1, keepdims=True)
                ssum = jnp.sum(s_mat, axis=1, keepdims=True)
                yield
                num = sc * qc + sv
                den = sc * qn + ssum
                hh = num * (1.0 / jnp.maximum(jnp.abs(den), jnp.exp2(-m_t)))
                mu = jnp.mean(hh, axis=1, keepdims=True)
                yield
                hc = hh - mu
                var = jnp.mean(hc * hc, axis=1, keepdims=True)
                yield
                win_outs.append(hc * lax.rsqrt(var + LN_EPS))
            c_st[h] = c_cur
            n_st[h] = n_cur
            m_st[h] = m_cur
            hn = win_outs[0] if len(win_outs) == 1 else jnp.concatenate(win_outs, axis=0)
            m_outs[h] = _sigmoid(mo[:, hs]) * (hn * mg[:, hs])

        o_diag = [None] * G_HEADS

        def gla_diag():
            gk3 = gk.reshape(nsub, SUB, gw)
            bc3 = bc2.reshape(nsub, SUB, gw)
            half = SUB // 2
            hi = lambda a: a.reshape(nsub, 2, half, a.shape[-1])[:, 1].reshape(nsub * half, a.shape[-1])
            gq_hi, bc2_hi = hi(gq), hi(bc2)
            p_acc = jnp.zeros((tt, LANE), _F32)
            p_hi = jnp.zeros((tt // 2, LANE), _F32)
            for s in range(SUB):
                if s < half:
                    kb = jnp.broadcast_to(gk3[:, s:s + 1, :], (nsub, SUB, gw)).reshape(tt, gw)
                    bb = jnp.broadcast_to(bc3[:, s:s + 1, :], (nsub, SUB, gw)).reshape(tt, gw)
                    y = gq * kb * jnp.exp2(jnp.minimum(bc2 - bb, 0.0))
                    p_acc = p_acc + _dot(y, gsel[s])
                else:
                    kb = jnp.broadcast_to(gk3[:, s:s + 1, :], (nsub, half, gw)).reshape(tt // 2, gw)
                    bb = jnp.broadcast_to(bc3[:, s:s + 1, :], (nsub, half, gw)).reshape(tt // 2, gw)
                    y = gq_hi * kb * jnp.exp2(jnp.minimum(bc2_hi - bb, 0.0))
                    p_hi = p_hi + _dot(y, gsel[s])
                yield
            p4 = p_acc.reshape(nsub, 2, half, LANE)
            p_acc = jnp.concatenate([p4[:, 0:1], p4[:, 1:2] + p_hi.reshape(nsub, 1, half, LANE)],
                                    axis=1).reshape(tt, LANE)
            p_bf = p_acc.astype(_BF)
            r2 = lax.broadcasted_iota(jnp.int32, (WIN, WIN), 0)
            c2 = lax.broadcasted_iota(jnp.int32, (WIN, WIN), 1)
            sub_mask = (c2 <= r2) & ((c2 // SUB) == (r2 // SUB))
            for h in range(G_HEADS):
                outs = []
                for w in range(tt // WIN):
                    rs = slice(w * WIN, (w + 1) * WIN)
                    a_full = jnp.dot(p_bf[rs], rsel[h], preferred_element_type=_F32)
                    a_h = jnp.where(sub_mask, a_full, 0.0)
                    outs.append(_dot(a_h, gv[rs, h * G_DV:(h + 1) * G_DV]))
                o_diag[h] = outs[0] if len(outs) == 1 else jnp.concatenate(outs, axis=0)
                yield

        head_of_lane = lax.broadcasted_iota(jnp.int32, (SUB, gw), 1) // G_DK
        o_int = [[None] * nsub for _ in range(G_HEADS)]

        def gla_rec():
            st = s_st[...]
            for j in range(nsub):
                js = slice(j * SUB, (j + 1) * SUB)
                qj, kj = qh[js], kh[js]
                qq = jnp.concatenate([jnp.where(head_of_lane == h, qj, 0.0) for h in range(G_HEADS)], axis=0)
                kk = jnp.concatenate([jnp.where(head_of_lane == h, kj, 0.0) for h in range(G_HEADS)], axis=0)
                vv = jnp.concatenate([gv[js, h * G_DV:(h + 1) * G_DV] for h in range(G_HEADS)], axis=0)
                oj = _dot_nt(qq, st)
                for h in range(G_HEADS):
                    o_int[h][j] = oj[h * SUB:(h + 1) * SUB]
                st = eg[j * SUB:j * SUB + 1, :] * st + _dot_tn(vv, kk)
                yield
            s_st[...] = st

        _round_robin([in_proj(), gla_diag(), gla_rec()] + [mlstm_head(h) for h in range(M_HEADS)])

        gn = g_norm_g[...]
        g_outs = []
        for h in range(G_HEADS):
            hs = slice(h * G_DV, (h + 1) * G_DV)
            og = o_diag[h] + jnp.concatenate(o_int[h], axis=0)
            rms = lax.rsqrt(jnp.mean(og * og, axis=1, keepdims=True) + LN_EPS)
            gate = gg[:, hs]
            g_outs.append(gate * _sigmoid(gate) * (og * rms * gn[:, hs]))

        y = jnp.concatenate(m_outs + g_outs, axis=1)
        mix = _dot(y, w_out[...])
        o_ref[0] = _layer_norm(ALPHA * h0 + mix, ln1_g[...], ln1_b[...])

    for par in range(2):
        pl.when(g % 2 == par)(functools.partial(step, par))


def _mixer_constants(tt):
    r = np.arange(tt)
    tri64 = ((r[None, :] <= r[:, None]) & (r[None, :] // CHUNK == r[:, None] // CHUNK))
    same16 = (r[None, :] // SUB == r[:, None] // SUB)
    tri16 = (r[None, :] <= r[:, None]) & same16
    gw = G_HEADS * G_DK
    gsel = np.zeros((SUB, gw, LANE), np.float32)
    for s in range(SUB):
        for h in range(G_HEADS):
            gsel[s, h * G_DK:(h + 1) * G_DK, h * SUB + s] = 1.0
    rsel = np.zeros((G_HEADS, LANE, WIN), np.float32)
    for h in range(G_HEADS):
        for s in range(SUB):
            rsel[h, h * SUB + s, s::SUB] = 1.0
    to_bf = lambda a: jnp.asarray(a.astype(np.float32), dtype=_BF)
    return to_bf(tri64), to_bf(tri16), to_bf(same16), to_bf(gsel), to_bf(rsel)


def _full(shape):
    nd = len(shape)
    return pl.BlockSpec(shape, lambda b, i, _nd=nd: (0,) * _nd)


def _mixer_call(x, lnin_g, lnin_b, w_m, w_g, w_gate, conv_w, conv_b, gate_bias, m_norm_g, glw, glb, g_norm_g,
                w_out, ln1_g, ln1_b, tt):
    bsz, t, d = x.shape
    consts = _mixer_constants(tt)
    params = (lnin_g, lnin_b, w_m, w_g, w_gate, conv_w, conv_b, gate_bias, m_norm_g, glw, glb, g_norm_g,
              w_out, ln1_g, ln1_b) + consts
    nt = t // tt
    ntiles = bsz * nt
    in_tile = pl.BlockSpec((1, tt, d), lambda g: (jnp.minimum(g, ntiles - 1) // nt, jnp.minimum(g, ntiles - 1) % nt, 0))
    out_tile = pl.BlockSpec((1, tt, d), lambda g: (jnp.maximum(g - 1, 0) // nt, jnp.maximum(g - 1, 0) % nt, 0))
    full = lambda shape: pl.BlockSpec(shape, lambda g, _nd=len(shape): (0,) * _nd)
    return pl.pallas_call(
        functools.partial(_mixer_kernel, tt=tt, nt=nt),
        out_shape=jax.ShapeDtypeStruct((bsz, t, d), _F32),
        grid=(ntiles + 1,),
        in_specs=[in_tile] + [full(p.shape) for p in params],
        out_specs=out_tile,
        scratch_shapes=[
            pltpu.VMEM((tt + 8, 2 * M_QK), _F32),
            pltpu.VMEM((M_HEADS, M_DK, M_DV), _F32),
            pltpu.VMEM((M_HEADS, 1, M_DK), _F32),
            pltpu.VMEM((M_HEADS, 1, 1), _F32),
            pltpu.VMEM((G_DV, G_HEADS * G_DK), _F32),
            pltpu.VMEM((2, tt, IN_COLS_R), _F32),
            pltpu.VMEM((2, tt, D_MODEL), _F32),
        ],
        compiler_params=pltpu.CompilerParams(
            dimension_semantics=("arbitrary",), vmem_limit_bytes=VMEM_LIMIT),
        name="mixer",
    )(x, *params)


def _memkv_kernel(mem_ref, wk_ref, wv_ref, wq_ref, wo_ref, wqk_ref, vo_ref):
    m = mem_ref[0]
    k = _dot(m, wk_ref[...])
    v = _dot(m, wv_ref[...])
    for h in range(X_HEADS):
        hs = slice(h * X_DH, (h + 1) * X_DH)
        ms = slice(h * N_MEM, (h + 1) * N_MEM)
        wqk_ref[0, :, ms] = (_dot_nt(wq_ref[:, hs], k[:, hs]) * (X_DH ** -0.5)).astype(_BF)
        vo_ref[0, ms, :] = _dot(v[:, hs], wo_ref[hs, :]).astype(_BF)


def _memkv_call(mem, wk, wv, wq, wo):
    bsz, nm, d = mem.shape
    blk = pl.BlockSpec((1, nm, d), lambda b: (b, 0, 0))
    wspec = pl.BlockSpec((d, d), lambda b: (0, 0))
    return pl.pallas_call(
        _memkv_kernel,
        out_shape=(jax.ShapeDtypeStruct((bsz, d, X_HEADS * nm), _BF),
                   jax.ShapeDtypeStruct((bsz, X_HEADS * nm, d), _BF)),
        grid=(bsz,),
        in_specs=[blk, wspec, wspec, wspec, wspec],
        out_specs=(pl.BlockSpec((1, d, X_HEADS * nm), lambda b: (b, 0, 0)),
                   pl.BlockSpec((1, X_HEADS * nm, d), lambda b: (b, 0, 0))),
        compiler_params=pltpu.CompilerParams(
            dimension_semantics=("arbitrary",), vmem_limit_bytes=VMEM_LIMIT),
        name="memkv",
    )(mem, wk, wv, wq, wo)


def _attnmlp_kernel(h_ref, wqk_ref, vo_ref, ln2_g, ln2_b, w1, w2, ln3_g, ln3_b, o_ref):
    rows = h_ref.shape[1] // ROW_SPLIT

    def chain(r, wait):
        for _ in range(wait):
            yield
        rs = slice(r * rows, (r + 1) * rows)
        h1 = h_ref[0, rs, :]
        s_all = _dot(h1, wqk_ref[0])
        yield
        probs = []
        for h in range(X_HEADS):
            s = s_all[:, h * N_MEM:(h + 1) * N_MEM]
            e = jnp.exp(s - jnp.max(s, axis=1, keepdims=True))
            probs.append((e * (1.0 / jnp.sum(e, axis=1, keepdims=True))).astype(_BF))
            yield
        xa = jnp.dot(jnp.concatenate(probs, axis=1), vo_ref[0], preferred_element_type=_F32)
        yield
        h2 = _layer_norm(ALPHA * h1 + xa, ln2_g[...], ln2_b[...])
        h2b = h2.astype(_BF)
        yield
        ff = jnp.zeros(h2.shape, _F32)
        for f in range(D_FF // FF_BLK):
            fs = slice(f * FF_BLK, (f + 1) * FF_BLK)
            hid = jnp.maximum(jnp.dot(h2b, w1[:, fs], preferred_element_type=_F32), 0.0)
            ff = ff + _dot(hid * hid, w2[fs, :])
            yield
        o_ref[0, rs, :] = _layer_norm(ALPHA * h2 + ff, ln3_g[...], ln3_b[...])

    _round_robin([chain(r, r * ROW_DELAY) for r in range(ROW_SPLIT)])


def _attnmlp_call(h1, wqk, vo, ln2_g, ln2_b, w1, w2, ln3_g, ln3_b, tm):
    bsz, t, d = h1.shape
    tile = pl.BlockSpec((1, tm, d), lambda b, i: (b, i, 0))
    per_batch = lambda a: pl.BlockSpec((1,) + a.shape[1:], lambda b, i: (b, 0, 0))
    params = (ln2_g, ln2_b, w1, w2, ln3_g, ln3_b)
    return pl.pallas_call(
        _attnmlp_kernel,
        out_shape=jax.ShapeDtypeStruct((bsz, t, d), _F32),
        grid=(bsz, t // tm),
        in_specs=[tile, per_batch(wqk), per_batch(vo)] + [_full(p.shape) for p in params],
        out_specs=tile,
        compiler_params=pltpu.CompilerParams(
            dimension_semantics=("arbitrary", "arbitrary"), vmem_limit_bytes=VMEM_LIMIT),
        name="attnmlp",
    )(h1, wqk, vo, *params)


def _time_tile(t, target):
    tt = min(t, target)
    assert t % tt == 0 and tt % WIN == 0, (t, tt)
    return tt


def kernel(x, mem, ln_in_g, ln_in_b, w_in, conv_w, conv_b, m_i_bias, m_f_bias, m_norm_g, g_lr_w, g_lr_b, g_norm_g, w_out, ln1_g, ln1_b, x_wq, x_wk, x_wv, x_wo, ln2_g, ln2_b, w_ff1, w_ff2, ln3_g, ln3_b):
    assert w_in.shape[0] == DEPTH == 1
    row = lambda a: a.reshape(1, -1).astype(_F32)
    wi = w_in[0]
    w_m = wi[:, O_MQ:O_MI].astype(_BF)
    w_g = wi[:, O_GQ:O_GLR].astype(_BF)
    w_gate = jnp.concatenate([wi[:, O_MI:O_GQ], wi[:, O_GLR:O_GLR + G_RANK],
                              jnp.zeros((D_MODEL, LANE - 2 * M_HEADS - G_RANK), wi.dtype)], axis=1).astype(_BF)
    gate_bias = jnp.concatenate([m_i_bias[0], m_f_bias[0],
                                 jnp.zeros((LANE - 2 * M_HEADS,), _F32)]).reshape(1, LANE)
    glw = jnp.zeros((LANE, G_HEADS * G_DK), _F32).at[2 * M_HEADS:2 * M_HEADS + G_RANK].set(g_lr_w[0]).astype(_BF)
    tt = _time_tile(x.shape[1], MIXER_ROWS)
    h1 = _mixer_call(x, row(ln_in_g), row(ln_in_b), w_m, w_g, w_gate, conv_w[0].astype(_F32), row(conv_b[0]),
                     gate_bias, row(m_norm_g[0]), glw, row(g_lr_b[0]), row(g_norm_g[0]), w_out[0].astype(_BF),
                     row(ln1_g[0]), row(ln1_b[0]), tt)
    wqk, vo = _memkv_call(mem, x_wk[0].astype(_BF), x_wv[0].astype(_BF), x_wq[0].astype(_BF), x_wo[0].astype(_BF))
    tm = _time_tile(x.shape[1], ATTN_ROWS)
    return _attnmlp_call(h1, wqk, vo, row(ln2_g[0]), row(ln2_b[0]),
                         w_ff1[0].astype(_BF), w_ff2[0].astype(_BF), row(ln3_g[0]), row(ln3_b[0]), tm)
```

```python
import functools

import numpy as np
import jax
import jax.numpy as jnp
from jax import lax
from jax.experimental import pallas as pl
from jax.experimental.pallas import tpu as pltpu

D_MODEL = 1024
CHUNK = 64
SUB = 16
WIN = 2 * CHUNK
N_MEM = 256
M_HEADS, M_DK, M_DV = 4, 128, 128
M_QK = M_HEADS * M_DK
G_HEADS, G_DK, G_DV = 4, 64, 128
G_RANK = 16
G_TAU = 16.0
X_HEADS = 4
X_DH = D_MODEL // X_HEADS
D_FF = 4 * D_MODEL
MIXER_ROWS = 256
ATTN_ROWS = 1024
FF_BLK = 512
IN_BLK = 512
ROW_SPLIT = 2
ROW_DELAY = 6
DEPTH = 1
ALPHA = (2.0 * DEPTH) ** 0.25
LN_EPS = 1e-5
LOG2E = 1.4426950408889634
LANE = 128

C_MQ, C_MK, C_MV, C_MO = 0, 512, 1024, 1536
C_GQ, C_GK, C_GV, C_GG = 2048, 2304, 2560, 3072
C_GATE = 3584
IN_COLS_R = C_GATE + LANE
O_MQ, O_MK, O_MV, O_MO, O_MI, O_MF = 0, 512, 1024, 1536, 2048, 2052
O_GQ, O_GK, O_GV, O_GG, O_GLR = 2056, 2312, 2568, 3080, 3592

VMEM_LIMIT = 56 * 1024 * 1024

_BF = jnp.bfloat16
_F32 = jnp.float32


def _dot(a, b):
    return jnp.dot(a.astype(_BF), b.astype(_BF), preferred_element_type=_F32)


def _dot_nt(a, b):
    return lax.dot_general(a.astype(_BF), b.astype(_BF), (((1,), (1,)), ((), ())),
                           preferred_element_type=_F32)


def _dot_tn(a, b):
    return lax.dot_general(a.astype(_BF), b.astype(_BF), (((0,), (0,)), ((), ())),
                           preferred_element_type=_F32)


def _split2(x):
    hi = x.astype(_BF)
    lo = (x - hi.astype(_F32)).astype(_BF)
    return jnp.concatenate([hi, lo], axis=1)


def _merge2(y, n):
    return y[:, 0:n] + y[:, n:2 * n]


def _layer_norm(x, g, b):
    mu = jnp.mean(x, axis=-1, keepdims=True)
    xc = x - mu
    var = jnp.mean(xc * xc, axis=-1, keepdims=True)
    return xc * lax.rsqrt(var + LN_EPS) * g + b


def _log_sigmoid(x):
    return -(jnp.maximum(-x, 0.0) + jnp.log(1.0 + jnp.exp(-jnp.abs(x))))


def _sigmoid(x):
    return 1.0 / (1.0 + jnp.exp(-x))


def _round_robin(chains):
    while chains:
        alive = []
        for ch in chains:
            try:
                next(ch)
                alive.append(ch)
            except StopIteration:
                pass
        chains = alive


def _mixer_kernel(x_ref, lnin_g, lnin_b, w_m, w_g, w_gate, conv_w, conv_b, gate_bias, m_norm_g, glw, glb, g_norm_g,
                  w_out, ln1_g, ln1_b, tri64, tri16, ones16, gsel, rsel, *rest, tt, nt, n_cast):
    cast_in, rest = rest[:n_cast], rest[n_cast:]
    o_ref, cast_out = rest[0], rest[1:1 + n_cast]
    uext, c_st, n_st, m_st, s_st, proj_buf, h0_buf = rest[1 + n_cast:]
    g = pl.program_id(0)

    for src_ref, dst_ref in zip(cast_in, cast_out):
        dst_ref[...] = src_ref[0].astype(_BF)

    @pl.when(g == 0)
    def _():
        proj_buf[1] = jnp.zeros(proj_buf.shape[1:], _F32)
        h0_buf[1] = jnp.zeros(h0_buf.shape[1:], _F32)

    @pl.when((g == 0) | ((g - 1) % nt == 0))
    def _():
        uext[0:8, :] = jnp.zeros((8, 2 * M_QK), _F32)
        c_st[...] = jnp.zeros(c_st.shape, _F32)
        n_st[...] = jnp.zeros(n_st.shape, _F32)
        m_st[...] = jnp.zeros(m_st.shape, _F32)
        s_st[...] = jnp.zeros(s_st.shape, _F32)

    def step(slot_a):
        slot_b = 1 - slot_a

        def in_proj():
            h0n = _layer_norm(x_ref[0], lnin_g[...], lnin_b[...])
            h0_buf[slot_a] = h0n
            h0b = h0n.astype(_BF)
            yield
            for w_ref, base in ((w_m, C_MQ), (w_g, C_GQ), (w_gate, C_GATE)):
                ncol = w_ref.shape[1]
                for c0 in range(0, ncol, IN_BLK):
                    c1 = min(c0 + IN_BLK, ncol)
                    proj_buf[slot_a, :, base + c0:base + c1] = jnp.dot(h0b, w_ref[:, c0:c1],
                                                                       preferred_element_type=_F32)
                    yield

        pb = proj_buf.at[slot_b]
        h0 = h0_buf[slot_b]

        uext[8:8 + tt, :] = pb[:, C_MQ:C_MV]
        cw = conv_w[...]
        conv = (uext[5:5 + tt, :] * cw[0:1, :] + uext[6:6 + tt, :] * cw[1:2, :]
                + uext[7:7 + tt, :] * cw[2:3, :] + uext[8:8 + tt, :] * cw[3:4, :]) + conv_b[...]
        uext[0:8, :] = uext[tt:tt + 8, :]
        qk = conv * _sigmoid(conv)
        mq = qk[:, 0:M_QK]
        mk = qk[:, M_QK:2 * M_QK] * (M_DK ** -0.5)
        mv = pb[:, C_MV:C_MO]
        mo = pb[:, C_MO:C_GQ]

        gb = pb[:, C_GATE:C_GATE + LANE] + gate_bias[...]
        lane = lax.broadcasted_iota(jnp.int32, (tt, LANE), 1)
        lf = _log_sigmoid(gb)
        bcum = _merge2(_dot(tri64[...], _split2(lf)), LANE)
        vcol = jnp.where((lane >= M_HEADS) & (lane < 2 * M_HEADS), bcum, gb) * LOG2E
        vrow = vcol.T

        gw = G_HEADS * G_DK
        la = _log_sigmoid(_dot(gb, glw[...]) + glb[...]) * (1.0 / G_TAU)
        la2 = _split2(la)
        bc = _merge2(_dot(tri16[...], la2), gw)
        bend = _merge2(_dot(ones16[...], la2), gw)
        gq = pb[:, C_GQ:C_GK] * (G_DK ** -0.5)
        gk = pb[:, C_GK:C_GV]
        gv = pb[:, C_GV:C_GG]
        gg = pb[:, C_GG:C_GATE]
        bc2 = bc * LOG2E
        bend2 = bend * LOG2E
        qh = gq * jnp.exp2(bc2)
        kh = gk * jnp.exp2(bend2 - bc2)
        eg = jnp.exp2(bend2)
        nsub = tt // SUB

        rr = lax.broadcasted_iota(jnp.int32, (WIN, WIN), 0)
        cc = lax.broadcasted_iota(jnp.int32, (WIN, WIN), 1)
        valid = (cc <= rr) & ((cc >= CHUNK) == (rr >= CHUNK))
        first = lax.broadcasted_iota(jnp.int32, (WIN, 1), 0) < CHUNK
        mg = m_norm_g[...]
        m_outs = [None] * M_HEADS

        def mlstm_head(h):
            hs = slice(h * M_DK, (h + 1) * M_DK)
            c_cur = c_st[h]
            n_cur = n_st[h]
            m_cur = m_st[h]
            win_outs = []
            for w in range(tt // WIN):
                rs = slice(w * WIN, (w + 1) * WIN)
                q, k, v = mq[rs, hs], mk[rs, hs], mv[rs, hs]
                li_col, b_col = vcol[rs, h:h + 1], vcol[rs, M_HEADS + h:M_HEADS + h + 1]
                li_row, b_row = vrow[h:h + 1, rs], vrow[M_HEADS + h:M_HEADS + h + 1, rs]
                dm = jnp.where(valid, b_col - b_row + li_row, -jnp.inf)
                md = jnp.max(dm, axis=1, keepdims=True)
                g0, g1 = b_col[CHUNK - 1:CHUNK, :], b_col[WIN - 1:WIN, :]
                a_col = jnp.where(first, g0, g1) - b_col + li_col
                yield
                am0 = jnp.max(a_col[0:CHUNK], axis=0, keepdims=True)
                am1 = jnp.max(a_col[CHUNK:WIN], axis=0, keepdims=True)
                wk = jnp.exp2(a_col - jnp.where(first, am0, am1)) * k
                s_raw = _dot_nt(q, k)
                yield
                cprev, nprev, mprev = [], [], []
                for c, (gc, am) in enumerate(((g0, am0), (g1, am1))):
                    cs = slice(c * CHUNK, (c + 1) * CHUNK)
                    cprev.append(c_cur)
                    nprev.append(n_cur)
                    mprev.append(m_cur)
                    u_mat = _dot_tn(wk[cs], v[cs])
                    u_vec = jnp.sum(wk[cs], axis=0, keepdims=True)
                    m_new = jnp.maximum(gc + m_cur, am)
                    dec = jnp.exp2(gc + m_cur - m_new)
                    inj = jnp.exp2(am - m_new)
                    c_cur = dec * c_cur + inj * u_mat
                    n_cur = dec * n_cur + inj * u_vec
                    m_cur = m_new
                yield
                inter = b_col + jnp.where(first, mprev[0], mprev[1])
                m_t = jnp.maximum(inter, md)
                sc = jnp.exp2(inter - m_t)
                s_mat = s_raw * jnp.exp2(dm - m_t)
                yield
                sv = _dot(s_mat, v)
                qc = jnp.concatenate([_dot(q[0:CHUNK], cprev[0]), _dot(q[CHUNK:WIN], cprev[1])], axis=0)
                qn = jnp.sum(q * jnp.where(first, nprev[0], nprev[1]), axis=1, keepdims=True)
                ssum = jnp.sum(s_mat, axis=1, keepdims=True)
                yield
                num = sc * qc + sv
                den = sc * qn + ssum
                hh = num * (1.0 / jnp.maximum(jnp.abs(den), jnp.exp2(-m_t)))
                mu = jnp.mean(hh, axis=1, keepdims=True)
                yield
                hc = hh - mu
                var = jnp.mean(hc * hc, axis=1, keepdims=True)
                yield
                win_outs.append(hc * lax.rsqrt(var + LN_EPS))
            c_st[h] = c_cur
            n_st[h] = n_cur
            m_st[h] = m_cur
            hn = win_outs[0] if len(win_outs) == 1 else jnp.concatenate(win_outs, axis=0)
            m_outs[h] = _sigmoid(mo[:, hs]) * (hn * mg[:, hs])

        o_diag = [None] * G_HEADS

        def gla_diag():
            gk3 = gk.reshape(nsub, SUB, gw)
            bc3 = bc2.reshape(nsub, SUB, gw)
            half = SUB // 2
            hi = lambda a: a.reshape(nsub, 2, half, a.shape[-1])[:, 1].reshape(nsub * half, a.shape[-1])
            gq_hi, bc2_hi = hi(gq), hi(bc2)
            p_acc = jnp.zeros((tt, LANE), _F32)
            p_hi = jnp.zeros((tt // 2, LANE), _F32)
            for s in range(SUB):
                if s < half:
                    kb = jnp.broadcast_to(gk3[:, s:s + 1, :], (nsub, SUB, gw)).reshape(tt, gw)
                    bb = jnp.broadcast_to(bc3[:, s:s + 1, :], (nsub, SUB, gw)).reshape(tt, gw)
                    y = gq * kb * jnp.exp2(jnp.minimum(bc2 - bb, 0.0))
                    p_acc = p_acc + _dot(y, gsel[s])
                else:
                    kb = jnp.broadcast_to(gk3[:, s:s + 1, :], (nsub, half, gw)).reshape(tt // 2, gw)
                    bb = jnp.broadcast_to(bc3[:, s:s + 1, :], (nsub, half, gw)).reshape(tt // 2, gw)
                    y = gq_hi * kb * jnp.exp2(jnp.minimum(bc2_hi - bb, 0.0))
                    p_hi = p_hi + _dot(y, gsel[s])
                yield
            p4 = p_acc.reshape(nsub, 2, half, LANE)
            p_acc = jnp.concatenate([p4[:, 0:1], p4[:, 1:2] + p_hi.reshape(nsub, 1, half, LANE)],
                                    axis=1).reshape(tt, LANE)
            p_bf = p_acc.astype(_BF)
            r2 = lax.broadcasted_iota(jnp.int32, (WIN, WIN), 0)
            c2 = lax.broadcasted_iota(jnp.int32, (WIN, WIN), 1)
            sub_mask = (c2 <= r2) & ((c2 // SUB) == (r2 // SUB))
            for h in range(G_HEADS):
                outs = []
                for w in range(tt // WIN):
                    rs = slice(w * WIN, (w + 1) * WIN)
                    a_full = jnp.dot(p_bf[rs], rsel[h], preferred_element_type=_F32)
                    a_h = jnp.where(sub_mask, a_full, 0.0)
                    outs.append(_dot(a_h, gv[rs, h * G_DV:(h + 1) * G_DV]))
                o_diag[h] = outs[0] if len(outs) == 1 else jnp.concatenate(outs, axis=0)
                yield

        head_of_lane = lax.broadcasted_iota(jnp.int32, (SUB, gw), 1) // G_DK
        o_int = [[None] * nsub for _ in range(G_HEADS)]

        def gla_rec():
            st = s_st[...]
            for j in range(nsub):
                js = slice(j * SUB, (j + 1) * SUB)
                qj, kj = qh[js], kh[js]
                qq = jnp.concatenate([jnp.where(head_of_lane == h, qj, 0.0) for h in range(G_HEADS)], axis=0)
                kk = jnp.concatenate([jnp.where(head_of_lane == h, kj, 0.0) for h in range(G_HEADS)], axis=0)
                vv = jnp.concatenate([gv[js, h * G_DV:(h + 1) * G_DV] for h in range(G_HEADS)], axis=0)
                oj = _dot_nt(qq, st)
                for h in range(G_HEADS):
                    o_int[h][j] = oj[h * SUB:(h + 1) * SUB]
                st = eg[j * SUB:j * SUB + 1, :] * st + _dot_tn(vv, kk)
                yield
            s_st[...] = st

        _round_robin([in_proj(), gla_diag(), gla_rec()] + [mlstm_head(h) for h in range(M_HEADS)])

        gn = g_norm_g[...]
        g_outs = []
        for h in range(G_HEADS):
            hs = slice(h * G_DV, (h + 1) * G_DV)
            og = o_diag[h] + jnp.concatenate(o_int[h], axis=0)
            rms = lax.rsqrt(jnp.mean(og * og, axis=1, keepdims=True) + LN_EPS)
            gate = gg[:, hs]
            g_outs.append(gate * _sigmoid(gate) * (og * rms * gn[:, hs]))

        y = jnp.concatenate(m_outs + g_outs, axis=1)
        mix = _dot(y, w_out[...])
        o_ref[0] = _layer_norm(ALPHA * h0 + mix, ln1_g[...], ln1_b[...])

    for par in range(2):
        pl.when(g % 2 == par)(functools.partial(step, par))


def _mixer_constants(tt):
    r = np.arange(tt)
    tri64 = ((r[None, :] <= r[:, None]) & (r[None, :] // CHUNK == r[:, None] // CHUNK))
    same16 = (r[None, :] // SUB == r[:, None] // SUB)
    tri16 = (r[None, :] <= r[:, None]) & same16
    gw = G_HEADS * G_DK
    gsel = np.zeros((SUB, gw, LANE), np.float32)
    for s in range(SUB):
        for h in range(G_HEADS):
            gsel[s, h * G_DK:(h + 1) * G_DK, h * SUB + s] = 1.0
    rsel = np.zeros((G_HEADS, LANE, WIN), np.float32)
    for h in range(G_HEADS):
        for s in range(SUB):
            rsel[h, h * SUB + s, s::SUB] = 1.0
    to_bf = lambda a: jnp.asarray(a.astype(np.float32), dtype=_BF)
    return to_bf(tri64), to_bf(tri16), to_bf(same16), to_bf(gsel), to_bf(rsel)


def _full(shape):
    nd = len(shape)
    return pl.BlockSpec(shape, lambda b, i, _nd=nd: (0,) * _nd)


def _mixer_call(x, lnin_g, lnin_b, w_m, w_g, w_gate, conv_w, conv_b, gate_bias, m_norm_g, glw, glb, g_norm_g,
                w_out, ln1_g, ln1_b, tt, to_cast):
    bsz, t, d = x.shape
    consts = _mixer_constants(tt)
    params = (lnin_g, lnin_b, w_m, w_g, w_gate, conv_w, conv_b, gate_bias, m_norm_g, glw, glb, g_norm_g,
              w_out, ln1_g, ln1_b) + consts
    nt = t // tt
    ntiles = bsz * nt
    in_tile = pl.BlockSpec((1, tt, d), lambda g: (jnp.minimum(g, ntiles - 1) // nt, jnp.minimum(g, ntiles - 1) % nt, 0))
    out_tile = pl.BlockSpec((1, tt, d), lambda g: (jnp.maximum(g - 1, 0) // nt, jnp.maximum(g - 1, 0) % nt, 0))
    full = lambda shape: pl.BlockSpec(shape, lambda g, _nd=len(shape): (0,) * _nd)
    cast_in = [pl.BlockSpec((1, w.shape[1] // ntiles, w.shape[2]), lambda g: (0, jnp.minimum(g, ntiles - 1), 0))
               for w in to_cast]
    cast_out = [pl.BlockSpec((w.shape[1] // ntiles, w.shape[2]), lambda g: (jnp.minimum(g, ntiles - 1), 0))
                for w in to_cast]
    assert all(w.shape[1] % (16 * ntiles) == 0 for w in to_cast)
    return pl.pallas_call(
        functools.partial(_mixer_kernel, tt=tt, nt=nt, n_cast=len(to_cast)),
        out_shape=[jax.ShapeDtypeStruct((bsz, t, d), _F32)]
        + [jax.ShapeDtypeStruct(w.shape[1:], _BF) for w in to_cast],
        grid=(ntiles + 1,),
        in_specs=[in_tile] + [full(p.shape) for p in params] + cast_in,
        out_specs=[out_tile] + cast_out,
        scratch_shapes=[
            pltpu.VMEM((tt + 8, 2 * M_QK), _F32),
            pltpu.VMEM((M_HEADS, M_DK, M_DV), _F32),
            pltpu.VMEM((M_HEADS, 1, M_DK), _F32),
            pltpu.VMEM((M_HEADS, 1, 1), _F32),
            pltpu.VMEM((G_DV, G_HEADS * G_DK), _F32),
            pltpu.VMEM((2, tt, IN_COLS_R), _F32),
            pltpu.VMEM((2, tt, D_MODEL), _F32),
        ],
        compiler_params=pltpu.CompilerParams(
            dimension_semantics=("arbitrary",), vmem_limit_bytes=VMEM_LIMIT),
        name="mixer",
    )(x, *params, *to_cast)


def _memkv_kernel(mem_ref, wk_ref, wv_ref, wq_ref, wo_ref, wqk_ref, vo_ref):
    m = mem_ref[0]
    k = _dot(m, wk_ref[...])
    v = _dot(m, wv_ref[...])
    for h in range(X_HEADS):
        hs = slice(h * X_DH, (h + 1) * X_DH)
        ms = slice(h * N_MEM, (h + 1) * N_MEM)
        wqk_ref[0, :, ms] = (_dot_nt(wq_ref[:, hs], k[:, hs]) * (X_DH ** -0.5)).astype(_BF)
        vo_ref[0, ms, :] = _dot(v[:, hs], wo_ref[hs, :]).astype(_BF)


def _memkv_call(mem, wk, wv, wq, wo):
    bsz, nm, d = mem.shape
    blk = pl.BlockSpec((1, nm, d), lambda b: (b, 0, 0))
    wspec = pl.BlockSpec((d, d), lambda b: (0, 0))
    return pl.pallas_call(
        _memkv_kernel,
        out_shape=(jax.ShapeDtypeStruct((bsz, d, X_HEADS * nm), _BF),
                   jax.ShapeDtypeStruct((bsz, X_HEADS * nm, d), _BF)),
        grid=(bsz,),
        in_specs=[blk, wspec, wspec, wspec, wspec],
        out_specs=(pl.BlockSpec((1, d, X_HEADS * nm), lambda b: (b, 0, 0)),
                   pl.BlockSpec((1, X_HEADS * nm, d), lambda b: (b, 0, 0))),
        compiler_params=pltpu.CompilerParams(
            dimension_semantics=("arbitrary",), vmem_limit_bytes=VMEM_LIMIT),
        name="memkv",
    )(mem, wk, wv, wq, wo)


def _attnmlp_kernel(h_ref, wqk_ref, vo_ref, ln2_g, ln2_b, w1, w2, ln3_g, ln3_b, o_ref):
    rows = h_ref.shape[1] // ROW_SPLIT

    def chain(r, wait):
        for _ in range(wait):
            yield
        rs = slice(r * rows, (r + 1) * rows)
        h1 = h_ref[0, rs, :]
        s_all = _dot(h1, wqk_ref[0])
        yield
        probs = []
        for h in range(X_HEADS):
            s = s_all[:, h * N_MEM:(h + 1) * N_MEM]
            e = jnp.exp(s - jnp.max(s, axis=1, keepdims=True))
            probs.append((e * (1.0 / jnp.sum(e, axis=1, keepdims=True))).astype(_BF))
            yield
        xa = jnp.dot(jnp.concatenate(probs, axis=1), vo_ref[0], preferred_element_type=_F32)
        yield
        h2 = _layer_norm(ALPHA * h1 + xa, ln2_g[...], ln2_b[...])
        h2b = h2.astype(_BF)
        yield
        ff = jnp.zeros(h2.shape, _F32)
        for f in range(D_FF // FF_BLK):
            fs = slice(f * FF_BLK, (f + 1) * FF_BLK)
            hid = jnp.maximum(jnp.dot(h2b, w1[:, fs], preferred_element_type=_F32), 0.0)
            ff = ff + _dot(hid * hid, w2[fs, :])
            yield
        o_ref[0, rs, :] = _layer_norm(ALPHA * h2 + ff, ln3_g[...], ln3_b[...])

    _round_robin([chain(r, r * ROW_DELAY) for r in range(ROW_SPLIT)])


def _attnmlp_call(h1, wqk, vo, ln2_g, ln2_b, w1, w2, ln3_g, ln3_b, tm):
    bsz, t, d = h1.shape
    tile = pl.BlockSpec((1, tm, d), lambda b, i: (b, i, 0))
    per_batch = lambda a: pl.BlockSpec((1,) + a.shape[1:], lambda b, i: (b, 0, 0))
    params = (ln2_g, ln2_b, w1, w2, ln3_g, ln3_b)
    return pl.pallas_call(
        _attnmlp_kernel,
        out_shape=jax.ShapeDtypeStruct((bsz, t, d), _F32),
        grid=(bsz, t // tm),
        in_specs=[tile, per_batch(wqk), per_batch(vo)] + [_full(p.shape) for p in params],
        out_specs=tile,
        compiler_params=pltpu.CompilerParams(
            dimension_semantics=("arbitrary", "arbitrary"), vmem_limit_bytes=VMEM_LIMIT),
        name="attnmlp",
    )(h1, wqk, vo, *params)


def _time_tile(t, target):
    tt = min(t, target)
    assert t % tt == 0 and tt % WIN == 0, (t, tt)
    return tt


def kernel(x, mem, ln_in_g, ln_in_b, w_in, conv_w, conv_b, m_i_bias, m_f_bias, m_norm_g, g_lr_w, g_lr_b, g_norm_g, w_out, ln1_g, ln1_b, x_wq, x_wk, x_wv, x_wo, ln2_g, ln2_b, w_ff1, w_ff2, ln3_g, ln3_b):
    assert w_in.shape[0] == DEPTH == 1
    row = lambda a: a.reshape(1, -1).astype(_F32)
    wi = w_in[0]
    w_m = wi[:, O_MQ:O_MI].astype(_BF)
    w_g = wi[:, O_GQ:O_GLR].astype(_BF)
    w_gate = jnp.concatenate([wi[:, O_MI:O_GQ], wi[:, O_GLR:O_GLR + G_RANK],
                              jnp.zeros((D_MODEL, LANE - 2 * M_HEADS - G_RANK), wi.dtype)], axis=1).astype(_BF)
    gate_bias = jnp.concatenate([m_i_bias[0], m_f_bias[0],
                                 jnp.zeros((LANE - 2 * M_HEADS,), _F32)]).reshape(1, LANE)
    glw = jnp.zeros((LANE, G_HEADS * G_DK), _F32).at[2 * M_HEADS:2 * M_HEADS + G_RANK].set(g_lr_w[0]).astype(_BF)
    tt = _time_tile(x.shape[1], MIXER_ROWS)
    h1, wk_b, wv_b, wq_b, wo_b, w1_b, w2_b = _mixer_call(
        x, row(ln_in_g), row(ln_in_b), w_m, w_g, w_gate, conv_w[0].astype(_F32), row(conv_b[0]),
        gate_bias, row(m_norm_g[0]), glw, row(g_lr_b[0]), row(g_norm_g[0]), w_out[0].astype(_BF),
        row(ln1_g[0]), row(ln1_b[0]), tt, (x_wk, x_wv, x_wq, x_wo, w_ff1, w_ff2))
    wqk, vo = _memkv_call(mem, wk_b, wv_b, wq_b, wo_b)
    tm = _time_tile(x.shape[1], ATTN_ROWS)
    return _attnmlp_call(h1, wqk, vo, row(ln2_g[0]), row(ln2_b[0]), w1_b, w2_b, row(ln3_g[0]), row(ln3_b[0]), tm)
```

```python
import functools

import numpy as np
import jax
import jax.numpy as jnp
from jax import lax
from jax.experimental import pallas as pl
from jax.experimental.pallas import tpu as pltpu

D_MODEL = 1024
CHUNK = 64
SUB = 16
WIN = 2 * CHUNK
N_MEM = 256
M_HEADS, M_DK, M_DV = 4, 128, 128
M_QK = M_HEADS * M_DK
G_HEADS, G_DK, G_DV = 4, 64, 128
G_RANK = 16
G_TAU = 16.0
X_HEADS = 4
X_DH = D_MODEL // X_HEADS
D_FF = 4 * D_MODEL
MIXER_ROWS = 256
ATTN_ROWS = 1024
FF_BLK = 512
IN_BLK = 512
ROW_SPLIT = 2
ROW_DELAY = 6
DEPTH = 1
ALPHA = (2.0 * DEPTH) ** 0.25
LN_EPS = 1e-5
LOG2E = 1.4426950408889634
LANE = 128

C_MQ, C_MK, C_MV, C_MO = 0, 512, 1024, 1536
C_GQ, C_GK, C_GV, C_GG = 2048, 2304, 2560, 3072
C_GATE = 3584
IN_COLS_R = C_GATE + LANE
O_MQ, O_MK, O_MV, O_MO, O_MI, O_MF = 0, 512, 1024, 1536, 2048, 2052
O_GQ, O_GK, O_GV, O_GG, O_GLR = 2056, 2312, 2568, 3080, 3592

VMEM_LIMIT = 56 * 1024 * 1024

_BF = jnp.bfloat16
_F32 = jnp.float32


def _dot(a, b):
    return jnp.dot(a.astype(_BF), b.astype(_BF), preferred_element_type=_F32)


def _dot_nt(a, b):
    return lax.dot_general(a.astype(_BF), b.astype(_BF), (((1,), (1,)), ((), ())),
                           preferred_element_type=_F32)


def _dot_tn(a, b):
    return lax.dot_general(a.astype(_BF), b.astype(_BF), (((0,), (0,)), ((), ())),
                           preferred_element_type=_F32)


def _split2(x):
    hi = x.astype(_BF)
    lo = (x - hi.astype(_F32)).astype(_BF)
    return jnp.concatenate([hi, lo], axis=1)


def _merge2(y, n):
    return y[:, 0:n] + y[:, n:2 * n]


def _layer_norm(x, g, b):
    mu = jnp.mean(x, axis=-1, keepdims=True)
    xc = x - mu
    var = jnp.mean(xc * xc, axis=-1, keepdims=True)
    return xc * lax.rsqrt(var + LN_EPS) * g + b


def _log_sigmoid(x):
    return -(jnp.maximum(-x, 0.0) + jnp.log(1.0 + jnp.exp(-jnp.abs(x))))


def _sigmoid(x):
    return 1.0 / (1.0 + jnp.exp(-x))


def _round_robin(chains):
    while chains:
        alive = []
        for ch in chains:
            try:
                next(ch)
                alive.append(ch)
            except StopIteration:
                pass
        chains = alive


def _mixer_kernel(x_ref, lnin_g, lnin_b, w_m, w_g, w_gate, conv_w, conv_b, gate_bias, m_norm_g, glw, glb, g_norm_g,
                  w_out, ln1_g, ln1_b, tri64, tri16, ones16, gsel, rsel, *rest, tt, nt, n_cast):
    cast_in, rest = rest[:n_cast], rest[n_cast:]
    o_ref, cast_out = rest[0], rest[1:1 + n_cast]
    uext, c_st, n_st, m_st, s_st, proj_buf, h0_buf = rest[1 + n_cast:]
    g = pl.program_id(0)

    @pl.when(g == 0)
    def _():
        proj_buf[1] = jnp.zeros(proj_buf.shape[1:], _F32)
        h0_buf[1] = jnp.zeros(h0_buf.shape[1:], _F32)

    @pl.when((g == 0) | ((g - 1) % nt == 0))
    def _():
        uext[0:8, :] = jnp.zeros((8, 2 * M_QK), _F32)
        c_st[...] = jnp.zeros(c_st.shape, _F32)
        n_st[...] = jnp.zeros(n_st.shape, _F32)
        m_st[...] = jnp.zeros(m_st.shape, _F32)
        s_st[...] = jnp.zeros(s_st.shape, _F32)

    def step(slot_a):
        slot_b = 1 - slot_a

        def in_proj():
            h0n = _layer_norm(x_ref[0], lnin_g[...], lnin_b[...])
            h0_buf[slot_a] = h0n
            h0b = h0n.astype(_BF)
            yield
            for w_ref, base in ((w_m, C_MQ), (w_g, C_GQ), (w_gate, C_GATE)):
                ncol = w_ref.shape[1]
                for c0 in range(0, ncol, IN_BLK):
                    c1 = min(c0 + IN_BLK, ncol)
                    proj_buf[slot_a, :, base + c0:base + c1] = jnp.dot(h0b, w_ref[:, c0:c1],
                                                                       preferred_element_type=_F32)
                    yield

        pb = proj_buf.at[slot_b]
        h0 = h0_buf[slot_b]

        uext[8:8 + tt, :] = pb[:, C_MQ:C_MV]
        cw = conv_w[...]
        conv = (uext[5:5 + tt, :] * cw[0:1, :] + uext[6:6 + tt, :] * cw[1:2, :]
                + uext[7:7 + tt, :] * cw[2:3, :] + uext[8:8 + tt, :] * cw[3:4, :]) + conv_b[...]
        uext[0:8, :] = uext[tt:tt + 8, :]
        qk = conv * _sigmoid(conv)
        mq = qk[:, 0:M_QK]
        mk = qk[:, M_QK:2 * M_QK] * (M_DK ** -0.5)
        mv = pb[:, C_MV:C_MO]
        mo = pb[:, C_MO:C_GQ]

        gb = pb[:, C_GATE:C_GATE + LANE] + gate_bias[...]
        lane = lax.broadcasted_iota(jnp.int32, (tt, LANE), 1)
        lf = _log_sigmoid(gb)
        bcum = _merge2(_dot(tri64[...], _split2(lf)), LANE)
        vcol = jnp.where((lane >= M_HEADS) & (lane < 2 * M_HEADS), bcum, gb) * LOG2E
        vrow = vcol.T

        gw = G_HEADS * G_DK
        la = _log_sigmoid(_dot(gb, glw[...]) + glb[...]) * (1.0 / G_TAU)
        la2 = _split2(la)
        bc = _merge2(_dot(tri16[...], la2), gw)
        bend = _merge2(_dot(ones16[...], la2), gw)
        gq = pb[:, C_GQ:C_GK] * (G_DK ** -0.5)
        gk = pb[:, C_GK:C_GV]
        gv = pb[:, C_GV:C_GG]
        gg = pb[:, C_GG:C_GATE]
        bc2 = bc * LOG2E
        bend2 = bend * LOG2E
        qh = gq * jnp.exp2(bc2)
        kh = gk * jnp.exp2(bend2 - bc2)
        eg = jnp.exp2(bend2)
        nsub = tt // SUB

        rr = lax.broadcasted_iota(jnp.int32, (WIN, WIN), 0)
        cc = lax.broadcasted_iota(jnp.int32, (WIN, WIN), 1)
        valid = (cc <= rr) & ((cc >= CHUNK) == (rr >= CHUNK))
        first = lax.broadcasted_iota(jnp.int32, (WIN, 1), 0) < CHUNK
        mg = m_norm_g[...]
        m_outs = [None] * M_HEADS

        def mlstm_head(h):
            hs = slice(h * M_DK, (h + 1) * M_DK)
            c_cur = c_st[h]
            n_cur = n_st[h]
            m_cur = m_st[h]
            win_outs = []
            for w in range(tt // WIN):
                rs = slice(w * WIN, (w + 1) * WIN)
                q, k, v = mq[rs, hs], mk[rs, hs], mv[rs, hs]
                li_col, b_col = vcol[rs, h:h + 1], vcol[rs, M_HEADS + h:M_HEADS + h + 1]
                li_row, b_row = vrow[h:h + 1, rs], vrow[M_HEADS + h:M_HEADS + h + 1, rs]
                dm = jnp.where(valid, b_col - b_row + li_row, -jnp.inf)
                md = jnp.max(dm, axis=1, keepdims=True)
                g0, g1 = b_col[CHUNK - 1:CHUNK, :], b_col[WIN - 1:WIN, :]
                a_col = jnp.where(first, g0, g1) - b_col + li_col
                yield
                am0 = jnp.max(a_col[0:CHUNK], axis=0, keepdims=True)
                am1 = jnp.max(a_col[CHUNK:WIN], axis=0, keepdims=True)
                wk = jnp.exp2(a_col - jnp.where(first, am0, am1)) * k
                s_raw = _dot_nt(q, k)
                yield
                cprev, nprev, mprev = [], [], []
                for c, (gc, am) in enumerate(((g0, am0), (g1, am1))):
                    cs = slice(c * CHUNK, (c + 1) * CHUNK)
                    cprev.append(c_cur)
                    nprev.append(n_cur)
                    mprev.append(m_cur)
                    u_mat = _dot_tn(wk[cs], v[cs])
                    u_vec = jnp.sum(wk[cs], axis=0, keepdims=True)
                    m_new = jnp.maximum(gc + m_cur, am)
                    dec = jnp.exp2(gc + m_cur - m_new)
                    inj = jnp.exp2(am - m_new)
                    c_cur = dec * c_cur + inj * u_mat
                    n_cur = dec * n_cur + inj * u_vec
                    m_cur = m_new
                yield
                inter = b_col + jnp.where(first, mprev[0], mprev[1])
                m_t = jnp.maximum(inter, md)
                sc = jnp.exp2(inter - m_t)
                s_mat = s_raw * jnp.exp2(dm - m_t)
                yield
                sv = _dot(s_mat, v)
                qc = jnp.concatenate([_dot(q[0:CHUNK], cprev[0]), _dot(q[CHUNK:WIN], cprev[1])], axis=0)
                qn = jnp.sum(q * jnp.where(first, nprev[0], nprev[1]), axis=1, keepdims=True)
                ssum = jnp.sum(s_mat, axis=1, keepdims=True)
                yield
                num = sc * qc + sv
                den = sc * qn + ssum
                hh = num * (1.0 / jnp.maximum(jnp.abs(den), jnp.exp2(-m_t)))
                mu = jnp.mean(hh, axis=1, keepdims=True)
                yield
                hc = hh - mu
                var = jnp.mean(hc * hc, axis=1, keepdims=True)
                yield
                win_outs.append(hc * lax.rsqrt(var + LN_EPS))
            c_st[h] = c_cur
            n_st[h] = n_cur
            m_st[h] = m_cur
            hn = win_outs[0] if len(win_outs) == 1 else jnp.concatenate(win_outs, axis=0)
            m_outs[h] = _sigmoid(mo[:, hs]) * (hn * mg[:, hs])

        o_diag = [None] * G_HEADS

        def gla_diag():
            gk3 = gk.reshape(nsub, SUB, gw)
            bc3 = bc2.reshape(nsub, SUB, gw)
            half = SUB // 2
            hi = lambda a: a.reshape(nsub, 2, half, a.shape[-1])[:, 1].reshape(nsub * half, a.shape[-1])
            gq_hi, bc2_hi = hi(gq), hi(bc2)
            p_acc = jnp.zeros((tt, LANE), _F32)
            p_hi = jnp.zeros((tt // 2, LANE), _F32)
            for s in range(SUB):
                if s < half:
                    kb = jnp.broadcast_to(gk3[:, s:s + 1, :], (nsub, SUB, gw)).reshape(tt, gw)
                    bb = jnp.broadcast_to(bc3[:, s:s + 1, :], (nsub, SUB, gw)).reshape(tt, gw)
                    y = gq * kb * jnp.exp2(jnp.minimum(bc2 - bb, 0.0))
                    p_acc = p_acc + _dot(y, gsel[s])
                else:
                    kb = jnp.broadcast_to(gk3[:, s:s + 1, :], (nsub, half, gw)).reshape(tt // 2, gw)
                    bb = jnp.broadcast_to(bc3[:, s:s + 1, :], (nsub, half, gw)).reshape(tt // 2, gw)
                    y = gq_hi * kb * jnp.exp2(jnp.minimum(bc2_hi - bb, 0.0))
                    p_hi = p_hi + _dot(y, gsel[s])
                yield
            p4 = p_acc.reshape(nsub, 2, half, LANE)
            p_acc = jnp.concatenate([p4[:, 0:1], p4[:, 1:2] + p_hi.reshape(nsub, 1, half, LANE)],
                                    axis=1).reshape(tt, LANE)
            p_bf = p_acc.astype(_BF)
            r2 = lax.broadcasted_iota(jnp.int32, (WIN, WIN), 0)
            c2 = lax.broadcasted_iota(jnp.int32, (WIN, WIN), 1)
            sub_mask = (c2 <= r2) & ((c2 // SUB) == (r2 // SUB))
            for h in range(G_HEADS):
                outs = []
                for w in range(tt // WIN):
                    rs = slice(w * WIN, (w + 1) * WIN)
                    a_full = jnp.dot(p_bf[rs], rsel[h], preferred_element_type=_F32)
                    a_h = jnp.where(sub_mask, a_full, 0.0)
                    outs.append(_dot(a_h, gv[rs, h * G_DV:(h + 1) * G_DV]))
                o_diag[h] = outs[0] if len(outs) == 1 else jnp.concatenate(outs, axis=0)
                yield

        head_of_lane = lax.broadcasted_iota(jnp.int32, (SUB, gw), 1) // G_DK
        o_int = [[None] * nsub for _ in range(G_HEADS)]

        def gla_rec():
            st = s_st[...]
            for j in range(nsub):
                js = slice(j * SUB, (j + 1) * SUB)
                qj, kj = qh[js], kh[js]
                qq = jnp.concatenate([jnp.where(head_of_lane == h, qj, 0.0) for h in range(G_HEADS)], axis=0)
                kk = jnp.concatenate([jnp.where(head_of_lane == h, kj, 0.0) for h in range(G_HEADS)], axis=0)
                vv = jnp.concatenate([gv[js, h * G_DV:(h + 1) * G_DV] for h in range(G_HEADS)], axis=0)
                oj = _dot_nt(qq, st)
                for h in range(G_HEADS):
                    o_int[h][j] = oj[h * SUB:(h + 1) * SUB]
                st = eg[j * SUB:j * SUB + 1, :] * st + _dot_tn(vv, kk)
                yield
            s_st[...] = st

        def cast_weights():
            for src_ref, dst_ref in zip(cast_in, cast_out):
                dst_ref[...] = src_ref[0].astype(_BF)
                yield

        _round_robin([in_proj(), gla_diag(), gla_rec()] + [mlstm_head(h) for h in range(M_HEADS)]
                     + [cast_weights()])

        gn = g_norm_g[...]
        g_outs = []
        for h in range(G_HEADS):
            hs = slice(h * G_DV, (h + 1) * G_DV)
            og = o_diag[h] + jnp.concatenate(o_int[h], axis=0)
            rms = lax.rsqrt(jnp.mean(og * og, axis=1, keepdims=True) + LN_EPS)
            gate = gg[:, hs]
            g_outs.append(gate * _sigmoid(gate) * (og * rms * gn[:, hs]))

        y = jnp.concatenate(m_outs + g_outs, axis=1)
        mix = _dot(y, w_out[...])
        o_ref[0] = _layer_norm(ALPHA * h0 + mix, ln1_g[...], ln1_b[...])

    for par in range(2):
        pl.when(g % 2 == par)(functools.partial(step, par))


def _mixer_constants(tt):
    r = np.arange(tt)
    tri64 = ((r[None, :] <= r[:, None]) & (r[None, :] // CHUNK == r[:, None] // CHUNK))
    same16 = (r[None, :] // SUB == r[:, None] // SUB)
    tri16 = (r[None, :] <= r[:, None]) & same16
    gw = G_HEADS * G_DK
    gsel = np.zeros((SUB, gw, LANE), np.float32)
    for s in range(SUB):
        for h in range(G_HEADS):
            gsel[s, h * G_DK:(h + 1) * G_DK, h * SUB + s] = 1.0
    rsel = np.zeros((G_HEADS, LANE, WIN), np.float32)
    for h in range(G_HEADS):
        for s in range(SUB):
            rsel[h, h * SUB + s, s::SUB] = 1.0
    to_bf = lambda a: jnp.asarray(a.astype(np.float32), dtype=_BF)
    return to_bf(tri64), to_bf(tri16), to_bf(same16), to_bf(gsel), to_bf(rsel)


def _full(shape):
    nd = len(shape)
    return pl.BlockSpec(shape, lambda b, i, _nd=nd: (0,) * _nd)


def _mixer_call(x, lnin_g, lnin_b, w_m, w_g, w_gate, conv_w, conv_b, gate_bias, m_norm_g, glw, glb, g_norm_g,
                w_out, ln1_g, ln1_b, tt, to_cast):
    bsz, t, d = x.shape
    consts = _mixer_constants(tt)
    params = (lnin_g, lnin_b, w_m, w_g, w_gate, conv_w, conv_b, gate_bias, m_norm_g, glw, glb, g_norm_g,
              w_out, ln1_g, ln1_b) + consts
    nt = t // tt
    ntiles = bsz * nt
    in_tile = pl.BlockSpec((1, tt, d), lambda g: (jnp.minimum(g, ntiles - 1) // nt, jnp.minimum(g, ntiles - 1) % nt, 0))
    out_tile = pl.BlockSpec((1, tt, d), lambda g: (jnp.maximum(g - 1, 0) // nt, jnp.maximum(g - 1, 0) % nt, 0))
    full = lambda shape: pl.BlockSpec(shape, lambda g, _nd=len(shape): (0,) * _nd)
    cast_in = [pl.BlockSpec((1, w.shape[1] // ntiles, w.shape[2]), lambda g: (0, jnp.minimum(g, ntiles - 1), 0))
               for w in to_cast]
    cast_out = [pl.BlockSpec((w.shape[1] // ntiles, w.shape[2]), lambda g: (jnp.minimum(g, ntiles - 1), 0))
                for w in to_cast]
    assert all(w.shape[1] % (16 * ntiles) == 0 for w in to_cast)
    return pl.pallas_call(
        functools.partial(_mixer_kernel, tt=tt, nt=nt, n_cast=len(to_cast)),
        out_shape=[jax.ShapeDtypeStruct((bsz, t, d), _F32)]
        + [jax.ShapeDtypeStruct(w.shape[1:], _BF) for w in to_cast],
        grid=(ntiles + 1,),
        in_specs=[in_tile] + [full(p.shape) for p in params] + cast_in,
        out_specs=[out_tile] + cast_out,
        scratch_shapes=[
            pltpu.VMEM((tt + 8, 2 * M_QK), _F32),
            pltpu.VMEM((M_HEADS, M_DK, M_DV), _F32),
            pltpu.VMEM((M_HEADS, 1, M_DK), _F32),
            pltpu.VMEM((M_HEADS, 1, 1), _F32),
            pltpu.VMEM((G_DV, G_HEADS * G_DK), _F32),
            pltpu.VMEM((2, tt, IN_COLS_R), _F32),
            pltpu.VMEM((2, tt, D_MODEL), _F32),
        ],
        compiler_params=pltpu.CompilerParams(
            dimension_semantics=("arbitrary",), vmem_limit_bytes=VMEM_LIMIT),
        name="mixer",
    )(x, *params, *to_cast)


def _memkv_kernel(mem_ref, wk_ref, wv_ref, wq_ref, wo_ref, wqk_ref, vo_ref):
    m = mem_ref[0]
    k = _dot(m, wk_ref[...])
    v = _dot(m, wv_ref[...])
    for h in range(X_HEADS):
        hs = slice(h * X_DH, (h + 1) * X_DH)
        ms = slice(h * N_MEM, (h + 1) * N_MEM)
        wqk_ref[0, :, ms] = (_dot_nt(wq_ref[:, hs], k[:, hs]) * (X_DH ** -0.5)).astype(_BF)
        vo_ref[0, ms, :] = _dot(v[:, hs], wo_ref[hs, :]).astype(_BF)


def _memkv_call(mem, wk, wv, wq, wo):
    bsz, nm, d = mem.shape
    blk = pl.BlockSpec((1, nm, d), lambda b: (b, 0, 0))
    wspec = pl.BlockSpec((d, d), lambda b: (0, 0))
    return pl.pallas_call(
        _memkv_kernel,
        out_shape=(jax.ShapeDtypeStruct((bsz, d, X_HEADS * nm), _BF),
                   jax.ShapeDtypeStruct((bsz, X_HEADS * nm, d), _BF)),
        grid=(bsz,),
        in_specs=[blk, wspec, wspec, wspec, wspec],
        out_specs=(pl.BlockSpec((1, d, X_HEADS * nm), lambda b: (b, 0, 0)),
                   pl.BlockSpec((1, X_HEADS * nm, d), lambda b: (b, 0, 0))),
        compiler_params=pltpu.CompilerParams(
            dimension_semantics=("arbitrary",), vmem_limit_bytes=VMEM_LIMIT),
        name="memkv",
    )(mem, wk, wv, wq, wo)


def _attnmlp_kernel(h_ref, wqk_ref, vo_ref, ln2_g, ln2_b, w1, w2, ln3_g, ln3_b, o_ref):
    rows = h_ref.shape[1] // ROW_SPLIT

    def chain(r, wait):
        for _ in range(wait):
            yield
        rs = slice(r * rows, (r + 1) * rows)
        h1 = h_ref[0, rs, :]
        s_all = _dot(h1, wqk_ref[0])
        yield
        probs = []
        for h in range(X_HEADS):
            s = s_all[:, h * N_MEM:(h + 1) * N_MEM]
            e = jnp.exp(s - jnp.max(s, axis=1, keepdims=True))
            probs.append((e * (1.0 / jnp.sum(e, axis=1, keepdims=True))).astype(_BF))
            yield
        xa = jnp.dot(jnp.concatenate(probs, axis=1), vo_ref[0], preferred_element_type=_F32)
        yield
        h2 = _layer_norm(ALPHA * h1 + xa, ln2_g[...], ln2_b[...])
        h2b = h2.astype(_BF)
        yield
        ff = jnp.zeros(h2.shape, _F32)
        for f in range(D_FF // FF_BLK):
            fs = slice(f * FF_BLK, (f + 1) * FF_BLK)
            hid = jnp.maximum(jnp.dot(h2b, w1[:, fs], preferred_element_type=_F32), 0.0)
            ff = ff + _dot(hid * hid, w2[fs, :])
            yield
        o_ref[0, rs, :] = _layer_norm(ALPHA * h2 + ff, ln3_g[...], ln3_b[...])

    _round_robin([chain(r, r * ROW_DELAY) for r in range(ROW_SPLIT)])


def _attnmlp_call(h1, wqk, vo, ln2_g, ln2_b, w1, w2, ln3_g, ln3_b, tm):
    bsz, t, d = h1.shape
    tile = pl.BlockSpec((1, tm, d), lambda b, i: (b, i, 0))
    per_batch = lambda a: pl.BlockSpec((1,) + a.shape[1:], lambda b, i: (b, 0, 0))
    params = (ln2_g, ln2_b, w1, w2, ln3_g, ln3_b)
    return pl.pallas_call(
        _attnmlp_kernel,
        out_shape=jax.ShapeDtypeStruct((bsz, t, d), _F32),
        grid=(bsz, t // tm),
        in_specs=[tile, per_batch(wqk), per_batch(vo)] + [_full(p.shape) for p in params],
        out_specs=tile,
        compiler_params=pltpu.CompilerParams(
            dimension_semantics=("arbitrary", "arbitrary"), vmem_limit_bytes=VMEM_LIMIT),
        name="attnmlp",
    )(h1, wqk, vo, *params)


def _time_tile(t, target):
    tt = min(t, target)
    assert t % tt == 0 and tt % WIN == 0, (t, tt)
    return tt


def kernel(x, mem, ln_in_g, ln_in_b, w_in, conv_w, conv_b, m_i_bias, m_f_bias, m_norm_g, g_lr_w, g_lr_b, g_norm_g, w_out, ln1_g, ln1_b, x_wq, x_wk, x_wv, x_wo, ln2_g, ln2_b, w_ff1, w_ff2, ln3_g, ln3_b):
    assert w_in.shape[0] == DEPTH == 1
    row = lambda a: a.reshape(1, -1).astype(_F32)
    wi = w_in[0]
    w_m = wi[:, O_MQ:O_MI].astype(_BF)
    w_g = wi[:, O_GQ:O_GLR].astype(_BF)
    w_gate = jnp.concatenate([wi[:, O_MI:O_GQ], wi[:, O_GLR:O_GLR + G_RANK],
                              jnp.zeros((D_MODEL, LANE - 2 * M_HEADS - G_RANK), wi.dtype)], axis=1).astype(_BF)
    gate_bias = jnp.concatenate([m_i_bias[0], m_f_bias[0],
                                 jnp.zeros((LANE - 2 * M_HEADS,), _F32)]).reshape(1, LANE)
    glw = jnp.zeros((LANE, G_HEADS * G_DK), _F32).at[2 * M_HEADS:2 * M_HEADS + G_RANK].set(g_lr_w[0]).astype(_BF)
    tt = _time_tile(x.shape[1], MIXER_ROWS)
    h1, wk_b, wv_b, wq_b, wo_b, w1_b, w2_b = _mixer_call(
        x, row(ln_in_g), row(ln_in_b), w_m, w_g, w_gate, conv_w[0].astype(_F32), row(conv_b[0]),
        gate_bias, row(m_norm_g[0]), glw, row(g_lr_b[0]), row(g_norm_g[0]), w_out[0].astype(_BF),
        row(ln1_g[0]), row(ln1_b[0]), tt, (x_wk, x_wv, x_wq, x_wo, w_ff1, w_ff2))
    wqk, vo = _memkv_call(mem, wk_b, wv_b, wq_b, wo_b)
    tm = _time_tile(x.shape[1], ATTN_ROWS)
    return _attnmlp_call(h1, wqk, vo, row(ln2_g[0]), row(ln2_b[0]), w1_b, w2_b, row(ln3_g[0]), row(ln3_b[0]), tm)
```

```python
import functools

import numpy as np
import jax
import jax.numpy as jnp
from jax import lax
from jax.experimental import pallas as pl
from jax.experimental.pallas import tpu as pltpu

D_MODEL = 1024
CHUNK = 64
SUB = 16
WIN = 2 * CHUNK
N_MEM = 256
M_HEADS, M_DK, M_DV = 4, 128, 128
M_QK = M_HEADS * M_DK
G_HEADS, G_DK, G_DV = 4, 64, 128
G_RANK = 16
G_TAU = 16.0
X_HEADS = 4
X_DH = D_MODEL // X_HEADS
D_FF = 4 * D_MODEL
MIXER_ROWS = 256
ATTN_ROWS = 1024
FF_BLK = 512
IN_BLK = 512
ROW_SPLIT = 2
ROW_DELAY = 6
DEPTH = 1
ALPHA = (2.0 * DEPTH) ** 0.25
LN_EPS = 1e-5
LOG2E = 1.4426950408889634
LANE = 128

C_MQ, C_MK, C_MV, C_MO = 0, 512, 1024, 1536
C_GQ, C_GK, C_GV, C_GG = 2048, 2304, 2560, 3072
C_GATE = 3584
IN_COLS_R = C_GATE + LANE
O_MQ, O_MK, O_MV, O_MO, O_MI, O_MF = 0, 512, 1024, 1536, 2048, 2052
O_GQ, O_GK, O_GV, O_GG, O_GLR = 2056, 2312, 2568, 3080, 3592

VMEM_LIMIT = 56 * 1024 * 1024

_BF = jnp.bfloat16
_F32 = jnp.float32


def _dot(a, b):
    return jnp.dot(a.astype(_BF), b.astype(_BF), preferred_element_type=_F32)


def _dot_nt(a, b):
    return lax.dot_general(a.astype(_BF), b.astype(_BF), (((1,), (1,)), ((), ())),
                           preferred_element_type=_F32)


def _dot_tn(a, b):
    return lax.dot_general(a.astype(_BF), b.astype(_BF), (((0,), (0,)), ((), ())),
                           preferred_element_type=_F32)


def _split2(x):
    hi = x.astype(_BF)
    lo = (x - hi.astype(_F32)).astype(_BF)
    return jnp.concatenate([hi, lo], axis=1)


def _merge2(y, n):
    return y[:, 0:n] + y[:, n:2 * n]


def _layer_norm(x, g, b):
    mu = jnp.mean(x, axis=-1, keepdims=True)
    xc = x - mu
    var = jnp.mean(xc * xc, axis=-1, keepdims=True)
    return xc * lax.rsqrt(var + LN_EPS) * g + b


def _log_sigmoid(x):
    return -(jnp.maximum(-x, 0.0) + jnp.log(1.0 + jnp.exp(-jnp.abs(x))))


def _sigmoid(x):
    return 1.0 / (1.0 + jnp.exp(-x))


def _round_robin(chains):
    while chains:
        alive = []
        for ch in chains:
            try:
                next(ch)
                alive.append(ch)
            except StopIteration:
                pass
        chains = alive


def _mixer_kernel(x_ref, lnin_g, lnin_b, w_m, w_g, w_gate, conv_w, conv_b, gate_bias, m_norm_g, glw, glb, g_norm_g,
                  w_out, ln1_g, ln1_b, tri64, tri16, ones16, gsel, rsel, *rest, tt, nt, n_cast):
    cast_in, rest = rest[:n_cast], rest[n_cast:]
    o_ref, cast_out = rest[0], rest[1:1 + n_cast]
    uext, c_st, n_st, m_st, s_st, proj_buf, h0_buf = rest[1 + n_cast:]
    g = pl.program_id(0)

    @pl.when(g == 0)
    def _():
        proj_buf[1] = jnp.zeros(proj_buf.shape[1:], _F32)
        h0_buf[1] = jnp.zeros(h0_buf.shape[1:], _F32)

    @pl.when((g == 0) | ((g - 1) % nt == 0))
    def _():
        uext[0:8, :] = jnp.zeros((8, 2 * M_QK), _F32)
        c_st[...] = jnp.zeros(c_st.shape, _F32)
        n_st[...] = jnp.zeros(n_st.shape, _F32)
        m_st[...] = jnp.zeros(m_st.shape, _F32)
        s_st[...] = jnp.zeros(s_st.shape, _F32)

    def step(slot_a):
        slot_b = 1 - slot_a

        def in_proj():
            h0n = _layer_norm(x_ref[0], lnin_g[...], lnin_b[...])
            h0_buf[slot_a] = h0n
            h0b = h0n.astype(_BF)
            yield
            for w_ref, base in ((w_m, C_MQ), (w_g, C_GQ), (w_gate, C_GATE)):
                ncol = w_ref.shape[1]
                for c0 in range(0, ncol, IN_BLK):
                    c1 = min(c0 + IN_BLK, ncol)
                    proj_buf[slot_a, :, base + c0:base + c1] = jnp.dot(h0b, w_ref[:, c0:c1],
                                                                       preferred_element_type=_F32)
                    yield

        pb = proj_buf.at[slot_b]
        h0 = h0_buf[slot_b]

        uext[8:8 + tt, :] = pb[:, C_MQ:C_MV]
        cw = conv_w[...]
        conv = (uext[5:5 + tt, :] * cw[0:1, :] + uext[6:6 + tt, :] * cw[1:2, :]
                + uext[7:7 + tt, :] * cw[2:3, :] + uext[8:8 + tt, :] * cw[3:4, :]) + conv_b[...]
        uext[0:8, :] = uext[tt:tt + 8, :]
        qk = conv * _sigmoid(conv)
        mq = qk[:, 0:M_QK]
        mk = qk[:, M_QK:2 * M_QK] * (M_DK ** -0.5)
        mv = pb[:, C_MV:C_MO]
        mo = pb[:, C_MO:C_GQ]

        gb = pb[:, C_GATE:C_GATE + LANE] + gate_bias[...]
        lane = lax.broadcasted_iota(jnp.int32, (tt, LANE), 1)
        lf = _log_sigmoid(gb)
        bcum = _merge2(_dot(tri64[...], _split2(lf)), LANE)
        vcol = jnp.where((lane >= M_HEADS) & (lane < 2 * M_HEADS), bcum, gb) * LOG2E
        vrow = vcol.T

        gw = G_HEADS * G_DK
        la = _log_sigmoid(_dot(gb, glw[...]) + glb[...]) * (1.0 / G_TAU)
        la2 = _split2(la)
        bc = _merge2(_dot(tri16[...], la2), gw)
        bend = _merge2(_dot(ones16[...], la2), gw)
        gq = pb[:, C_GQ:C_GK] * (G_DK ** -0.5)
        gk = pb[:, C_GK:C_GV]
        gv = pb[:, C_GV:C_GG]
        gg = pb[:, C_GG:C_GATE]
        bc2 = bc * LOG2E
        bend2 = bend * LOG2E
        qh = gq * jnp.exp2(bc2)
        kh = gk * jnp.exp2(bend2 - bc2)
        eg = jnp.exp2(bend2)
        nsub = tt // SUB

        rr = lax.broadcasted_iota(jnp.int32, (WIN, WIN), 0)
        cc = lax.broadcasted_iota(jnp.int32, (WIN, WIN), 1)
        valid = (cc <= rr) & ((cc >= CHUNK) == (rr >= CHUNK))
        first = lax.broadcasted_iota(jnp.int32, (WIN, 1), 0) < CHUNK
        mg = m_norm_g[...]
        m_outs = [None] * M_HEADS

        def mlstm_head(h):
            hs = slice(h * M_DK, (h + 1) * M_DK)
            c_cur = c_st[h]
            n_cur = n_st[h]
            m_cur = m_st[h]
            win_outs = []
            for w in range(tt // WIN):
                rs = slice(w * WIN, (w + 1) * WIN)
                q, k, v = mq[rs, hs], mk[rs, hs], mv[rs, hs]
                li_col, b_col = vcol[rs, h:h + 1], vcol[rs, M_HEADS + h:M_HEADS + h + 1]
                li_row, b_row = vrow[h:h + 1, rs], vrow[M_HEADS + h:M_HEADS + h + 1, rs]
                dm = jnp.where(valid, b_col - b_row + li_row, -jnp.inf)
                md = jnp.max(dm, axis=1, keepdims=True)
                g0, g1 = b_col[CHUNK - 1:CHUNK, :], b_col[WIN - 1:WIN, :]
                a_col = jnp.where(first, g0, g1) - b_col + li_col
                yield
                am0 = jnp.max(a_col[0:CHUNK], axis=0, keepdims=True)
                am1 = jnp.max(a_col[CHUNK:WIN], axis=0, keepdims=True)
                wk = jnp.exp2(a_col - jnp.where(first, am0, am1)) * k
                s_raw = _dot_nt(q, k)
                yield
                cprev, nprev, mprev = [], [], []
                for c, (gc, am) in enumerate(((g0, am0), (g1, am1))):
                    cs = slice(c * CHUNK, (c + 1) * CHUNK)
                    cprev.append(c_cur)
                    nprev.append(n_cur)
                    mprev.append(m_cur)
                    u_mat = _dot_tn(wk[cs], v[cs])
                    u_vec = jnp.sum(wk[cs], axis=0, keepdims=True)
                    m_new = jnp.maximum(gc + m_cur, am)
                    dec = jnp.exp2(gc + m_cur - m_new)
                    inj = jnp.exp2(am - m_new)
                    c_cur = dec * c_cur + inj * u_mat
                    n_cur = dec * n_cur + inj * u_vec
                    m_cur = m_new
                yield
                inter = b_col + jnp.where(first, mprev[0], mprev[1])
                m_t = jnp.maximum(inter, md)
                sc = jnp.exp2(inter - m_t)
                s_mat = s_raw * jnp.exp2(dm - m_t)
                yield
                sv = _dot(s_mat, v)
                qc = jnp.concatenate([_dot(q[0:CHUNK], cprev[0]), _dot(q[CHUNK:WIN], cprev[1])], axis=0)
                qn = jnp.sum(q * jnp.where(first, nprev[0], nprev[1]), axis=1, keepdims=True)
                ssum = jnp.sum(s_mat, axis=1, keepdims=True)
                yield
                num = sc * qc + sv
                den = sc * qn + ssum
                hh = num * (1.0 / jnp.maximum(jnp.abs(den), jnp.exp2(-m_t)))
                mu = jnp.mean(hh, axis=1, keepdims=True)
                yield
                hc = hh - mu
                var = jnp.mean(hc * hc, axis=1, keepdims=True)
                yield
                win_outs.append(hc * lax.rsqrt(var + LN_EPS))
            c_st[h] = c_cur
            n_st[h] = n_cur
            m_st[h] = m_cur
            hn = win_outs[0] if len(win_outs) == 1 else jnp.concatenate(win_outs, axis=0)
            m_outs[h] = _sigmoid(mo[:, hs]) * (hn * mg[:, hs])

        o_diag = [None] * G_HEADS

        def gla_diag():
            gk3 = gk.reshape(nsub, SUB, gw)
            bc3 = bc2.reshape(nsub, SUB, gw)
            half = SUB // 2
            hi = lambda a: a.reshape(nsub, 2, half, a.shape[-1])[:, 1].reshape(nsub * half, a.shape[-1])
            gq_hi, bc2_hi = hi(gq), hi(bc2)
            p_acc = jnp.zeros((tt, LANE), _F32)
            p_hi = jnp.zeros((tt // 2, LANE), _F32)
            for s in range(SUB):
                if s < half:
                    kb = jnp.broadcast_to(gk3[:, s:s + 1, :], (nsub, SUB, gw)).reshape(tt, gw)
                    bb = jnp.broadcast_to(bc3[:, s:s + 1, :], (nsub, SUB, gw)).reshape(tt, gw)
                    y = gq * kb * jnp.exp2(jnp.minimum(bc2 - bb, 0.0))
                    p_acc = p_acc + _dot(y, gsel[s])
                else:
                    kb = jnp.broadcast_to(gk3[:, s:s + 1, :], (nsub, half, gw)).reshape(tt // 2, gw)
                    bb = jnp.broadcast_to(bc3[:, s:s + 1, :], (nsub, half, gw)).reshape(tt // 2, gw)
                    y = gq_hi * kb * jnp.exp2(jnp.minimum(bc2_hi - bb, 0.0))
                    p_hi = p_hi + _dot(y, gsel[s])
                yield
            p4 = p_acc.reshape(nsub, 2, half, LANE)
            p_acc = jnp.concatenate([p4[:, 0:1], p4[:, 1:2] + p_hi.reshape(nsub, 1, half, LANE)],
                                    axis=1).reshape(tt, LANE)
            p_bf = p_acc.astype(_BF)
            r2 = lax.broadcasted_iota(jnp.int32, (WIN, WIN), 0)
            c2 = lax.broadcasted_iota(jnp.int32, (WIN, WIN), 1)
            sub_mask = (c2 <= r2) & ((c2 // SUB) == (r2 // SUB))
            for h in range(G_HEADS):
                outs = []
                for w in range(tt // WIN):
                    rs = slice(w * WIN, (w + 1) * WIN)
                    a_full = jnp.dot(p_bf[rs], rsel[h], preferred_element_type=_F32)
                    a_h = jnp.where(sub_mask, a_full, 0.0)
                    outs.append(_dot(a_h, gv[rs, h * G_DV:(h + 1) * G_DV]))
                o_diag[h] = outs[0] if len(outs) == 1 else jnp.concatenate(outs, axis=0)
                yield

        head_of_lane = lax.broadcasted_iota(jnp.int32, (SUB, gw), 1) // G_DK
        o_int = [[None] * nsub for _ in range(G_HEADS)]

        def gla_rec():
            st = s_st[...]
            for j in range(nsub):
                js = slice(j * SUB, (j + 1) * SUB)
                qj, kj = qh[js], kh[js]
                qq = jnp.concatenate([jnp.where(head_of_lane == h, qj, 0.0) for h in range(G_HEADS)], axis=0)
                kk = jnp.concatenate([jnp.where(head_of_lane == h, kj, 0.0) for h in range(G_HEADS)], axis=0)
                vv = jnp.concatenate([gv[js, h * G_DV:(h + 1) * G_DV] for h in range(G_HEADS)], axis=0)
                oj = _dot_nt(qq, st)
                for h in range(G_HEADS):
                    o_int[h][j] = oj[h * SUB:(h + 1) * SUB]
                st = eg[j * SUB:j * SUB + 1, :] * st + _dot_tn(vv, kk)
                yield
            s_st[...] = st

        def cast_weights():
            for src_ref, dst_ref in zip(cast_in, cast_out):
                dst_ref[...] = src_ref[0].astype(_BF)
                yield

        _round_robin([in_proj(), gla_diag(), gla_rec()] + [mlstm_head(h) for h in range(M_HEADS)]
                     + [cast_weights()])

        gn = g_norm_g[...]
        g_outs = []
        for h in range(G_HEADS):
            hs = slice(h * G_DV, (h + 1) * G_DV)
            og = o_diag[h] + jnp.concatenate(o_int[h], axis=0)
            rms = lax.rsqrt(jnp.mean(og * og, axis=1, keepdims=True) + LN_EPS)
            gate = gg[:, hs]
            g_outs.append(gate * _sigmoid(gate) * (og * rms * gn[:, hs]))

        y = jnp.concatenate(m_outs + g_outs, axis=1)
        mix = _dot(y, w_out[...])
        o_ref[0] = _layer_norm(ALPHA * h0 + mix, ln1_g[...], ln1_b[...])

    for par in range(2):
        pl.when(g % 2 == par)(functools.partial(step, par))


def _mixer_constants(tt):
    r = np.arange(tt)
    tri64 = ((r[None, :] <= r[:, None]) & (r[None, :] // CHUNK == r[:, None] // CHUNK))
    same16 = (r[None, :] // SUB == r[:, None] // SUB)
    tri16 = (r[None, :] <= r[:, None]) & same16
    gw = G_HEADS * G_DK
    gsel = np.zeros((SUB, gw, LANE), np.float32)
    for s in range(SUB):
        for h in range(G_HEADS):
            gsel[s, h * G_DK:(h + 1) * G_DK, h * SUB + s] = 1.0
    rsel = np.zeros((G_HEADS, LANE, WIN), np.float32)
    for h in range(G_HEADS):
        for s in range(SUB):
            rsel[h, h * SUB + s, s::SUB] = 1.0
    to_bf = lambda a: jnp.asarray(a.astype(np.float32), dtype=_BF)
    return to_bf(tri64), to_bf(tri16), to_bf(same16), to_bf(gsel), to_bf(rsel)


def _full(shape):
    nd = len(shape)
    return pl.BlockSpec(shape, lambda b, i, _nd=nd: (0,) * _nd)


def _mixer_call(x, lnin_g, lnin_b, w_m, w_g, w_gate, conv_w, conv_b, gate_bias, m_norm_g, glw, glb, g_norm_g,
                w_out, ln1_g, ln1_b, tt, to_cast):
    bsz, t, d = x.shape
    consts = _mixer_constants(tt)
    params = (lnin_g, lnin_b, w_m, w_g, w_gate, conv_w, conv_b, gate_bias, m_norm_g, glw, glb, g_norm_g,
              w_out, ln1_g, ln1_b) + consts
    nt = t // tt
    ntiles = bsz * nt
    in_tile = pl.BlockSpec((1, tt, d), lambda g: (jnp.minimum(g, ntiles - 1) // nt, jnp.minimum(g, ntiles - 1) % nt, 0))
    out_tile = pl.BlockSpec((1, tt, d), lambda g: (jnp.maximum(g - 1, 0) // nt, jnp.maximum(g - 1, 0) % nt, 0))
    full = lambda shape: pl.BlockSpec(shape, lambda g, _nd=len(shape): (0,) * _nd)
    cast_in = [pl.BlockSpec((1, w.shape[1] // ntiles, w.shape[2]), lambda g: (0, jnp.minimum(g, ntiles - 1), 0))
               for w in to_cast]
    cast_out = [pl.BlockSpec((w.shape[1] // ntiles, w.shape[2]), lambda g: (jnp.minimum(g, ntiles - 1), 0))
                for w in to_cast]
    assert all(w.shape[1] % (16 * ntiles) == 0 for w in to_cast)
    return pl.pallas_call(
        functools.partial(_mixer_kernel, tt=tt, nt=nt, n_cast=len(to_cast)),
        out_shape=[jax.ShapeDtypeStruct((bsz, t, d), _F32)]
        + [jax.ShapeDtypeStruct(w.shape[1:], _BF) for w in to_cast],
        grid=(ntiles + 1,),
        in_specs=[in_tile] + [full(p.shape) for p in params] + cast_in,
        out_specs=[out_tile] + cast_out,
        scratch_shapes=[
            pltpu.VMEM((tt + 8, 2 * M_QK), _F32),
            pltpu.VMEM((M_HEADS, M_DK, M_DV), _F32),
            pltpu.VMEM((M_HEADS, 1, M_DK), _F32),
            pltpu.VMEM((M_HEADS, 1, 1), _F32),
            pltpu.VMEM((G_DV, G_HEADS * G_DK), _F32),
            pltpu.VMEM((2, tt, IN_COLS_R), _F32),
            pltpu.VMEM((2, tt, D_MODEL), _F32),
        ],
        compiler_params=pltpu.CompilerParams(
            dimension_semantics=("arbitrary",), vmem_limit_bytes=VMEM_LIMIT),
        name="mixer",
    )(x, *params, *to_cast)


def _memkv_kernel(mem_ref, wk_ref, wv_ref, wq_ref, wo_ref, wqk_ref, vo_ref):
    m = mem_ref[0]
    k = _dot(m, wk_ref[...])
    v = _dot(m, wv_ref[...])
    for h in range(X_HEADS):
        hs = slice(h * X_DH, (h + 1) * X_DH)
        ms = slice(h * N_MEM, (h + 1) * N_MEM)
        wqk_ref[0, :, ms] = (_dot_nt(wq_ref[:, hs], k[:, hs]) * (X_DH ** -0.5)).astype(_BF)
        vo_ref[0, ms, :] = _dot(v[:, hs], wo_ref[hs, :]).astype(_BF)


def _memkv_call(mem, wk, wv, wq, wo):
    bsz, nm, d = mem.shape
    blk = pl.BlockSpec((1, nm, d), lambda b: (b, 0, 0))
    wspec = pl.BlockSpec((d, d), lambda b: (0, 0))
    return pl.pallas_call(
        _memkv_kernel,
        out_shape=(jax.ShapeDtypeStruct((bsz, d, X_HEADS * nm), _BF),
                   jax.ShapeDtypeStruct((bsz, X_HEADS * nm, d), _BF)),
        grid=(bsz,),
        in_specs=[blk, wspec, wspec, wspec, wspec],
        out_specs=(pl.BlockSpec((1, d, X_HEADS * nm), lambda b: (b, 0, 0)),
                   pl.BlockSpec((1, X_HEADS * nm, d), lambda b: (b, 0, 0))),
        compiler_params=pltpu.CompilerParams(
            dimension_semantics=("arbitrary",), vmem_limit_bytes=VMEM_LIMIT),
        name="memkv",
    )(mem, wk, wv, wq, wo)


def _attnmlp_kernel(h_ref, wqk_ref, vo_ref, ln2_g, ln2_b, w1, w2, ln3_g, ln3_b, o_ref):
    rows = h_ref.shape[1] // ROW_SPLIT

    def chain(r, wait):
        for _ in range(wait):
            yield
        rs = slice(r * rows, (r + 1) * rows)
        h1 = h_ref[0, rs, :]
        s_all = _dot(h1, wqk_ref[0])
        yield
        probs = []
        for h in range(X_HEADS):
            s = s_all[:, h * N_MEM:(h + 1) * N_MEM]
            e = jnp.exp(s - jnp.max(s, axis=1, keepdims=True))
            probs.append((e * (1.0 / jnp.sum(e, axis=1, keepdims=True))).astype(_BF))
            yield
        xa = jnp.dot(jnp.concatenate(probs, axis=1), vo_ref[0], preferred_element_type=_F32)
        yield
        h2 = _layer_norm(ALPHA * h1 + xa, ln2_g[...], ln2_b[...])
        h2b = h2.astype(_BF)
        yield
        ff = jnp.zeros(h2.shape, _F32)
        for f in range(D_FF // FF_BLK):
            fs = slice(f * FF_BLK, (f + 1) * FF_BLK)
            hid = jnp.maximum(jnp.dot(h2b, w1[:, fs], preferred_element_type=_F32), 0.0)
            ff = ff + _dot(hid * hid, w2[fs, :])
            yield
        o_ref[0, rs, :] = _layer_norm(ALPHA * h2 + ff, ln3_g[...], ln3_b[...])

    _round_robin([chain(r, r * ROW_DELAY) for r in range(ROW_SPLIT)])


def _attnmlp_call(h1, wqk, vo, ln2_g, ln2_b, w1, w2, ln3_g, ln3_b, tm):
    bsz, t, d = h1.shape
    tile = pl.BlockSpec((1, tm, d), lambda b, i: (b, i, 0))
    per_batch = lambda a: pl.BlockSpec((1,) + a.shape[1:], lambda b, i: (b, 0, 0))
    params = (ln2_g, ln2_b, w1, w2, ln3_g, ln3_b)
    return pl.pallas_call(
        _attnmlp_kernel,
        out_shape=jax.ShapeDtypeStruct((bsz, t, d), _F32),
        grid=(bsz, t // tm),
        in_specs=[tile, per_batch(wqk), per_batch(vo)] + [_full(p.shape) for p in params],
        out_specs=tile,
        compiler_params=pltpu.CompilerParams(
            dimension_semantics=("arbitrary", "arbitrary"), vmem_limit_bytes=VMEM_LIMIT),
        name="attnmlp",
    )(h1, wqk, vo, *params)


def _time_tile(t, target):
    tt = min(t, target)
    assert t % tt == 0 and tt % WIN == 0, (t, tt)
    return tt


def kernel(x, mem, ln_in_g, ln_in_b, w_in, conv_w, conv_b, m_i_bias, m_f_bias, m_norm_g, g_lr_w, g_lr_b, g_norm_g, w_out, ln1_g, ln1_b, x_wq, x_wk, x_wv, x_wo, ln2_g, ln2_b, w_ff1, w_ff2, ln3_g, ln3_b):
    assert w_in.shape[0] == DEPTH == 1
    row = lambda a: a.reshape(1, -1).astype(_F32)
    wi = w_in[0]
    wt = lax.optimization_barrier(jnp.swapaxes(wi, 0, 1))
    w_m = jnp.swapaxes(wt[O_MQ:O_MI], 0, 1).astype(_BF)
    w_g = jnp.swapaxes(wt[O_GQ:O_GLR], 0, 1).astype(_BF)
    w_gate = jnp.concatenate([wi[:, O_MI:O_GQ], wi[:, O_GLR:O_GLR + G_RANK],
                              jnp.zeros((D_MODEL, LANE - 2 * M_HEADS - G_RANK), wi.dtype)], axis=1).astype(_BF)
    gate_bias = jnp.concatenate([m_i_bias[0], m_f_bias[0],
                                 jnp.zeros((LANE - 2 * M_HEADS,), _F32)]).reshape(1, LANE)
    glw = jnp.zeros((LANE, G_HEADS * G_DK), _F32).at[2 * M_HEADS:2 * M_HEADS + G_RANK].set(g_lr_w[0]).astype(_BF)
    tt = _time_tile(x.shape[1], MIXER_ROWS)
    h1, wk_b, wv_b, wq_b, wo_b, w1_b, w2_b = _mixer_call(
        x, row(ln_in_g), row(ln_in_b), w_m, w_g, w_gate, conv_w[0].astype(_F32), row(conv_b[0]),
        gate_bias, row(m_norm_g[0]), glw, row(g_lr_b[0]), row(g_norm_g[0]), w_out[0].astype(_BF),
        row(ln1_g[0]), row(ln1_b[0]), tt, (x_wk, x_wv, x_wq, x_wo, w_ff1, w_ff2))
    wqk, vo = _memkv_call(mem, wk_b, wv_b, wq_b, wo_b)
    tm = _time_tile(x.shape[1], ATTN_ROWS)
    return _attnmlp_call(h1, wqk, vo, row(ln2_g[0]), row(ln2_b[0]), w1_b, w2_b, row(ln3_g[0]), row(ln3_b[0]), tm)
```

```python
import functools

import numpy as np
import jax
import jax.numpy as jnp
from jax import lax
from jax.experimental import pallas as pl
from jax.experimental.pallas import tpu as pltpu

D_MODEL = 1024
CHUNK = 64
SUB = 16
WIN = 2 * CHUNK
N_MEM = 256
M_HEADS, M_DK, M_DV = 4, 128, 128
M_QK = M_HEADS * M_DK
G_HEADS, G_DK, G_DV = 4, 64, 128
G_RANK = 16
G_TAU = 16.0
X_HEADS = 4
X_DH = D_MODEL // X_HEADS
D_FF = 4 * D_MODEL
MIXER_ROWS = 256
ATTN_ROWS = 1024
FF_BLK = 512
IN_BLK = 512
ROW_SPLIT = 2
ROW_DELAY = 6
DEPTH = 1
ALPHA = (2.0 * DEPTH) ** 0.25
LN_EPS = 1e-5
LOG2E = 1.4426950408889634
LANE = 128

C_MQ, C_MK, C_MV, C_MO = 0, 512, 1024, 1536
C_GQ, C_GK, C_GV, C_GG = 2048, 2304, 2560, 3072
C_GATE = 3584
IN_COLS_R = C_GATE + LANE
O_MQ, O_MK, O_MV, O_MO, O_MI, O_MF = 0, 512, 1024, 1536, 2048, 2052
O_GQ, O_GK, O_GV, O_GG, O_GLR = 2056, 2312, 2568, 3080, 3592

VMEM_LIMIT = 56 * 1024 * 1024

_BF = jnp.bfloat16
_F32 = jnp.float32


def _dot(a, b):
    return jnp.dot(a.astype(_BF), b.astype(_BF), preferred_element_type=_F32)


def _dot_nt(a, b):
    return lax.dot_general(a.astype(_BF), b.astype(_BF), (((1,), (1,)), ((), ())),
                           preferred_element_type=_F32)


def _dot_tn(a, b):
    return lax.dot_general(a.astype(_BF), b.astype(_BF), (((0,), (0,)), ((), ())),
                           preferred_element_type=_F32)


def _split2(x):
    hi = x.astype(_BF)
    lo = (x - hi.astype(_F32)).astype(_BF)
    return jnp.concatenate([hi, lo], axis=1)


def _merge2(y, n):
    return y[:, 0:n] + y[:, n:2 * n]


def _layer_norm(x, g, b):
    mu = jnp.mean(x, axis=-1, keepdims=True)
    xc = x - mu
    var = jnp.mean(xc * xc, axis=-1, keepdims=True)
    return xc * lax.rsqrt(var + LN_EPS) * g + b


def _log_sigmoid(x):
    return -(jnp.maximum(-x, 0.0) + jnp.log(1.0 + jnp.exp(-jnp.abs(x))))


def _sigmoid(x):
    return 1.0 / (1.0 + jnp.exp(-x))


def _round_robin(chains):
    while chains:
        alive = []
        for ch in chains:
            try:
                next(ch)
                alive.append(ch)
            except StopIteration:
                pass
        chains = alive


def _mixer_kernel(x_ref, lnin_g, lnin_b, w_m, w_g, w_gate, conv_w, conv_b, gate_bias, m_norm_g, glw, glb, g_norm_g,
                  w_out, ln1_g, ln1_b, tri64, tri16, ones16, gsel, rsel, *rest, tt, nt, n_cast):
    cast_in, rest = rest[:n_cast], rest[n_cast:]
    o_ref, cast_out = rest[0], rest[1:1 + n_cast]
    uext, eext, c_st, n_st, m_st, s_st, proj_buf, h0_buf = rest[1 + n_cast:]
    g = pl.program_id(0)

    @pl.when(g == 0)
    def _():
        proj_buf[1] = jnp.zeros(proj_buf.shape[1:], _F32)
        h0_buf[1] = jnp.zeros(h0_buf.shape[1:], _F32)

    @pl.when((g == 0) | ((g - 1) % nt == 0))
    def _():
        uext[0:8, :] = jnp.zeros((8, 2 * M_QK), _F32)
        eext[0:8, :] = jnp.zeros((8, 2 * M_QK), _F32)
        c_st[...] = jnp.zeros(c_st.shape, _F32)
        n_st[...] = jnp.zeros(n_st.shape, _F32)
        m_st[...] = jnp.zeros(m_st.shape, _F32)
        s_st[...] = jnp.zeros(s_st.shape, _F32)

    def step(slot_a):
        slot_b = 1 - slot_a

        def in_proj():
            h0n = _layer_norm(x_ref[0], lnin_g[...], lnin_b[...])
            h0_buf[slot_a] = h0n
            h0b = h0n.astype(_BF)
            yield
            for w_ref, base in ((w_m, C_MQ), (w_g, C_GQ), (w_gate, C_GATE)):
                ncol = w_ref.shape[1]
                for c0 in range(0, ncol, IN_BLK):
                    c1 = min(c0 + IN_BLK, ncol)
                    proj_buf[slot_a, :, base + c0:base + c1] = jnp.dot(h0b, w_ref[:, c0:c1],
                                                                       preferred_element_type=_F32)
                    yield

        pb = proj_buf.at[slot_b]
        h0 = h0_buf[slot_b]

        uext[8:8 + tt, :] = pb[:, C_MQ:C_MV]
        cw = conv_w[...]
        u0 = uext[8:8 + tt, :]
        u1 = uext[7:7 + tt, :]
        eext[8:8 + tt, :] = u0 * cw[1:2, :] + u1 * cw[0:1, :]
        conv = u0 * cw[3:4, :] + u1 * cw[2:3, :] + eext[6:6 + tt, :] + conv_b[...]
        uext[0:8, :] = uext[tt:tt + 8, :]
        eext[0:8, :] = eext[tt:tt + 8, :]
        qk = conv * _sigmoid(conv)
        mq = qk[:, 0:M_QK]
        mk = qk[:, M_QK:2 * M_QK] * (M_DK ** -0.5)
        mv = pb[:, C_MV:C_MO]
        mo = pb[:, C_MO:C_GQ]

        gb = pb[:, C_GATE:C_GATE + LANE] + gate_bias[...]
        lane = lax.broadcasted_iota(jnp.int32, (tt, LANE), 1)
        lf = _log_sigmoid(gb)
        bcum = _merge2(_dot(tri64[...], _split2(lf)), LANE)
        vcol = jnp.where((lane >= M_HEADS) & (lane < 2 * M_HEADS), bcum, gb) * LOG2E
        vrow = vcol.T

        gw = G_HEADS * G_DK
        la = _log_sigmoid(_dot(gb, glw[...]) + glb[...]) * (1.0 / G_TAU)
        la2 = _split2(la)
        bc = _merge2(_dot(tri16[...], la2), gw)
        bend = _merge2(_dot(ones16[...], la2), gw)
        gq = pb[:, C_GQ:C_GK] * (G_DK ** -0.5)
        gk = pb[:, C_GK:C_GV]
        gv = pb[:, C_GV:C_GG]
        gg = pb[:, C_GG:C_GATE]
        bc2 = bc * LOG2E
        bend2 = bend * LOG2E
        qh = gq * jnp.exp2(bc2)
        kh = gk * jnp.exp2(bend2 - bc2)
        eg = jnp.exp2(bend2)
        nsub = tt // SUB

        rr = lax.broadcasted_iota(jnp.int32, (WIN, WIN), 0)
        cc = lax.broadcasted_iota(jnp.int32, (WIN, WIN), 1)
        valid = (cc <= rr) & ((cc >= CHUNK) == (rr >= CHUNK))
        first = lax.broadcasted_iota(jnp.int32, (WIN, 1), 0) < CHUNK
        mg = m_norm_g[...]
        m_outs = [None] * M_HEADS

        def mlstm_head(h):
            hs = slice(h * M_DK, (h + 1) * M_DK)
            c_cur = c_st[h]
            n_cur = n_st[h]
            m_cur = m_st[h]
            win_outs = []
            for w in range(tt // WIN):
                rs = slice(w * WIN, (w + 1) * WIN)
                q, k, v = mq[rs, hs], mk[rs, hs], mv[rs, hs]
                li_col, b_col = vcol[rs, h:h + 1], vcol[rs, M_HEADS + h:M_HEADS + h + 1]
                li_row, b_row = vrow[h:h + 1, rs], vrow[M_HEADS + h:M_HEADS + h + 1, rs]
                dm = jnp.where(valid, b_col - b_row + li_row, -jnp.inf)
                md = jnp.max(dm, axis=1, keepdims=True)
                g0, g1 = b_col[CHUNK - 1:CHUNK, :], b_col[WIN - 1:WIN, :]
                a_col = jnp.where(first, g0, g1) - b_col + li_col
                yield
                am0 = jnp.max(a_col[0:CHUNK], axis=0, keepdims=True)
                am1 = jnp.max(a_col[CHUNK:WIN], axis=0, keepdims=True)
                wk = jnp.exp2(a_col - jnp.where(first, am0, am1)) * k
                s_raw = _dot_nt(q, k)
                yield
                cprev, nprev, mprev = [], [], []
                for c, (gc, am) in enumerate(((g0, am0), (g1, am1))):
                    cs = slice(c * CHUNK, (c + 1) * CHUNK)
                    cprev.append(c_cur)
                    nprev.append(n_cur)
                    mprev.append(m_cur)
                    u_mat = _dot_tn(wk[cs], v[cs])
                    u_vec = jnp.sum(wk[cs], axis=0, keepdims=True)
                    m_new = jnp.maximum(gc + m_cur, am)
                    dec = jnp.exp2(gc + m_cur - m_new)
                    inj = jnp.exp2(am - m_new)
                    c_cur = dec * c_cur + inj * u_mat
                    n_cur = dec * n_cur + inj * u_vec
                    m_cur = m_new
                yield
                inter = b_col + jnp.where(first, mprev[0], mprev[1])
                m_t = jnp.maximum(inter, md)
                sc = jnp.exp2(inter - m_t)
                s_mat = s_raw * jnp.exp2(dm - m_t)
                yield
                sv = _dot(s_mat, v)
                qc = jnp.concatenate([_dot(q[0:CHUNK], cprev[0]), _dot(q[CHUNK:WIN], cprev[1])], axis=0)
                qn = jnp.sum(q * jnp.where(first, nprev[0], nprev[1]), axis=1, keepdims=True)
                ssum = jnp.sum(s_mat, axis=1, keepdims=True)
                yield
                num = sc * qc + sv
                den = sc * qn + ssum
                hh = num * (1.0 / jnp.maximum(jnp.abs(den), jnp.exp2(-m_t)))
                mu = jnp.mean(hh, axis=1, keepdims=True)
                yield
                hc = hh - mu
                var = jnp.mean(hc * hc, axis=1, keepdims=True)
                yield
                win_outs.append(hc * lax.rsqrt(var + LN_EPS))
            c_st[h] = c_cur
            n_st[h] = n_cur
            m_st[h] = m_cur
            hn = win_outs[0] if len(win_outs) == 1 else jnp.concatenate(win_outs, axis=0)
            m_outs[h] = _sigmoid(mo[:, hs]) * (hn * mg[:, hs])

        o_diag = [None] * G_HEADS

        def gla_diag():
            gk3 = gk.reshape(nsub, SUB, gw)
            bc3 = bc2.reshape(nsub, SUB, gw)
            half = SUB // 2
            hi = lambda a: a.reshape(nsub, 2, half, a.shape[-1])[:, 1].reshape(nsub * half, a.shape[-1])
            gq_hi, bc2_hi = hi(gq), hi(bc2)
            p_acc = jnp.zeros((tt, LANE), _F32)
            p_hi = jnp.zeros((tt // 2, LANE), _F32)
            for s in range(SUB):
                if s < half:
                    kb = jnp.broadcast_to(gk3[:, s:s + 1, :], (nsub, SUB, gw)).reshape(tt, gw)
                    bb = jnp.broadcast_to(bc3[:, s:s + 1, :], (nsub, SUB, gw)).reshape(tt, gw)
                    y = gq * kb * jnp.exp2(jnp.minimum(bc2 - bb, 0.0))
                    p_acc = p_acc + _dot(y, gsel[s])
                else:
                    kb = jnp.broadcast_to(gk3[:, s:s + 1, :], (nsub, half, gw)).reshape(tt // 2, gw)
                    bb = jnp.broadcast_to(bc3[:, s:s + 1, :], (nsub, half, gw)).reshape(tt // 2, gw)
                    y = gq_hi * kb * jnp.exp2(jnp.minimum(bc2_hi - bb, 0.0))
                    p_hi = p_hi + _dot(y, gsel[s])
                yield
            p4 = p_acc.reshape(nsub, 2, half, LANE)
            p_acc = jnp.concatenate([p4[:, 0:1], p4[:, 1:2] + p_hi.reshape(nsub, 1, half, LANE)],
                                    axis=1).reshape(tt, LANE)
            p_bf = p_acc.astype(_BF)
            r2 = lax.broadcasted_iota(jnp.int32, (WIN, WIN), 0)
            c2 = lax.broadcasted_iota(jnp.int32, (WIN, WIN), 1)
            sub_mask = (c2 <= r2) & ((c2 // SUB) == (r2 // SUB))
            for h in range(G_HEADS):
                outs = []
                for w in range(tt // WIN):
                    rs = slice(w * WIN, (w + 1) * WIN)
                    a_full = jnp.dot(p_bf[rs], rsel[h], preferred_element_type=_F32)
                    a_h = jnp.where(sub_mask, a_full, 0.0)
                    outs.append(_dot(a_h, gv[rs, h * G_DV:(h + 1) * G_DV]))
                o_diag[h] = outs[0] if len(outs) == 1 else jnp.concatenate(outs, axis=0)
                yield

        head_of_lane = lax.broadcasted_iota(jnp.int32, (SUB, gw), 1) // G_DK
        o_int = [[None] * nsub for _ in range(G_HEADS)]

        def gla_rec():
            st = s_st[...]
            for j in range(nsub):
                js = slice(j * SUB, (j + 1) * SUB)
                qj, kj = qh[js], kh[js]
                qq = jnp.concatenate([jnp.where(head_of_lane == h, qj, 0.0) for h in range(G_HEADS)], axis=0)
                kk = jnp.concatenate([jnp.where(head_of_lane == h, kj, 0.0) for h in range(G_HEADS)], axis=0)
                vv = jnp.concatenate([gv[js, h * G_DV:(h + 1) * G_DV] for h in range(G_HEADS)], axis=0)
                oj = _dot_nt(qq, st)
                for h in range(G_HEADS):
                    o_int[h][j] = oj[h * SUB:(h + 1) * SUB]
                st = eg[j * SUB:j * SUB + 1, :] * st + _dot_tn(vv, kk)
                yield
            s_st[...] = st

        def cast_weights():
            for src_ref, dst_ref in zip(cast_in, cast_out):
                dst_ref[...] = src_ref[0].astype(_BF)
                yield

        _round_robin([in_proj(), gla_diag(), gla_rec()] + [mlstm_head(h) for h in range(M_HEADS)]
                     + [cast_weights()])

        gn = g_norm_g[...]
        g_outs = []
        for h in range(G_HEADS):
            hs = slice(h * G_DV, (h + 1) * G_DV)
            og = o_diag[h] + jnp.concatenate(o_int[h], axis=0)
            rms = lax.rsqrt(jnp.mean(og * og, axis=1, keepdims=True) + LN_EPS)
            gate = gg[:, hs]
            g_outs.append(gate * _sigmoid(gate) * (og * rms * gn[:, hs]))

        y = jnp.concatenate(m_outs + g_outs, axis=1)
        mix = _dot(y, w_out[...])
        o_ref[0] = _layer_norm(ALPHA * h0 + mix, ln1_g[...], ln1_b[...])

    for par in range(2):
        pl.when(g % 2 == par)(functools.partial(step, par))


def _mixer_constants(tt):
    r = np.arange(tt)
    tri64 = ((r[None, :] <= r[:, None]) & (r[None, :] // CHUNK == r[:, None] // CHUNK))
    same16 = (r[None, :] // SUB == r[:, None] // SUB)
    tri16 = (r[None, :] <= r[:, None]) & same16
    gw = G_HEADS * G_DK
    gsel = np.zeros((SUB, gw, LANE), np.float32)
    for s in range(SUB):
        for h in range(G_HEADS):
            gsel[s, h * G_DK:(h + 1) * G_DK, h * SUB + s] = 1.0
    rsel = np.zeros((G_HEADS, LANE, WIN), np.float32)
    for h in range(G_HEADS):
        for s in range(SUB):
            rsel[h, h * SUB + s, s::SUB] = 1.0
    to_bf = lambda a: jnp.asarray(a.astype(np.float32), dtype=_BF)
    return to_bf(tri64), to_bf(tri16), to_bf(same16), to_bf(gsel), to_bf(rsel)


def _full(shape):
    nd = len(shape)
    return pl.BlockSpec(shape, lambda b, i, _nd=nd: (0,) * _nd)


def _mixer_call(x, lnin_g, lnin_b, w_m, w_g, w_gate, conv_w, conv_b, gate_bias, m_norm_g, glw, glb, g_norm_g,
                w_out, ln1_g, ln1_b, tt, to_cast):
    bsz, t, d = x.shape
    consts = _mixer_constants(tt)
    params = (lnin_g, lnin_b, w_m, w_g, w_gate, conv_w, conv_b, gate_bias, m_norm_g, glw, glb, g_norm_g,
              w_out, ln1_g, ln1_b) + consts
    nt = t // tt
    ntiles = bsz * nt
    in_tile = pl.BlockSpec((1, tt, d), lambda g: (jnp.minimum(g, ntiles - 1) // nt, jnp.minimum(g, ntiles - 1) % nt, 0))
    out_tile = pl.BlockSpec((1, tt, d), lambda g: (jnp.maximum(g - 1, 0) // nt, jnp.maximum(g - 1, 0) % nt, 0))
    full = lambda shape: pl.BlockSpec(shape, lambda g, _nd=len(shape): (0,) * _nd)
    cast_in = [pl.BlockSpec((1, w.shape[1] // ntiles, w.shape[2]), lambda g: (0, jnp.minimum(g, ntiles - 1), 0))
               for w in to_cast]
    cast_out = [pl.BlockSpec((w.shape[1] // ntiles, w.shape[2]), lambda g: (jnp.minimum(g, ntiles - 1), 0))
                for w in to_cast]
    assert all(w.shape[1] % (16 * ntiles) == 0 for w in to_cast)
    return pl.pallas_call(
        functools.partial(_mixer_kernel, tt=tt, nt=nt, n_cast=len(to_cast)),
        out_shape=[jax.ShapeDtypeStruct((bsz, t, d), _F32)]
        + [jax.ShapeDtypeStruct(w.shape[1:], _BF) for w in to_cast],
        grid=(ntiles + 1,),
        in_specs=[in_tile] + [full(p.shape) for p in params] + cast_in,
        out_specs=[out_tile] + cast_out,
        scratch_shapes=[
            pltpu.VMEM((tt + 8, 2 * M_QK), _F32),
            pltpu.VMEM((tt + 8, 2 * M_QK), _F32),
            pltpu.VMEM((M_HEADS, M_DK, M_DV), _F32),
            pltpu.VMEM((M_HEADS, 1, M_DK), _F32),
            pltpu.VMEM((M_HEADS, 1, 1), _F32),
            pltpu.VMEM((G_DV, G_HEADS * G_DK), _F32),
            pltpu.VMEM((2, tt, IN_COLS_R), _F32),
            pltpu.VMEM((2, tt, D_MODEL), _F32),
        ],
        compiler_params=pltpu.CompilerParams(
            dimension_semantics=("arbitrary",), vmem_limit_bytes=VMEM_LIMIT),
        name="mixer",
    )(x, *params, *to_cast)


def _memkv_kernel(mem_ref, wk_ref, wv_ref, wq_ref, wo_ref, wqk_ref, vo_ref):
    m = mem_ref[0]
    k = _dot(m, wk_ref[...])
    v = _dot(m, wv_ref[...])
    for h in range(X_HEADS):
        hs = slice(h * X_DH, (h + 1) * X_DH)
        ms = slice(h * N_MEM, (h + 1) * N_MEM)
        wqk_ref[0, :, ms] = (_dot_nt(wq_ref[:, hs], k[:, hs]) * (X_DH ** -0.5)).astype(_BF)
        vo_ref[0, ms, :] = _dot(v[:, hs], wo_ref[hs, :]).astype(_BF)


def _memkv_call(mem, wk, wv, wq, wo):
    bsz, nm, d = mem.shape
    blk = pl.BlockSpec((1, nm, d), lambda b: (b, 0, 0))
    wspec = pl.BlockSpec((d, d), lambda b: (0, 0))
    return pl.pallas_call(
        _memkv_kernel,
        out_shape=(jax.ShapeDtypeStruct((bsz, d, X_HEADS * nm), _BF),
                   jax.ShapeDtypeStruct((bsz, X_HEADS * nm, d), _BF)),
        grid=(bsz,),
        in_specs=[blk, wspec, wspec, wspec, wspec],
        out_specs=(pl.BlockSpec((1, d, X_HEADS * nm), lambda b: (b, 0, 0)),
                   pl.BlockSpec((1, X_HEADS * nm, d), lambda b: (b, 0, 0))),
        compiler_params=pltpu.CompilerParams(
            dimension_semantics=("arbitrary",), vmem_limit_bytes=VMEM_LIMIT),
        name="memkv",
    )(mem, wk, wv, wq, wo)


def _attnmlp_kernel(h_ref, wqk_ref, vo_ref, ln2_g, ln2_b, w1, w2, ln3_g, ln3_b, o_ref):
    rows = h_ref.shape[1] // ROW_SPLIT

    def chain(r, wait):
        for _ in range(wait):
            yield
        rs = slice(r * rows, (r + 1) * rows)
        h1 = h_ref[0, rs, :]
        s_all = _dot(h1, wqk_ref[0])
        yield
        probs = []
        for h in range(X_HEADS):
            s = s_all[:, h * N_MEM:(h + 1) * N_MEM]
            e = jnp.exp(s - jnp.max(s, axis=1, keepdims=True))
            probs.append((e * (1.0 / jnp.sum(e, axis=1, keepdims=True))).astype(_BF))
            yield
        xa = jnp.dot(jnp.concatenate(probs, axis=1), vo_ref[0], preferred_element_type=_F32)
        yield
        h2 = _layer_norm(ALPHA * h1 + xa, ln2_g[...], ln2_b[...])
        h2b = h2.astype(_BF)
        yield
        ff = jnp.zeros(h2.shape, _F32)
        for f in range(D_FF // FF_BLK):
            fs = slice(f * FF_BLK, (f + 1) * FF_BLK)
            hid = jnp.maximum(jnp.dot(h2b, w1[:, fs], preferred_element_type=_F32), 0.0)
            ff = ff + _dot(hid * hid, w2[fs, :])
            yield
        o_ref[0, rs, :] = _layer_norm(ALPHA * h2 + ff, ln3_g[...], ln3_b[...])

    _round_robin([chain(r, r * ROW_DELAY) for r in range(ROW_SPLIT)])


def _attnmlp_call(h1, wqk, vo, ln2_g, ln2_b, w1, w2, ln3_g, ln3_b, tm):
    bsz, t, d = h1.shape
    tile = pl.BlockSpec((1, tm, d), lambda b, i: (b, i, 0))
    per_batch = lambda a: pl.BlockSpec((1,) + a.shape[1:], lambda b, i: (b, 0, 0))
    params = (ln2_g, ln2_b, w1, w2, ln3_g, ln3_b)
    return pl.pallas_call(
        _attnmlp_kernel,
        out_shape=jax.ShapeDtypeStruct((bsz, t, d), _F32),
        grid=(bsz, t // tm),
        in_specs=[tile, per_batch(wqk), per_batch(vo)] + [_full(p.shape) for p in params],
        out_specs=tile,
        compiler_params=pltpu.CompilerParams(
            dimension_semantics=("arbitrary", "arbitrary"), vmem_limit_bytes=VMEM_LIMIT),
        name="attnmlp",
    )(h1, wqk, vo, *params)


def _time_tile(t, target):
    tt = min(t, target)
    assert t % tt == 0 and tt % WIN == 0, (t, tt)
    return tt


def kernel(x, mem, ln_in_g, ln_in_b, w_in, conv_w, conv_b, m_i_bias, m_f_bias, m_norm_g, g_lr_w, g_lr_b, g_norm_g, w_out, ln1_g, ln1_b, x_wq, x_wk, x_wv, x_wo, ln2_g, ln2_b, w_ff1, w_ff2, ln3_g, ln3_b):
    assert w_in.shape[0] == DEPTH == 1
    row = lambda a: a.reshape(1, -1).astype(_F32)
    wi = w_in[0]
    w_m = wi[:, O_MQ:O_MI].astype(_BF)
    w_g = wi[:, O_GQ:O_GLR].astype(_BF)
    w_gate = jnp.concatenate([wi[:, O_MI:O_GQ], wi[:, O_GLR:O_GLR + G_RANK],
                              jnp.zeros((D_MODEL, LANE - 2 * M_HEADS - G_RANK), wi.dtype)], axis=1).astype(_BF)
    gate_bias = jnp.concatenate([m_i_bias[0], m_f_bias[0],
                                 jnp.zeros((LANE - 2 * M_HEADS,), _F32)]).reshape(1, LANE)
    glw = jnp.zeros((LANE, G_HEADS * G_DK), _F32).at[2 * M_HEADS:2 * M_HEADS + G_RANK].set(g_lr_w[0]).astype(_BF)
    tt = _time_tile(x.shape[1], MIXER_ROWS)
    h1, wk_b, wv_b, wq_b, wo_b, w1_b, w2_b = _mixer_call(
        x, row(ln_in_g), row(ln_in_b), w_m, w_g, w_gate, conv_w[0].astype(_F32), row(conv_b[0]),
        gate_bias, row(m_norm_g[0]), glw, row(g_lr_b[0]), row(g_norm_g[0]), w_out[0].astype(_BF),
        row(ln1_g[0]), row(ln1_b[0]), tt, (x_wk, x_wv, x_wq, x_wo, w_ff1, w_ff2))
    wqk, vo = _memkv_call(mem, wk_b, wv_b, wq_b, wo_b)
    tm = _time_tile(x.shape[1], ATTN_ROWS)
    return _attnmlp_call(h1, wqk, vo, row(ln2_g[0]), row(ln2_b[0]), w1_b, w2_b, row(ln3_g[0]), row(ln3_b[0]), tm)
```

```python
import functools

import numpy as np
import jax
import jax.numpy as jnp
from jax import lax
from jax.experimental import pallas as pl
from jax.experimental.pallas import tpu as pltpu

D_MODEL = 1024
CHUNK = 64
SUB = 16
WIN = 2 * CHUNK
N_MEM = 256
M_HEADS, M_DK, M_DV = 4, 128, 128
M_QK = M_HEADS * M_DK
G_HEADS, G_DK, G_DV = 4, 64, 128
G_RANK = 16
G_TAU = 16.0
X_HEADS = 4
X_DH = D_MODEL // X_HEADS
D_FF = 4 * D_MODEL
MIXER_ROWS = 256
ATTN_ROWS = 1024
FF_BLK = 512
IN_BLK = 512
ROW_SPLIT = 2
ROW_DELAY = 7
DEPTH = 1
ALPHA = (2.0 * DEPTH) ** 0.25
LN_EPS = 1e-5
LOG2E = 1.4426950408889634
LANE = 128

C_MQ, C_MK, C_MV, C_MO = 0, 512, 1024, 1536
C_GQ, C_GK, C_GV, C_GG = 2048, 2304, 2560, 3072
C_GATE = 3584
IN_COLS_R = C_GATE + LANE
O_MQ, O_MK, O_MV, O_MO, O_MI, O_MF = 0, 512, 1024, 1536, 2048, 2052
O_GQ, O_GK, O_GV, O_GG, O_GLR = 2056, 2312, 2568, 3080, 3592

VMEM_LIMIT = 56 * 1024 * 1024

_BF = jnp.bfloat16
_F32 = jnp.float32


def _dot(a, b):
    return jnp.dot(a.astype(_BF), b.astype(_BF), preferred_element_type=_F32)


def _dot_nt(a, b):
    return lax.dot_general(a.astype(_BF), b.astype(_BF), (((1,), (1,)), ((), ())),
                           preferred_element_type=_F32)


def _dot_tn(a, b):
    return lax.dot_general(a.astype(_BF), b.astype(_BF), (((0,), (0,)), ((), ())),
                           preferred_element_type=_F32)


def _split2(x):
    hi = x.astype(_BF)
    lo = (x - hi.astype(_F32)).astype(_BF)
    return jnp.concatenate([hi, lo], axis=1)


def _merge2(y, n):
    return y[:, 0:n] + y[:, n:2 * n]


def _layer_norm(x, g, b):
    mu = jnp.mean(x, axis=-1, keepdims=True)
    xc = x - mu
    var = jnp.mean(xc * xc, axis=-1, keepdims=True)
    return xc * lax.rsqrt(var + LN_EPS) * g + b


def _log_sigmoid(x):
    return -(jnp.maximum(-x, 0.0) + jnp.log(1.0 + jnp.exp(-jnp.abs(x))))


def _sigmoid(x):
    return 1.0 / (1.0 + jnp.exp(-x))


def _round_robin(chains):
    while chains:
        alive = []
        for ch in chains:
            try:
                next(ch)
                alive.append(ch)
            except StopIteration:
                pass
        chains = alive


def _mixer_kernel(x_ref, lnin_g, lnin_b, w_m, w_g, w_gate, conv_w, conv_b, gate_bias, m_norm_g, glw, glb, g_norm_g,
                  tri64, tri16, ones16, gsel, rsel, *rest, tt, nt, n_cast):
    cast_in, rest = rest[:n_cast], rest[n_cast:]
    o_ref, cast_out = rest[0], rest[1:1 + n_cast]
    uext, eext, c_st, n_st, m_st, s_st, proj_buf = rest[1 + n_cast:]
    g = pl.program_id(0)

    @pl.when(g == 0)
    def _():
        proj_buf[1] = jnp.zeros(proj_buf.shape[1:], _F32)

    @pl.when((g == 0) | ((g - 1) % nt == 0))
    def _():
        uext[0:8, :] = jnp.zeros((8, 2 * M_QK), _F32)
        eext[0:8, :] = jnp.zeros((8, 2 * M_QK), _F32)
        c_st[...] = jnp.zeros(c_st.shape, _F32)
        n_st[...] = jnp.zeros(n_st.shape, _F32)
        m_st[...] = jnp.zeros(m_st.shape, _F32)
        s_st[...] = jnp.zeros(s_st.shape, _F32)

    def step(slot_a):
        slot_b = 1 - slot_a

        def in_proj():
            h0b = _layer_norm(x_ref[0], lnin_g[...], lnin_b[...]).astype(_BF)
            yield
            for w_ref, base in ((w_m, C_MQ), (w_g, C_GQ), (w_gate, C_GATE)):
                ncol = w_ref.shape[1]
                for c0 in range(0, ncol, IN_BLK):
                    c1 = min(c0 + IN_BLK, ncol)
                    proj_buf[slot_a, :, base + c0:base + c1] = jnp.dot(h0b, w_ref[:, c0:c1],
                                                                       preferred_element_type=_F32)
                    yield

        pb = proj_buf.at[slot_b]

        uext[8:8 + tt, :] = pb[:, C_MQ:C_MV]
        cw = conv_w[...]
        u0 = uext[8:8 + tt, :]
        u1 = uext[7:7 + tt, :]
        eext[8:8 + tt, :] = u0 * cw[1:2, :] + u1 * cw[0:1, :]
        conv = u0 * cw[3:4, :] + u1 * cw[2:3, :] + eext[6:6 + tt, :] + conv_b[...]
        uext[0:8, :] = uext[tt:tt + 8, :]
        eext[0:8, :] = eext[tt:tt + 8, :]
        qk = conv * _sigmoid(conv)
        mq = qk[:, 0:M_QK]
        mk = qk[:, M_QK:2 * M_QK] * (M_DK ** -0.5)
        mv = pb[:, C_MV:C_MO]
        mo = pb[:, C_MO:C_GQ]

        gb = pb[:, C_GATE:C_GATE + LANE] + gate_bias[...]
        lane = lax.broadcasted_iota(jnp.int32, (tt, LANE), 1)
        lf = _log_sigmoid(gb)
        bcum = _merge2(_dot(tri64[...], _split2(lf)), LANE)
        vcol = jnp.where((lane >= M_HEADS) & (lane < 2 * M_HEADS), bcum, gb) * LOG2E
        vrow = vcol.T

        gw = G_HEADS * G_DK
        la = _log_sigmoid(_dot(gb, glw[...]) + glb[...]) * (1.0 / G_TAU)
        la2 = _split2(la)
        bc = _merge2(_dot(tri16[...], la2), gw)
        bend = _merge2(_dot(ones16[...], la2), gw)
        gq = pb[:, C_GQ:C_GK] * (G_DK ** -0.5)
        gk = pb[:, C_GK:C_GV]
        gv = pb[:, C_GV:C_GG]
        gg = pb[:, C_GG:C_GATE]
        bc2 = bc * LOG2E
        bend2 = bend * LOG2E
        qh = gq * jnp.exp2(bc2)
        kh = gk * jnp.exp2(bend2 - bc2)
        eg = jnp.exp2(bend2)
        nsub = tt // SUB

        rr = lax.broadcasted_iota(jnp.int32, (WIN, WIN), 0)
        cc = lax.broadcasted_iota(jnp.int32, (WIN, WIN), 1)
        valid = (cc <= rr) & ((cc >= CHUNK) == (rr >= CHUNK))
        first = lax.broadcasted_iota(jnp.int32, (WIN, 1), 0) < CHUNK
        mg = m_norm_g[...]
        m_outs = [None] * M_HEADS

        def mlstm_head(h):
            hs = slice(h * M_DK, (h + 1) * M_DK)
            c_cur = c_st[h]
            n_cur = n_st[h]
            m_cur = m_st[h]
            win_outs = []
            for w in range(tt // WIN):
                rs = slice(w * WIN, (w + 1) * WIN)
                q, k, v = mq[rs, hs], mk[rs, hs], mv[rs, hs]
                li_col, b_col = vcol[rs, h:h + 1], vcol[rs, M_HEADS + h:M_HEADS + h + 1]
                li_row, b_row = vrow[h:h + 1, rs], vrow[M_HEADS + h:M_HEADS + h + 1, rs]
                dm = jnp.where(valid, b_col - b_row + li_row, -jnp.inf)
                md = jnp.max(dm, axis=1, keepdims=True)
                g0, g1 = b_col[CHUNK - 1:CHUNK, :], b_col[WIN - 1:WIN, :]
                a_col = jnp.where(first, g0, g1) - b_col + li_col
                yield
                am0 = jnp.max(a_col[0:CHUNK], axis=0, keepdims=True)
                am1 = jnp.max(a_col[CHUNK:WIN], axis=0, keepdims=True)
                wk = jnp.exp2(a_col - jnp.where(first, am0, am1)) * k
                s_raw = _dot_nt(q, k)
                yield
                cprev, nprev, mprev = [], [], []
                for c, (gc, am) in enumerate(((g0, am0), (g1, am1))):
                    cs = slice(c * CHUNK, (c + 1) * CHUNK)
                    cprev.append(c_cur)
                    nprev.append(n_cur)
                    mprev.append(m_cur)
                    u_mat = _dot_tn(wk[cs], v[cs])
                    u_vec = jnp.sum(wk[cs], axis=0, keepdims=True)
                    m_new = jnp.maximum(gc + m_cur, am)
                    dec = jnp.exp2(gc + m_cur - m_new)
                    inj = jnp.exp2(am - m_new)
                    c_cur = dec * c_cur + inj * u_mat
                    n_cur = dec * n_cur + inj * u_vec
                    m_cur = m_new
                yield
                inter = b_col + jnp.where(first, mprev[0], mprev[1])
                m_t = jnp.maximum(inter, md)
                sc = jnp.exp2(inter - m_t)
                s_mat = s_raw * jnp.exp2(dm - m_t)
                yield
                sv = _dot(s_mat, v)
                qc = jnp.concatenate([_dot(q[0:CHUNK], cprev[0]), _dot(q[CHUNK:WIN], cprev[1])], axis=0)
                qn = jnp.sum(q * jnp.where(first, nprev[0], nprev[1]), axis=1, keepdims=True)
                ssum = jnp.sum(s_mat, axis=1, keepdims=True)
                yield
                num = sc * qc + sv
                den = sc * qn + ssum
                hh = num * (1.0 / jnp.maximum(jnp.abs(den), jnp.exp2(-m_t)))
                mu = jnp.mean(hh, axis=1, keepdims=True)
                yield
                hc = hh - mu
                var = jnp.mean(hc * hc, axis=1, keepdims=True)
                yield
                win_outs.append(hc * lax.rsqrt(var + LN_EPS))
            c_st[h] = c_cur
            n_st[h] = n_cur
            m_st[h] = m_cur
            hn = win_outs[0] if len(win_outs) == 1 else jnp.concatenate(win_outs, axis=0)
            m_outs[h] = _sigmoid(mo[:, hs]) * (hn * mg[:, hs])

        o_diag = [None] * G_HEADS

        def gla_diag():
            gk3 = gk.reshape(nsub, SUB, gw)
            bc3 = bc2.reshape(nsub, SUB, gw)
            half = SUB // 2
            hi = lambda a: a.reshape(nsub, 2, half, a.shape[-1])[:, 1].reshape(nsub * half, a.shape[-1])
            gq_hi, bc2_hi = hi(gq), hi(bc2)
            p_acc = jnp.zeros((tt, LANE), _F32)
            p_hi = jnp.zeros((tt // 2, LANE), _F32)
            for s in range(SUB):
                if s < half:
                    kb = jnp.broadcast_to(gk3[:, s:s + 1, :], (nsub, SUB, gw)).reshape(tt, gw)
                    bb = jnp.broadcast_to(bc3[:, s:s + 1, :], (nsub, SUB, gw)).reshape(tt, gw)
                    y = gq * kb * jnp.exp2(jnp.minimum(bc2 - bb, 0.0))
                    p_acc = p_acc + _dot(y, gsel[s])
                else:
                    kb = jnp.broadcast_to(gk3[:, s:s + 1, :], (nsub, half, gw)).reshape(tt // 2, gw)
                    bb = jnp.broadcast_to(bc3[:, s:s + 1, :], (nsub, half, gw)).reshape(tt // 2, gw)
                    y = gq_hi * kb * jnp.exp2(jnp.minimum(bc2_hi - bb, 0.0))
                    p_hi = p_hi + _dot(y, gsel[s])
                yield
            p4 = p_acc.reshape(nsub, 2, half, LANE)
            p_acc = jnp.concatenate([p4[:, 0:1], p4[:, 1:2] + p_hi.reshape(nsub, 1, half, LANE)],
                                    axis=1).reshape(tt, LANE)
            p_bf = p_acc.astype(_BF)
            r2 = lax.broadcasted_iota(jnp.int32, (WIN, WIN), 0)
            c2 = lax.broadcasted_iota(jnp.int32, (WIN, WIN), 1)
            sub_mask = (c2 <= r2) & ((c2 // SUB) == (r2 // SUB))
            for h in range(G_HEADS):
                outs = []
                for w in range(tt // WIN):
                    rs = slice(w * WIN, (w + 1) * WIN)
                    a_full = jnp.dot(p_bf[rs], rsel[h], preferred_element_type=_F32)
                    a_h = jnp.where(sub_mask, a_full, 0.0)
                    outs.append(_dot(a_h, gv[rs, h * G_DV:(h + 1) * G_DV]))
                o_diag[h] = outs[0] if len(outs) == 1 else jnp.concatenate(outs, axis=0)
                yield

        head_of_lane = lax.broadcasted_iota(jnp.int32, (SUB, gw), 1) // G_DK
        o_int = [[None] * nsub for _ in range(G_HEADS)]

        def gla_rec():
            st = s_st[...]
            for j in range(nsub):
                js = slice(j * SUB, (j + 1) * SUB)
                qj, kj = qh[js], kh[js]
                qq = jnp.concatenate([jnp.where(head_of_lane == h, qj, 0.0) for h in range(G_HEADS)], axis=0)
                kk = jnp.concatenate([jnp.where(head_of_lane == h, kj, 0.0) for h in range(G_HEADS)], axis=0)
                vv = jnp.concatenate([gv[js, h * G_DV:(h + 1) * G_DV] for h in range(G_HEADS)], axis=0)
                oj = _dot_nt(qq, st)
                for h in range(G_HEADS):
                    o_int[h][j] = oj[h * SUB:(h + 1) * SUB]
                st = eg[j * SUB:j * SUB + 1, :] * st + _dot_tn(vv, kk)
                yield
            s_st[...] = st

        def cast_weights():
            for src_ref, dst_ref in zip(cast_in, cast_out):
                dst_ref[...] = src_ref[0].astype(_BF)
                yield

        _round_robin([in_proj(), gla_diag(), gla_rec()] + [mlstm_head(h) for h in range(M_HEADS)]
                     + [cast_weights()])

        gn = g_norm_g[...]
        g_outs = []
        for h in range(G_HEADS):
            hs = slice(h * G_DV, (h + 1) * G_DV)
            og = o_diag[h] + jnp.concatenate(o_int[h], axis=0)
            rms = lax.rsqrt(jnp.mean(og * og, axis=1, keepdims=True) + LN_EPS)
            gate = gg[:, hs]
            g_outs.append(gate * _sigmoid(gate) * (og * rms * gn[:, hs]))

        o_ref[0] = jnp.concatenate(m_outs + g_outs, axis=1).astype(_BF)

    for par in range(2):
        pl.when(g % 2 == par)(functools.partial(step, par))


def _mixer_constants(tt):
    r = np.arange(tt)
    tri64 = ((r[None, :] <= r[:, None]) & (r[None, :] // CHUNK == r[:, None] // CHUNK))
    same16 = (r[None, :] // SUB == r[:, None] // SUB)
    tri16 = (r[None, :] <= r[:, None]) & same16
    gw = G_HEADS * G_DK
    gsel = np.zeros((SUB, gw, LANE), np.float32)
    for s in range(SUB):
        for h in range(G_HEADS):
            gsel[s, h * G_DK:(h + 1) * G_DK, h * SUB + s] = 1.0
    rsel = np.zeros((G_HEADS, LANE, WIN), np.float32)
    for h in range(G_HEADS):
        for s in range(SUB):
            rsel[h, h * SUB + s, s::SUB] = 1.0
    to_bf = lambda a: jnp.asarray(a.astype(np.float32), dtype=_BF)
    return to_bf(tri64), to_bf(tri16), to_bf(same16), to_bf(gsel), to_bf(rsel)


def _full(shape):
    nd = len(shape)
    return pl.BlockSpec(shape, lambda b, i, _nd=nd: (0,) * _nd)


def _mixer_call(x, lnin_g, lnin_b, w_m, w_g, w_gate, conv_w, conv_b, gate_bias, m_norm_g, glw, glb, g_norm_g,
                tt, to_cast):
    bsz, t, d = x.shape
    consts = _mixer_constants(tt)
    params = (lnin_g, lnin_b, w_m, w_g, w_gate, conv_w, conv_b, gate_bias, m_norm_g, glw, glb, g_norm_g) + consts
    nt = t // tt
    ntiles = bsz * nt
    in_tile = pl.BlockSpec((1, tt, d), lambda g: (jnp.minimum(g, ntiles - 1) // nt, jnp.minimum(g, ntiles - 1) % nt, 0))
    out_tile = pl.BlockSpec((1, tt, d), lambda g: (jnp.maximum(g - 1, 0) // nt, jnp.maximum(g - 1, 0) % nt, 0))
    full = lambda shape: pl.BlockSpec(shape, lambda g, _nd=len(shape): (0,) * _nd)
    cast_in = [pl.BlockSpec((1, w.shape[1] // ntiles, w.shape[2]), lambda g: (0, jnp.minimum(g, ntiles - 1), 0))
               for w in to_cast]
    cast_out = [pl.BlockSpec((w.shape[1] // ntiles, w.shape[2]), lambda g: (jnp.minimum(g, ntiles - 1), 0))
                for w in to_cast]
    assert all(w.shape[1] % (16 * ntiles) == 0 for w in to_cast)
    return pl.pallas_call(
        functools.partial(_mixer_kernel, tt=tt, nt=nt, n_cast=len(to_cast)),
        out_shape=[jax.ShapeDtypeStruct((bsz, t, d), _BF)]
        + [jax.ShapeDtypeStruct(w.shape[1:], _BF) for w in to_cast],
        grid=(ntiles + 1,),
        in_specs=[in_tile] + [full(p.shape) for p in params] + cast_in,
        out_specs=[out_tile] + cast_out,
        scratch_shapes=[
            pltpu.VMEM((tt + 8, 2 * M_QK), _F32),
            pltpu.VMEM((tt + 8, 2 * M_QK), _F32),
            pltpu.VMEM((M_HEADS, M_DK, M_DV), _F32),
            pltpu.VMEM((M_HEADS, 1, M_DK), _F32),
            pltpu.VMEM((M_HEADS, 1, 1), _F32),
            pltpu.VMEM((G_DV, G_HEADS * G_DK), _F32),
            pltpu.VMEM((2, tt, IN_COLS_R), _F32),
        ],
        compiler_params=pltpu.CompilerParams(
            dimension_semantics=("arbitrary",), vmem_limit_bytes=VMEM_LIMIT),
        name="mixer",
    )(x, *params, *to_cast)


def _memkv_kernel(mem_ref, wk_ref, wv_ref, wq_ref, wo_ref, wqk_ref, vo_ref):
    m = mem_ref[0]
    k = _dot(m, wk_ref[...])
    v = _dot(m, wv_ref[...])
    for h in range(X_HEADS):
        hs = slice(h * X_DH, (h + 1) * X_DH)
        ms = slice(h * N_MEM, (h + 1) * N_MEM)
        wqk_ref[0, :, ms] = (_dot_nt(wq_ref[:, hs], k[:, hs]) * (X_DH ** -0.5)).astype(_BF)
        vo_ref[0, ms, :] = _dot(v[:, hs], wo_ref[hs, :]).astype(_BF)


def _memkv_call(mem, wk, wv, wq, wo):
    bsz, nm, d = mem.shape
    blk = pl.BlockSpec((1, nm, d), lambda b: (b, 0, 0))
    wspec = pl.BlockSpec((d, d), lambda b: (0, 0))
    return pl.pallas_call(
        _memkv_kernel,
        out_shape=(jax.ShapeDtypeStruct((bsz, d, X_HEADS * nm), _BF),
                   jax.ShapeDtypeStruct((bsz, X_HEADS * nm, d), _BF)),
        grid=(bsz,),
        in_specs=[blk, wspec, wspec, wspec, wspec],
        out_specs=(pl.BlockSpec((1, d, X_HEADS * nm), lambda b: (b, 0, 0)),
                   pl.BlockSpec((1, X_HEADS * nm, d), lambda b: (b, 0, 0))),
        compiler_params=pltpu.CompilerParams(
            dimension_semantics=("arbitrary",), vmem_limit_bytes=VMEM_LIMIT),
        name="memkv",
    )(mem, wk, wv, wq, wo)


def _attnmlp_kernel(x_ref, y_ref, wqk_ref, vo_ref, lnin_g, lnin_b, w_out, ln1_g, ln1_b, ln2_g, ln2_b, w1, w2,
                    ln3_g, ln3_b, o_ref):
    rows = x_ref.shape[1] // ROW_SPLIT

    def chain(r, wait):
        for _ in range(wait):
            yield
        rs = slice(r * rows, (r + 1) * rows)
        h0 = _layer_norm(x_ref[0, rs, :], lnin_g[...], lnin_b[...])
        mix = jnp.dot(y_ref[0, rs, :], w_out[...], preferred_element_type=_F32)
        h1 = _layer_norm(ALPHA * h0 + mix, ln1_g[...], ln1_b[...])
        yield
        s_all = _dot(h1, wqk_ref[0])
        yield
        probs = []
        for h in range(X_HEADS):
            s = s_all[:, h * N_MEM:(h + 1) * N_MEM]
            e = jnp.exp(s - jnp.max(s, axis=1, keepdims=True))
            probs.append((e * (1.0 / jnp.sum(e, axis=1, keepdims=True))).astype(_BF))
            yield
        xa = jnp.dot(jnp.concatenate(probs, axis=1), vo_ref[0], preferred_element_type=_F32)
        yield
        h2 = _layer_norm(ALPHA * h1 + xa, ln2_g[...], ln2_b[...])
        h2b = h2.astype(_BF)
        yield
        ff = jnp.zeros(h2.shape, _F32)
        for f in range(D_FF // FF_BLK):
            fs = slice(f * FF_BLK, (f + 1) * FF_BLK)
            hid = jnp.maximum(jnp.dot(h2b, w1[:, fs], preferred_element_type=_F32), 0.0)
            ff = ff + _dot(hid * hid, w2[fs, :])
            yield
        o_ref[0, rs, :] = _layer_norm(ALPHA * h2 + ff, ln3_g[...], ln3_b[...])

    _round_robin([chain(r, r * ROW_DELAY) for r in range(ROW_SPLIT)])


def _attnmlp_call(x, y, wqk, vo, lnin_g, lnin_b, w_out, ln1_g, ln1_b, ln2_g, ln2_b, w1, w2, ln3_g, ln3_b, tm):
    bsz, t, d = x.shape
    tile = pl.BlockSpec((1, tm, d), lambda b, i: (b, i, 0))
    per_batch = lambda a: pl.BlockSpec((1,) + a.shape[1:], lambda b, i: (b, 0, 0), pipeline_mode=pl.Buffered(1))
    params = (lnin_g, lnin_b, w_out, ln1_g, ln1_b, ln2_g, ln2_b, w1, w2, ln3_g, ln3_b)
    return pl.pallas_call(
        _attnmlp_kernel,
        out_shape=jax.ShapeDtypeStruct((bsz, t, d), _F32),
        grid=(bsz, t // tm),
        in_specs=[tile, tile, per_batch(wqk), per_batch(vo)] + [_full(p.shape) for p in params],
        out_specs=tile,
        compiler_params=pltpu.CompilerParams(
            dimension_semantics=("arbitrary", "arbitrary"), vmem_limit_bytes=VMEM_LIMIT),
        name="attnmlp",
    )(x, y, wqk, vo, *params)


def _time_tile(t, target):
    tt = min(t, target)
    assert t % tt == 0 and tt % WIN == 0, (t, tt)
    return tt


def kernel(x, mem, ln_in_g, ln_in_b, w_in, conv_w, conv_b, m_i_bias, m_f_bias, m_norm_g, g_lr_w, g_lr_b, g_norm_g, w_out, ln1_g, ln1_b, x_wq, x_wk, x_wv, x_wo, ln2_g, ln2_b, w_ff1, w_ff2, ln3_g, ln3_b):
    assert w_in.shape[0] == DEPTH == 1
    row = lambda a: a.reshape(1, -1).astype(_F32)
    wi = w_in[0]
    w_m = wi[:, O_MQ:O_MI].astype(_BF)
    w_g = wi[:, O_GQ:O_GLR].astype(_BF)
    w_gate = jnp.concatenate([wi[:, O_MI:O_GQ], wi[:, O_GLR:O_GLR + G_RANK],
                              jnp.zeros((D_MODEL, LANE - 2 * M_HEADS - G_RANK), wi.dtype)], axis=1).astype(_BF)
    gate_bias = jnp.concatenate([m_i_bias[0], m_f_bias[0],
                                 jnp.zeros((LANE - 2 * M_HEADS,), _F32)]).reshape(1, LANE)
    glw = jnp.zeros((LANE, G_HEADS * G_DK), _F32).at[2 * M_HEADS:2 * M_HEADS + G_RANK].set(g_lr_w[0]).astype(_BF)
    tt = _time_tile(x.shape[1], MIXER_ROWS)
    y, wk_b, wv_b, wq_b, wo_b, w1_b, w2_b, wout_b = _mixer_call(
        x, row(ln_in_g), row(ln_in_b), w_m, w_g, w_gate, conv_w[0].astype(_F32), row(conv_b[0]),
        gate_bias, row(m_norm_g[0]), glw, row(g_lr_b[0]), row(g_norm_g[0]), tt,
        (x_wk, x_wv, x_wq, x_wo, w_ff1, w_ff2, w_out))
    wqk, vo = _memkv_call(mem, wk_b, wv_b, wq_b, wo_b)
    tm = _time_tile(x.shape[1], ATTN_ROWS)
    return _attnmlp_call(x, y, wqk, vo, row(ln_in_g), row(ln_in_b), wout_b, row(ln1_g[0]), row(ln1_b[0]),
                         row(ln2_g[0]), row(ln2_b[0]), w1_b, w2_b, row(ln3_g[0]), row(ln3_b[0]), tm)
```

```python
import functools

import numpy as np
import jax
import jax.numpy as jnp
from jax import lax
from jax.experimental import pallas as pl
from jax.experimental.pallas import tpu as pltpu

D_MODEL = 1024
CHUNK = 64
SUB = 16
WIN = 2 * CHUNK
N_MEM = 256
M_HEADS, M_DK, M_DV = 4, 128, 128
M_QK = M_HEADS * M_DK
G_HEADS, G_DK, G_DV = 4, 64, 128
G_RANK = 16
G_TAU = 16.0
X_HEADS = 4
X_DH = D_MODEL // X_HEADS
D_FF = 4 * D_MODEL
MIXER_ROWS = 256
ATTN_ROWS = 1024
FF_BLK = 512
IN_BLK = 512
ROW_SPLIT = 2
ROW_DELAY = 6
DEPTH = 1
ALPHA = (2.0 * DEPTH) ** 0.25
LN_EPS = 1e-5
LOG2E = 1.4426950408889634
LANE = 128

C_MQ, C_MK, C_MV, C_MO = 0, 512, 1024, 1536
C_GQ, C_GK, C_GV, C_GG = 2048, 2304, 2560, 3072
C_GATE = 3584
IN_COLS_R = C_GATE + LANE
O_MQ, O_MK, O_MV, O_MO, O_MI, O_MF = 0, 512, 1024, 1536, 2048, 2052
O_GQ, O_GK, O_GV, O_GG, O_GLR = 2056, 2312, 2568, 3080, 3592

VMEM_LIMIT = 56 * 1024 * 1024

_BF = jnp.bfloat16
_F32 = jnp.float32


def _dot(a, b):
    return jnp.dot(a.astype(_BF), b.astype(_BF), preferred_element_type=_F32)


def _dot_nt(a, b):
    return lax.dot_general(a.astype(_BF), b.astype(_BF), (((1,), (1,)), ((), ())),
                           preferred_element_type=_F32)


def _dot_tn(a, b):
    return lax.dot_general(a.astype(_BF), b.astype(_BF), (((0,), (0,)), ((), ())),
                           preferred_element_type=_F32)


def _split2(x):
    hi = x.astype(_BF)
    lo = (x - hi.astype(_F32)).astype(_BF)
    return jnp.concatenate([hi, lo], axis=1)


def _merge2(y, n):
    return y[:, 0:n] + y[:, n:2 * n]


def _layer_norm(x, g, b):
    mu = jnp.mean(x, axis=-1, keepdims=True)
    xc = x - mu
    var = jnp.mean(xc * xc, axis=-1, keepdims=True)
    return xc * lax.rsqrt(var + LN_EPS) * g + b


def _log_sigmoid(x):
    return -(jnp.maximum(-x, 0.0) + jnp.log(1.0 + jnp.exp(-jnp.abs(x))))


def _sigmoid(x):
    return 1.0 / (1.0 + jnp.exp(-x))


def _round_robin(chains):
    while chains:
        alive = []
        for ch in chains:
            try:
                next(ch)
                alive.append(ch)
            except StopIteration:
                pass
        chains = alive


def _mixer_kernel(x_ref, lnin_g, lnin_b, w_m, w_g, w_gate, conv_w, conv_b, gate_bias, m_norm_g, glw, glb, g_norm_g,
                  w_out, ln1_g, ln1_b, tri64, tri16, ones16, gsel, rsel, *rest, tt, nt, n_cast):
    cast_in, rest = rest[:n_cast], rest[n_cast:]
    o_ref, cast_out = rest[0], rest[1:1 + n_cast]
    uext, eext, c_st, n_st, m_st, s_st, proj_buf, h0_buf = rest[1 + n_cast:]
    g = pl.program_id(0)

    @pl.when(g == 0)
    def _():
        proj_buf[1] = jnp.zeros(proj_buf.shape[1:], _F32)
        h0_buf[1] = jnp.zeros(h0_buf.shape[1:], _F32)

    @pl.when((g == 0) | ((g - 1) % nt == 0))
    def _():
        uext[0:8, :] = jnp.zeros((8, 2 * M_QK), _F32)
        eext[0:8, :] = jnp.zeros((8, 2 * M_QK), _F32)
        c_st[...] = jnp.zeros(c_st.shape, _F32)
        n_st[...] = jnp.zeros(n_st.shape, _F32)
        m_st[...] = jnp.zeros(m_st.shape, _F32)
        s_st[...] = jnp.zeros(s_st.shape, _F32)

    def step(slot_a):
        slot_b = 1 - slot_a

        def in_proj():
            h0n = _layer_norm(x_ref[0], lnin_g[...], lnin_b[...])
            h0_buf[slot_a] = h0n
            h0b = h0n.astype(_BF)
            yield
            for w_ref, base in ((w_m, C_MQ), (w_g, C_GQ), (w_gate, C_GATE)):
                ncol = w_ref.shape[1]
                for c0 in range(0, ncol, IN_BLK):
                    c1 = min(c0 + IN_BLK, ncol)
                    proj_buf[slot_a, :, base + c0:base + c1] = jnp.dot(h0b, w_ref[:, c0:c1],
                                                                       preferred_element_type=_F32)
                    yield

        pb = proj_buf.at[slot_b]
        h0 = h0_buf[slot_b]

        uext[8:8 + tt, :] = pb[:, C_MQ:C_MV]
        cw = conv_w[...]
        u0 = uext[8:8 + tt, :]
        u1 = uext[7:7 + tt, :]
        eext[8:8 + tt, :] = u0 * cw[1:2, :] + u1 * cw[0:1, :]
        conv = u0 * cw[3:4, :] + u1 * cw[2:3, :] + eext[6:6 + tt, :] + conv_b[...]
        uext[0:8, :] = uext[tt:tt + 8, :]
        eext[0:8, :] = eext[tt:tt + 8, :]
        qk = conv * _sigmoid(conv)
        mq = qk[:, 0:M_QK]
        mk = qk[:, M_QK:2 * M_QK] * (M_DK ** -0.5)

        gb = pb[:, C_GATE:C_GATE + LANE] + gate_bias[...]
        lane = lax.broadcasted_iota(jnp.int32, (tt, LANE), 1)
        lf = _log_sigmoid(gb)
        bcum = _merge2(_dot(tri64[...], _split2(lf)), LANE)
        vcol = jnp.where((lane >= M_HEADS) & (lane < 2 * M_HEADS), bcum, gb) * LOG2E
        vrow = vcol.T

        gw = G_HEADS * G_DK
        la = _log_sigmoid(_dot(gb, glw[...]) + glb[...]) * (1.0 / G_TAU)
        la2 = _split2(la)
        bc = _merge2(_dot(tri16[...], la2), gw)
        bend = _merge2(_dot(ones16[...], la2), gw)
        gq = pb[:, C_GQ:C_GK] * (G_DK ** -0.5)
        gk = pb[:, C_GK:C_GV]
        bc2 = bc * LOG2E
        bend2 = bend * LOG2E
        qh = gq * jnp.exp2(bc2)
        kh = gk * jnp.exp2(bend2 - bc2)
        eg = jnp.exp2(bend2)
        nsub = tt // SUB

        rr = lax.broadcasted_iota(jnp.int32, (WIN, WIN), 0)
        cc = lax.broadcasted_iota(jnp.int32, (WIN, WIN), 1)
        valid = (cc <= rr) & ((cc >= CHUNK) == (rr >= CHUNK))
        first = lax.broadcasted_iota(jnp.int32, (WIN, 1), 0) < CHUNK
        mg = m_norm_g[...]
        m_outs = [None] * M_HEADS

        def mlstm_head(h):
            hs = slice(h * M_DK, (h + 1) * M_DK)
            c_cur = c_st[h]
            n_cur = n_st[h]
            m_cur = m_st[h]
            win_outs = []
            for w in range(tt // WIN):
                rs = slice(w * WIN, (w + 1) * WIN)
                q, k, v = mq[rs, hs], mk[rs, hs], pb[rs, C_MV + h * M_DV:C_MV + (h + 1) * M_DV]
                li_col, b_col = vcol[rs, h:h + 1], vcol[rs, M_HEADS + h:M_HEADS + h + 1]
                li_row, b_row = vrow[h:h + 1, rs], vrow[M_HEADS + h:M_HEADS + h + 1, rs]
                dm = jnp.where(valid, b_col - b_row + li_row, -jnp.inf)
                md = jnp.max(dm, axis=1, keepdims=True)
                g0, g1 = b_col[CHUNK - 1:CHUNK, :], b_col[WIN - 1:WIN, :]
                a_col = jnp.where(first, g0, g1) - b_col + li_col
                yield
                am0 = jnp.max(a_col[0:CHUNK], axis=0, keepdims=True)
                am1 = jnp.max(a_col[CHUNK:WIN], axis=0, keepdims=True)
                wk = jnp.exp2(a_col - jnp.where(first, am0, am1)) * k
                s_raw = _dot_nt(q, k)
                yield
                cprev, nprev, mprev = [], [], []
                for c, (gc, am) in enumerate(((g0, am0), (g1, am1))):
                    cs = slice(c * CHUNK, (c + 1) * CHUNK)
                    cprev.append(c_cur)
                    nprev.append(n_cur)
                    mprev.append(m_cur)
                    u_mat = _dot_tn(wk[cs], v[cs])
                    u_vec = jnp.sum(wk[cs], axis=0, keepdims=True)
                    m_new = jnp.maximum(gc + m_cur, am)
                    dec = jnp.exp2(gc + m_cur - m_new)
                    inj = jnp.exp2(am - m_new)
                    c_cur = dec * c_cur + inj * u_mat
                    n_cur = dec * n_cur + inj * u_vec
                    m_cur = m_new
                yield
                inter = b_col + jnp.where(first, mprev[0], mprev[1])
                m_t = jnp.maximum(inter, md)
                sc = jnp.exp2(inter - m_t)
                s_mat = s_raw * jnp.exp2(dm - m_t)
                yield
                sv = _dot(s_mat, v)
                qc = jnp.concatenate([_dot(q[0:CHUNK], cprev[0]), _dot(q[CHUNK:WIN], cprev[1])], axis=0)
                qn = jnp.sum(q * jnp.where(first, nprev[0], nprev[1]), axis=1, keepdims=True)
                ssum = jnp.sum(s_mat, axis=1, keepdims=True)
                yield
                num = sc * qc + sv
                den = sc * qn + ssum
                hh = num * (1.0 / jnp.maximum(jnp.abs(den), jnp.exp2(-m_t)))
                mu = jnp.mean(hh, axis=1, keepdims=True)
                yield
                hc = hh - mu
                var = jnp.mean(hc * hc, axis=1, keepdims=True)
                yield
                win_outs.append(hc * lax.rsqrt(var + LN_EPS))
            c_st[h] = c_cur
            n_st[h] = n_cur
            m_st[h] = m_cur
            hn = win_outs[0] if len(win_outs) == 1 else jnp.concatenate(win_outs, axis=0)
            m_outs[h] = _sigmoid(pb[:, C_MO + h * M_DV:C_MO + (h + 1) * M_DV]) * (hn * mg[:, hs])

        o_diag = [None] * G_HEADS

        def gla_diag():
            gk3 = gk.reshape(nsub, SUB, gw)
            bc3 = bc2.reshape(nsub, SUB, gw)
            half = SUB // 2
            hi = lambda a: a.reshape(nsub, 2, half, a.shape[-1])[:, 1].reshape(nsub * half, a.shape[-1])
            gq_hi, bc2_hi = hi(gq), hi(bc2)
            p_acc = jnp.zeros((tt, LANE), _F32)
            p_hi = jnp.zeros((tt // 2, LANE), _F32)
            for s in range(SUB):
                if s < half:
                    kb = jnp.broadcast_to(gk3[:, s:s + 1, :], (nsub, SUB, gw)).reshape(tt, gw)
                    bb = jnp.broadcast_to(bc3[:, s:s + 1, :], (nsub, SUB, gw)).reshape(tt, gw)
                    y = gq * kb * jnp.exp2(jnp.minimum(bc2 - bb, 0.0))
                    p_acc = p_acc + _dot(y, gsel[s])
                else:
                    kb = jnp.broadcast_to(gk3[:, s:s + 1, :], (nsub, half, gw)).reshape(tt // 2, gw)
                    bb = jnp.broadcast_to(bc3[:, s:s + 1, :], (nsub, half, gw)).reshape(tt // 2, gw)
                    y = gq_hi * kb * jnp.exp2(jnp.minimum(bc2_hi - bb, 0.0))
                    p_hi = p_hi + _dot(y, gsel[s])
                yield
            p4 = p_acc.reshape(nsub, 2, half, LANE)
            p_acc = jnp.concatenate([p4[:, 0:1], p4[:, 1:2] + p_hi.reshape(nsub, 1, half, LANE)],
                                    axis=1).reshape(tt, LANE)
            p_bf = p_acc.astype(_BF)
            r2 = lax.broadcasted_iota(jnp.int32, (WIN, WIN), 0)
            c2 = lax.broadcasted_iota(jnp.int32, (WIN, WIN), 1)
            sub_mask = (c2 <= r2) & ((c2 // SUB) == (r2 // SUB))
            for h in range(G_HEADS):
                outs = []
                for w in range(tt // WIN):
                    rs = slice(w * WIN, (w + 1) * WIN)
                    a_full = jnp.dot(p_bf[rs], rsel[h], preferred_element_type=_F32)
                    a_h = jnp.where(sub_mask, a_full, 0.0)
                    outs.append(_dot(a_h, pb[rs, C_GV + h * G_DV:C_GV + (h + 1) * G_DV]))
                o_diag[h] = outs[0] if len(outs) == 1 else jnp.concatenate(outs, axis=0)
                yield

        head_of_lane = lax.broadcasted_iota(jnp.int32, (SUB, gw), 1) // G_DK
        o_int = [[None] * nsub for _ in range(G_HEADS)]

        def gla_rec():
            st = s_st[...]
            for j in range(nsub):
                js = slice(j * SUB, (j + 1) * SUB)
                qj, kj = qh[js], kh[js]
                qq = jnp.concatenate([jnp.where(head_of_lane == h, qj, 0.0) for h in range(G_HEADS)], axis=0)
                kk = jnp.concatenate([jnp.where(head_of_lane == h, kj, 0.0) for h in range(G_HEADS)], axis=0)
                vv = jnp.concatenate([pb[js, C_GV + h * G_DV:C_GV + (h + 1) * G_DV] for h in range(G_HEADS)],
                                     axis=0)
                oj = _dot_nt(qq, st)
                for h in range(G_HEADS):
                    o_int[h][j] = oj[h * SUB:(h + 1) * SUB]
                st = eg[j * SUB:j * SUB + 1, :] * st + _dot_tn(vv, kk)
                yield
            s_st[...] = st

        def cast_weights():
            for src_ref, dst_ref in zip(cast_in, cast_out):
                dst_ref[...] = src_ref[0].astype(_BF)
                yield

        _round_robin([in_proj(), gla_diag(), gla_rec()] + [mlstm_head(h) for h in range(M_HEADS)]
                     + [cast_weights()])

        gn = g_norm_g[...]
        g_outs = []
        for h in range(G_HEADS):
            hs = slice(h * G_DV, (h + 1) * G_DV)
            og = o_diag[h] + jnp.concatenate(o_int[h], axis=0)
            rms = lax.rsqrt(jnp.mean(og * og, axis=1, keepdims=True) + LN_EPS)
            gate = pb[:, C_GG + h * G_DV:C_GG + (h + 1) * G_DV]
            g_outs.append(gate * _sigmoid(gate) * (og * rms * gn[:, hs]))

        y = jnp.concatenate(m_outs + g_outs, axis=1)
        mix = _dot(y, w_out[...])
        o_ref[0] = _layer_norm(ALPHA * h0 + mix, ln1_g[...], ln1_b[...])

    for par in range(2):
        pl.when(g % 2 == par)(functools.partial(step, par))


def _mixer_constants(tt):
    r = np.arange(tt)
    tri64 = ((r[None, :] <= r[:, None]) & (r[None, :] // CHUNK == r[:, None] // CHUNK))
    same16 = (r[None, :] // SUB == r[:, None] // SUB)
    tri16 = (r[None, :] <= r[:, None]) & same16
    gw = G_HEADS * G_DK
    gsel = np.zeros((SUB, gw, LANE), np.float32)
    for s in range(SUB):
        for h in range(G_HEADS):
            gsel[s, h * G_DK:(h + 1) * G_DK, h * SUB + s] = 1.0
    rsel = np.zeros((G_HEADS, LANE, WIN), np.float32)
    for h in range(G_HEADS):
        for s in range(SUB):
            rsel[h, h * SUB + s, s::SUB] = 1.0
    to_bf = lambda a: jnp.asarray(a.astype(np.float32), dtype=_BF)
    return to_bf(tri64), to_bf(tri16), to_bf(same16), to_bf(gsel), to_bf(rsel)


def _full(shape):
    nd = len(shape)
    return pl.BlockSpec(shape, lambda b, i, _nd=nd: (0,) * _nd)


def _mixer_call(x, lnin_g, lnin_b, w_m, w_g, w_gate, conv_w, conv_b, gate_bias, m_norm_g, glw, glb, g_norm_g,
                w_out, ln1_g, ln1_b, tt, to_cast):
    bsz, t, d = x.shape
    consts = _mixer_constants(tt)
    params = (lnin_g, lnin_b, w_m, w_g, w_gate, conv_w, conv_b, gate_bias, m_norm_g, glw, glb, g_norm_g,
              w_out, ln1_g, ln1_b) + consts
    nt = t // tt
    ntiles = bsz * nt
    in_tile = pl.BlockSpec((1, tt, d), lambda g: (jnp.minimum(g, ntiles - 1) // nt, jnp.minimum(g, ntiles - 1) % nt, 0))
    out_tile = pl.BlockSpec((1, tt, d), lambda g: (jnp.maximum(g - 1, 0) // nt, jnp.maximum(g - 1, 0) % nt, 0))
    full = lambda shape: pl.BlockSpec(shape, lambda g, _nd=len(shape): (0,) * _nd)
    cast_in = [pl.BlockSpec((1, w.shape[1] // ntiles, w.shape[2]), lambda g: (0, jnp.minimum(g, ntiles - 1), 0))
               for w in to_cast]
    cast_out = [pl.BlockSpec((w.shape[1] // ntiles, w.shape[2]), lambda g: (jnp.minimum(g, ntiles - 1), 0))
                for w in to_cast]
    assert all(w.shape[1] % (16 * ntiles) == 0 for w in to_cast)
    return pl.pallas_call(
        functools.partial(_mixer_kernel, tt=tt, nt=nt, n_cast=len(to_cast)),
        out_shape=[jax.ShapeDtypeStruct((bsz, t, d), _F32)]
        + [jax.ShapeDtypeStruct(w.shape[1:], _BF) for w in to_cast],
        grid=(ntiles + 1,),
        in_specs=[in_tile] + [full(p.shape) for p in params] + cast_in,
        out_specs=[out_tile] + cast_out,
        scratch_shapes=[
            pltpu.VMEM((tt + 8, 2 * M_QK), _F32),
            pltpu.VMEM((tt + 8, 2 * M_QK), _F32),
            pltpu.VMEM((M_HEADS, M_DK, M_DV), _F32),
            pltpu.VMEM((M_HEADS, 1, M_DK), _F32),
            pltpu.VMEM((M_HEADS, 1, 1), _F32),
            pltpu.VMEM((G_DV, G_HEADS * G_DK), _F32),
            pltpu.VMEM((2, tt, IN_COLS_R), _F32),
            pltpu.VMEM((2, tt, D_MODEL), _F32),
        ],
        compiler_params=pltpu.CompilerParams(
            dimension_semantics=("arbitrary",), vmem_limit_bytes=VMEM_LIMIT),
        name="mixer",
    )(x, *params, *to_cast)


def _memkv_kernel(mem_ref, wk_ref, wv_ref, wq_ref, wo_ref, wqk_ref, vo_ref):
    m = mem_ref[0]
    k = _dot(m, wk_ref[...])
    v = _dot(m, wv_ref[...])
    for h in range(X_HEADS):
        hs = slice(h * X_DH, (h + 1) * X_DH)
        ms = slice(h * N_MEM, (h + 1) * N_MEM)
        wqk_ref[0, :, ms] = (_dot_nt(wq_ref[:, hs], k[:, hs]) * (X_DH ** -0.5)).astype(_BF)
        vo_ref[0, ms, :] = _dot(v[:, hs], wo_ref[hs, :]).astype(_BF)


def _memkv_call(mem, wk, wv, wq, wo):
    bsz, nm, d = mem.shape
    blk = pl.BlockSpec((1, nm, d), lambda b: (b, 0, 0))
    wspec = pl.BlockSpec((d, d), lambda b: (0, 0))
    return pl.pallas_call(
        _memkv_kernel,
        out_shape=(jax.ShapeDtypeStruct((bsz, d, X_HEADS * nm), _BF),
                   jax.ShapeDtypeStruct((bsz, X_HEADS * nm, d), _BF)),
        grid=(bsz,),
        in_specs=[blk, wspec, wspec, wspec, wspec],
        out_specs=(pl.BlockSpec((1, d, X_HEADS * nm), lambda b: (b, 0, 0)),
                   pl.BlockSpec((1, X_HEADS * nm, d), lambda b: (b, 0, 0))),
        compiler_params=pltpu.CompilerParams(
            dimension_semantics=("arbitrary",), vmem_limit_bytes=VMEM_LIMIT),
        name="memkv",
    )(mem, wk, wv, wq, wo)


def _attnmlp_kernel(h_ref, wqk_ref, vo_ref, ln2_g, ln2_b, w1, w2, ln3_g, ln3_b, o_ref):
    rows = h_ref.shape[1] // ROW_SPLIT

    def chain(r, wait):
        for _ in range(wait):
            yield
        rs = slice(r * rows, (r + 1) * rows)
        h1 = h_ref[0, rs, :]
        s_all = _dot(h1, wqk_ref[0])
        yield
        probs = []
        for h in range(X_HEADS):
            s = s_all[:, h * N_MEM:(h + 1) * N_MEM]
            e = jnp.exp(s - jnp.max(s, axis=1, keepdims=True))
            probs.append((e * (1.0 / jnp.sum(e, axis=1, keepdims=True))).astype(_BF))
            yield
        xa = jnp.dot(jnp.concatenate(probs, axis=1), vo_ref[0], preferred_element_type=_F32)
        yield
        h2 = _layer_norm(ALPHA * h1 + xa, ln2_g[...], ln2_b[...])
        h2b = h2.astype(_BF)
        yield
        ff = jnp.zeros(h2.shape, _F32)
        for f in range(D_FF // FF_BLK):
            fs = slice(f * FF_BLK, (f + 1) * FF_BLK)
            hid = jnp.maximum(jnp.dot(h2b, w1[:, fs], preferred_element_type=_F32), 0.0)
            ff = ff + _dot(hid * hid, w2[fs, :])
            yield
        o_ref[0, rs, :] = _layer_norm(ALPHA * h2 + ff, ln3_g[...], ln3_b[...])

    _round_robin([chain(r, r * ROW_DELAY) for r in range(ROW_SPLIT)])


def _attnmlp_call(h1, wqk, vo, ln2_g, ln2_b, w1, w2, ln3_g, ln3_b, tm):
    bsz, t, d = h1.shape
    tile = pl.BlockSpec((1, tm, d), lambda b, i: (b, i, 0))
    per_batch = lambda a: pl.BlockSpec((1,) + a.shape[1:], lambda b, i: (b, 0, 0))
    params = (ln2_g, ln2_b, w1, w2, ln3_g, ln3_b)
    return pl.pallas_call(
        _attnmlp_kernel,
        out_shape=jax.ShapeDtypeStruct((bsz, t, d), _F32),
        grid=(bsz, t // tm),
        in_specs=[tile, per_batch(wqk), per_batch(vo)] + [_full(p.shape) for p in params],
        out_specs=tile,
        compiler_params=pltpu.CompilerParams(
            dimension_semantics=("arbitrary", "arbitrary"), vmem_limit_bytes=VMEM_LIMIT),
        name="attnmlp",
    )(h1, wqk, vo, *params)


def _time_tile(t, target):
    tt = min(t, target)
    assert t % tt == 0 and tt % WIN == 0, (t, tt)
    return tt


def kernel(x, mem, ln_in_g, ln_in_b, w_in, conv_w, conv_b, m_i_bias, m_f_bias, m_norm_g, g_lr_w, g_lr_b, g_norm_g, w_out, ln1_g, ln1_b, x_wq, x_wk, x_wv, x_wo, ln2_g, ln2_b, w_ff1, w_ff2, ln3_g, ln3_b):
    assert w_in.shape[0] == DEPTH == 1
    row = lambda a: a.reshape(1, -1).astype(_F32)
    wi = w_in[0]
    w_m = wi[:, O_MQ:O_MI].astype(_BF)
    w_g = wi[:, O_GQ:O_GLR].astype(_BF)
    w_gate = jnp.concatenate([wi[:, O_MI:O_GQ], wi[:, O_GLR:O_GLR + G_RANK],
                              jnp.zeros((D_MODEL, LANE - 2 * M_HEADS - G_RANK), wi.dtype)], axis=1).astype(_BF)
    gate_bias = jnp.concatenate([m_i_bias[0], m_f_bias[0],
                                 jnp.zeros((LANE - 2 * M_HEADS,), _F32)]).reshape(1, LANE)
    glw = jnp.zeros((LANE, G_HEADS * G_DK), _F32).at[2 * M_HEADS:2 * M_HEADS + G_RANK].set(g_lr_w[0]).astype(_BF)
    tt = _time_tile(x.shape[1], MIXER_ROWS)
    h1, wk_b, wv_b, wq_b, wo_b, w1_b, w2_b = _mixer_call(
        x, row(ln_in_g), row(ln_in_b), w_m, w_g, w_gate, conv_w[0].astype(_F32), row(conv_b[0]),
        gate_bias, row(m_norm_g[0]), glw, row(g_lr_b[0]), row(g_norm_g[0]), w_out[0].astype(_BF),
        row(ln1_g[0]), row(ln1_b[0]), tt, (x_wk, x_wv, x_wq, x_wo, w_ff1, w_ff2))
    wqk, vo = _memkv_call(mem, wk_b, wv_b, wq_b, wo_b)
    tm = _time_tile(x.shape[1], ATTN_ROWS)
    return _attnmlp_call(h1, wqk, vo, row(ln2_g[0]), row(ln2_b[0]), w1_b, w2_b, row(ln3_g[0]), row(ln3_b[0]), tm)
```

```python
import functools

import numpy as np
import jax
import jax.numpy as jnp
from jax import lax
from jax.experimental import pallas as pl
from jax.experimental.pallas import tpu as pltpu

D_MODEL = 1024
CHUNK = 64
SUB = 16
WIN = 2 * CHUNK
N_MEM = 256
M_HEADS, M_DK, M_DV = 4, 128, 128
M_QK = M_HEADS * M_DK
G_HEADS, G_DK, G_DV = 4, 64, 128
G_RANK = 16
G_TAU = 16.0
X_HEADS = 4
X_DH = D_MODEL // X_HEADS
D_FF = 4 * D_MODEL
MIXER_ROWS = 256
ATTN_ROWS = 1024
FF_BLK = 512
IN_BLK = 512
ROW_SPLIT = 2
ROW_DELAY = 6
DEPTH = 1
ALPHA = (2.0 * DEPTH) ** 0.25
LN_EPS = 1e-5
LOG2E = 1.4426950408889634
LANE = 128

C_MQ, C_MK, C_MV, C_MO = 0, 512, 1024, 1536
C_GQ, C_GK, C_GV, C_GG = 2048, 2304, 2560, 3072
C_GATE = 3584
IN_COLS_R = C_GATE + LANE
O_MQ, O_MK, O_MV, O_MO, O_MI, O_MF = 0, 512, 1024, 1536, 2048, 2052
O_GQ, O_GK, O_GV, O_GG, O_GLR = 2056, 2312, 2568, 3080, 3592

VMEM_LIMIT = 56 * 1024 * 1024

_BF = jnp.bfloat16
_F32 = jnp.float32


def _dot(a, b):
    return jnp.dot(a.astype(_BF), b.astype(_BF), preferred_element_type=_F32)


def _dot_nt(a, b):
    return lax.dot_general(a.astype(_BF), b.astype(_BF), (((1,), (1,)), ((), ())),
                           preferred_element_type=_F32)


def _dot_tn(a, b):
    return lax.dot_general(a.astype(_BF), b.astype(_BF), (((0,), (0,)), ((), ())),
                           preferred_element_type=_F32)


def _split2(x):
    hi = x.astype(_BF)
    lo = (x - hi.astype(_F32)).astype(_BF)
    return jnp.concatenate([hi, lo], axis=1)


def _merge2(y, n):
    return y[:, 0:n] + y[:, n:2 * n]


def _layer_norm(x, g, b):
    mu = jnp.mean(x, axis=-1, keepdims=True)
    xc = x - mu
    var = jnp.mean(xc * xc, axis=-1, keepdims=True)
    return xc * lax.rsqrt(var + LN_EPS) * g + b


def _log_sigmoid(x):
    return -(jnp.maximum(-x, 0.0) + jnp.log(1.0 + jnp.exp(-jnp.abs(x))))


def _sigmoid(x):
    return 1.0 / (1.0 + jnp.exp(-x))


def _round_robin(chains):
    while chains:
        alive = []
        for ch in chains:
            try:
                next(ch)
                alive.append(ch)
            except StopIteration:
                pass
        chains = alive


def _mixer_kernel(x_ref, lnin_g, lnin_b, w_m, w_g, w_gate, conv_w, conv_b, gate_bias, m_norm_g, glw, glb, g_norm_g,
                  w_out, ln1_g, ln1_b, tri64, tri16, ones16, gsel, rsel, *rest, tt, nt, n_cast):
    cast_in, rest = rest[:n_cast], rest[n_cast:]
    o_ref, cast_out = rest[0], rest[1:1 + n_cast]
    uext, eext, c_st, n_st, m_st, s_st, proj_buf, h0_buf = rest[1 + n_cast:]
    g = pl.program_id(0)

    @pl.when(g == 0)
    def _():
        proj_buf[1] = jnp.zeros(proj_buf.shape[1:], _F32)
        h0_buf[1] = jnp.zeros(h0_buf.shape[1:], _F32)

    @pl.when((g == 0) | ((g - 1) % nt == 0))
    def _():
        uext[0:8, :] = jnp.zeros((8, 2 * M_QK), _F32)
        eext[0:8, :] = jnp.zeros((8, 2 * M_QK), _F32)
        c_st[...] = jnp.zeros(c_st.shape, _F32)
        n_st[...] = jnp.zeros(n_st.shape, _F32)
        m_st[...] = jnp.zeros(m_st.shape, _F32)
        s_st[...] = jnp.zeros(s_st.shape, _F32)

    def step(slot_a):
        slot_b = 1 - slot_a

        def in_proj():
            h0n = _layer_norm(x_ref[0], lnin_g[...], lnin_b[...])
            h0_buf[slot_a] = h0n
            h0b = h0n.astype(_BF)
            yield
            for w_ref, base in ((w_m, C_MQ), (w_g, C_GQ), (w_gate, C_GATE)):
                ncol = w_ref.shape[1]
                for c0 in range(0, ncol, IN_BLK):
                    c1 = min(c0 + IN_BLK, ncol)
                    proj_buf[slot_a, :, base + c0:base + c1] = jnp.dot(h0b, w_ref[:, c0:c1],
                                                                       preferred_element_type=_F32)
                    yield

        pb = proj_buf.at[slot_b]
        h0 = h0_buf[slot_b]

        uext[8:8 + tt, :] = pb[:, C_MQ:C_MV]
        cw = conv_w[...]
        u0 = uext[8:8 + tt, :]
        u1 = uext[7:7 + tt, :]
        eext[8:8 + tt, :] = u0 * cw[1:2, :] + u1 * cw[0:1, :]
        conv = u0 * cw[3:4, :] + u1 * cw[2:3, :] + eext[6:6 + tt, :] + conv_b[...]
        uext[0:8, :] = uext[tt:tt + 8, :]
        eext[0:8, :] = eext[tt:tt + 8, :]
        qk = conv * _sigmoid(conv)
        mq = qk[:, 0:M_QK]
        mk = qk[:, M_QK:2 * M_QK] * (M_DK ** -0.5)
        mv = pb[:, C_MV:C_MO]
        mo = pb[:, C_MO:C_GQ]

        gb = pb[:, C_GATE:C_GATE + LANE] + gate_bias[...]
        lane = lax.broadcasted_iota(jnp.int32, (tt, LANE), 1)
        lf = _log_sigmoid(gb)
        bcum = _merge2(_dot(tri64[...], _split2(lf)), LANE)
        vcol = jnp.where((lane >= M_HEADS) & (lane < 2 * M_HEADS), bcum, gb) * LOG2E
        vrow = vcol.T

        gw = G_HEADS * G_DK
        la = _log_sigmoid(_dot(gb, glw[...]) + glb[...]) * (1.0 / G_TAU)
        la2 = _split2(la)
        bc = _merge2(_dot(tri16[...], la2), gw)
        bend = _merge2(_dot(ones16[...], la2), gw)
        gq = pb[:, C_GQ:C_GK] * (G_DK ** -0.5)
        gk = pb[:, C_GK:C_GV]
        gv = pb[:, C_GV:C_GG]
        gg = pb[:, C_GG:C_GATE]
        bc2 = bc * LOG2E
        bend2 = bend * LOG2E
        qh = gq * jnp.exp2(bc2)
        kh = gk * jnp.exp2(bend2 - bc2)
        eg = jnp.exp2(bend2)
        nsub = tt // SUB

        rr = lax.broadcasted_iota(jnp.int32, (WIN, WIN), 0)
        cc = lax.broadcasted_iota(jnp.int32, (WIN, WIN), 1)
        valid = (cc <= rr) & ((cc >= CHUNK) == (rr >= CHUNK))
        first = lax.broadcasted_iota(jnp.int32, (WIN, 1), 0) < CHUNK
        mg = m_norm_g[...]
        m_outs = [None] * M_HEADS

        def mlstm_head(h):
            hs = slice(h * M_DK, (h + 1) * M_DK)
            c_cur = c_st[h]
            n_cur = n_st[h]
            m_cur = m_st[h]
            win_outs = []
            for w in range(tt // WIN):
                rs = slice(w * WIN, (w + 1) * WIN)
                q, k, v = mq[rs, hs], mk[rs, hs], mv[rs, hs]
                li_col, b_col = vcol[rs, h:h + 1], vcol[rs, M_HEADS + h:M_HEADS + h + 1]
                li_row, b_row = vrow[h:h + 1, rs], vrow[M_HEADS + h:M_HEADS + h + 1, rs]
                dm = jnp.where(valid, b_col - b_row + li_row, -jnp.inf)
                md = jnp.max(dm, axis=1, keepdims=True)
                g0, g1 = b_col[CHUNK - 1:CHUNK, :], b_col[WIN - 1:WIN, :]
                a_col = jnp.where(first, g0, g1) - b_col + li_col
                yield
                am0 = jnp.max(a_col[0:CHUNK], axis=0, keepdims=True)
                am1 = jnp.max(a_col[CHUNK:WIN], axis=0, keepdims=True)
                wk = jnp.exp2(a_col - jnp.where(first, am0, am1)) * k
                s_raw = _dot_nt(q, k)
                yield
                cprev, nprev, mprev = [], [], []
                for c, (gc, am) in enumerate(((g0, am0), (g1, am1))):
                    cs = slice(c * CHUNK, (c + 1) * CHUNK)
                    cprev.append(c_cur)
                    nprev.append(n_cur)
                    mprev.append(m_cur)
                    u_mat = _dot_tn(wk[cs], v[cs])
                    u_vec = jnp.sum(wk[cs], axis=0, keepdims=True)
                    m_new = jnp.maximum(gc + m_cur, am)
                    dec = jnp.exp2(gc + m_cur - m_new)
                    inj = jnp.exp2(am - m_new)
                    c_cur = dec * c_cur + inj * u_mat
                    n_cur = dec * n_cur + inj * u_vec
                    m_cur = m_new
                yield
                inter = b_col + jnp.where(first, mprev[0], mprev[1])
                m_t = jnp.maximum(inter, md)
                sc = jnp.exp2(inter - m_t)
                s_mat = s_raw * jnp.exp2(dm - m_t)
                yield
                sv = _dot(s_mat, v)
                qc = jnp.concatenate([_dot(q[0:CHUNK], cprev[0]), _dot(q[CHUNK:WIN], cprev[1])], axis=0)
                qn = jnp.sum(q * jnp.where(first, nprev[0], nprev[1]), axis=1, keepdims=True)
                ssum = jnp.sum(s_mat, axis=1, keepdims=True)
                yield
                num = sc * qc + sv
                den = sc * qn + ssum
                hh = num * (1.0 / jnp.maximum(jnp.abs(den), jnp.exp2(-m_t)))
                mu = jnp.mean(hh, axis=1, keepdims=True)
                yield
                hc = hh - mu
                var = jnp.mean(hc * hc, axis=1, keepdims=True)
                yield
                win_outs.append(hc * lax.rsqrt(var + LN_EPS))
            c_st[h] = c_cur
            n_st[h] = n_cur
            m_st[h] = m_cur
            hn = win_outs[0] if len(win_outs) == 1 else jnp.concatenate(win_outs, axis=0)
            m_outs[h] = _sigmoid(mo[:, hs]) * (hn * mg[:, hs])

        o_diag = [None] * G_HEADS

        def gla_diag():
            gk3 = gk.reshape(nsub, SUB, gw)
            bc3 = bc2.reshape(nsub, SUB, gw)
            half = SUB // 2
            hi = lambda a: a.reshape(nsub, 2, half, a.shape[-1])[:, 1].reshape(nsub * half, a.shape[-1])
            gq_hi, bc2_hi = hi(gq), hi(bc2)
            p_acc = jnp.zeros((tt, LANE), _F32)
            p_hi = jnp.zeros((tt // 2, LANE), _F32)
            for s in range(SUB):
                if s < half:
                    kb = jnp.broadcast_to(gk3[:, s:s + 1, :], (nsub, SUB, gw)).reshape(tt, gw)
                    bb = jnp.broadcast_to(bc3[:, s:s + 1, :], (nsub, SUB, gw)).reshape(tt, gw)
                    y = gq * kb * jnp.exp2(jnp.minimum(bc2 - bb, 0.0))
                    p_acc = p_acc + _dot(y, gsel[s])
                else:
                    kb = jnp.broadcast_to(gk3[:, s:s + 1, :], (nsub, half, gw)).reshape(tt // 2, gw)
                    bb = jnp.broadcast_to(bc3[:, s:s + 1, :], (nsub, half, gw)).reshape(tt // 2, gw)
                    y = gq_hi * kb * jnp.exp2(jnp.minimum(bc2_hi - bb, 0.0))
                    p_hi = p_hi + _dot(y, gsel[s])
                yield
            p4 = p_acc.reshape(nsub, 2, half, LANE)
            p_acc = jnp.concatenate([p4[:, 0:1], p4[:, 1:2] + p_hi.reshape(nsub, 1, half, LANE)],
                                    axis=1).reshape(tt, LANE)
            p_bf = p_acc.astype(_BF)
            r2 = lax.broadcasted_iota(jnp.int32, (WIN, WIN), 0)
            c2 = lax.broadcasted_iota(jnp.int32, (WIN, WIN), 1)
            sub_mask = (c2 <= r2) & ((c2 // SUB) == (r2 // SUB))
            for h in range(G_HEADS):
                outs = []
                for w in range(tt // WIN):
                    rs = slice(w * WIN, (w + 1) * WIN)
                    a_full = jnp.dot(p_bf[rs], rsel[h], preferred_element_type=_F32)
                    a_h = jnp.where(sub_mask, a_full, 0.0)
                    outs.append(_dot(a_h, gv[rs, h * G_DV:(h + 1) * G_DV]))
                o_diag[h] = outs[0] if len(outs) == 1 else jnp.concatenate(outs, axis=0)
                yield

        head_of_lane = lax.broadcasted_iota(jnp.int32, (SUB, gw), 1) // G_DK
        o_int = [[None] * nsub for _ in range(G_HEADS)]

        def gla_rec():
            st = s_st[...]
            for j in range(nsub):
                js = slice(j * SUB, (j + 1) * SUB)
                qj, kj = qh[js], kh[js]
                qq = jnp.concatenate([jnp.where(head_of_lane == h, qj, 0.0) for h in range(G_HEADS)], axis=0)
                kk = jnp.concatenate([jnp.where(head_of_lane == h, kj, 0.0) for h in range(G_HEADS)], axis=0)
                vv = jnp.concatenate([gv[js, h * G_DV:(h + 1) * G_DV] for h in range(G_HEADS)], axis=0)
                oj = _dot_nt(qq, st)
                for h in range(G_HEADS):
                    o_int[h][j] = oj[h * SUB:(h + 1) * SUB]
                st = eg[j * SUB:j * SUB + 1, :] * st + _dot_tn(vv, kk)
                yield
            s_st[...] = st

        def cast_weights():
            for src_ref, dst_ref in zip(cast_in, cast_out):
                dst_ref[...] = src_ref[0].astype(_BF)
                yield

        _round_robin([in_proj(), gla_diag(), gla_rec()] + [mlstm_head(h) for h in range(M_HEADS)]
                     + [cast_weights()])

        gn = g_norm_g[...]
        g_outs = []
        for h in range(G_HEADS):
            hs = slice(h * G_DV, (h + 1) * G_DV)
            og = o_diag[h] + jnp.concatenate(o_int[h], axis=0)
            rms = lax.rsqrt(jnp.mean(og * og, axis=1, keepdims=True) + LN_EPS)
            gate = gg[:, hs]
            g_outs.append(gate * _sigmoid(gate) * (og * rms * gn[:, hs]))

        y = jnp.concatenate(m_outs + g_outs, axis=1)
        mix = _dot(y, w_out[...])
        o_ref[0] = _layer_norm(ALPHA * h0 + mix, ln1_g[...], ln1_b[...])

    for par in range(2):
        pl.when(g % 2 == par)(functools.partial(step, par))


def _mixer_constants(tt):
    r = np.arange(tt)
    tri64 = ((r[None, :] <= r[:, None]) & (r[None, :] // CHUNK == r[:, None] // CHUNK))
    same16 = (r[None, :] // SUB == r[:, None] // SUB)
    tri16 = (r[None, :] <= r[:, None]) & same16
    gw = G_HEADS * G_DK
    gsel = np.zeros((SUB, gw, LANE), np.float32)
    for s in range(SUB):
        for h in range(G_HEADS):
            gsel[s, h * G_DK:(h + 1) * G_DK, h * SUB + s] = 1.0
    rsel = np.zeros((G_HEADS, LANE, WIN), np.float32)
    for h in range(G_HEADS):
        for s in range(SUB):
            rsel[h, h * SUB + s, s::SUB] = 1.0
    to_bf = lambda a: jnp.asarray(a.astype(np.float32), dtype=_BF)
    return to_bf(tri64), to_bf(tri16), to_bf(same16), to_bf(gsel), to_bf(rsel)


def _full(shape):
    nd = len(shape)
    return pl.BlockSpec(shape, lambda b, i, _nd=nd: (0,) * _nd)


def _mixer_call(x, lnin_g, lnin_b, w_m, w_g, w_gate, conv_w, conv_b, gate_bias, m_norm_g, glw, glb, g_norm_g,
                w_out, ln1_g, ln1_b, tt, to_cast):
    bsz, t, d = x.shape
    consts = _mixer_constants(tt)
    params = (lnin_g, lnin_b, w_m, w_g, w_gate, conv_w, conv_b, gate_bias, m_norm_g, glw, glb, g_norm_g,
              w_out, ln1_g, ln1_b) + consts
    nt = t // tt
    ntiles = bsz * nt
    in_tile = pl.BlockSpec((1, tt, d), lambda g: (jnp.minimum(g, ntiles - 1) // nt, jnp.minimum(g, ntiles - 1) % nt, 0))
    out_tile = pl.BlockSpec((1, tt, d), lambda g: (jnp.maximum(g - 1, 0) // nt, jnp.maximum(g - 1, 0) % nt, 0))
    full = lambda shape: pl.BlockSpec(shape, lambda g, _nd=len(shape): (0,) * _nd)
    cast_in = [pl.BlockSpec((1, w.shape[1] // ntiles, w.shape[2]), lambda g: (0, jnp.minimum(g, ntiles - 1), 0))
               for w in to_cast]
    cast_out = [pl.BlockSpec((w.shape[1] // ntiles, w.shape[2]), lambda g: (jnp.minimum(g, ntiles - 1), 0))
                for w in to_cast]
    assert all(w.shape[1] % (16 * ntiles) == 0 for w in to_cast)
    return pl.pallas_call(
        functools.partial(_mixer_kernel, tt=tt, nt=nt, n_cast=len(to_cast)),
        out_shape=[jax.ShapeDtypeStruct((bsz, t, d), _F32)]
        + [jax.ShapeDtypeStruct(w.shape[1:], _BF) for w in to_cast],
        grid=(ntiles + 1,),
        in_specs=[in_tile] + [full(p.shape) for p in params] + cast_in,
        out_specs=[out_tile] + cast_out,
        scratch_shapes=[
            pltpu.VMEM((tt + 8, 2 * M_QK), _F32),
            pltpu.VMEM((tt + 8, 2 * M_QK), _F32),
            pltpu.VMEM((M_HEADS, M_DK, M_DV), _F32),
            pltpu.VMEM((M_HEADS, 1, M_DK), _F32),
            pltpu.VMEM((M_HEADS, 1, 1), _F32),
            pltpu.VMEM((G_DV, G_HEADS * G_DK), _F32),
            pltpu.VMEM((2, tt, IN_COLS_R), _F32),
            pltpu.VMEM((2, tt, D_MODEL), _F32),
        ],
        compiler_params=pltpu.CompilerParams(
            dimension_semantics=("arbitrary",), vmem_limit_bytes=VMEM_LIMIT),
        name="mixer",
    )(x, *params, *to_cast)


def _memkv_kernel(mem_ref, wk_ref, wv_ref, wq_ref, wo_ref, wqk_ref, vo_ref, k_sc, v_sc):
    h = pl.program_id(1)

    @pl.when(h == 0)
    def _():
        m = mem_ref[0]
        k = _dot(m, wk_ref[...])
        v = _dot(m, wv_ref[...])
        for hh in range(X_HEADS):
            hs = slice(hh * X_DH, (hh + 1) * X_DH)
            k_sc[hh] = k[:, hs].astype(_BF)
            v_sc[hh] = v[:, hs].astype(_BF)

    wqk_ref[0] = (_dot_nt(wq_ref[...], k_sc[h]) * (X_DH ** -0.5)).astype(_BF)
    vo_ref[0] = _dot(v_sc[h], wo_ref[...]).astype(_BF)


def _memkv_call(mem, wk, wv, wq, wo):
    bsz, nm, d = mem.shape
    blk = pl.BlockSpec((1, nm, d), lambda b, h: (b, 0, 0))
    wspec = pl.BlockSpec((d, d), lambda b, h: (0, 0))
    return pl.pallas_call(
        _memkv_kernel,
        out_shape=(jax.ShapeDtypeStruct((bsz, d, X_HEADS * nm), _BF),
                   jax.ShapeDtypeStruct((bsz, X_HEADS * nm, d), _BF)),
        grid=(bsz, X_HEADS),
        in_specs=[blk, wspec, wspec,
                  pl.BlockSpec((d, X_DH), lambda b, h: (0, h)), pl.BlockSpec((X_DH, d), lambda b, h: (h, 0))],
        out_specs=(pl.BlockSpec((1, d, nm), lambda b, h: (b, 0, h)),
                   pl.BlockSpec((1, nm, d), lambda b, h: (b, h, 0))),
        scratch_shapes=[pltpu.VMEM((X_HEADS, nm, X_DH), _BF), pltpu.VMEM((X_HEADS, nm, X_DH), _BF)],
        compiler_params=pltpu.CompilerParams(
            dimension_semantics=("arbitrary", "arbitrary"), vmem_limit_bytes=VMEM_LIMIT),
        name="memkv",
    )(mem, wk, wv, wq, wo)


def _attnmlp_kernel(h_ref, wqk_ref, vo_ref, ln2_g, ln2_b, w1, w2, ln3_g, ln3_b, o_ref):
    rows = h_ref.shape[1] // ROW_SPLIT

    def chain(r, wait):
        for _ in range(wait):
            yield
        rs = slice(r * rows, (r + 1) * rows)
        h1 = h_ref[0, rs, :]
        s_all = _dot(h1, wqk_ref[0])
        yield
        probs = []
        for h in range(X_HEADS):
            s = s_all[:, h * N_MEM:(h + 1) * N_MEM]
            e = jnp.exp(s - jnp.max(s, axis=1, keepdims=True))
            probs.append((e * (1.0 / jnp.sum(e, axis=1, keepdims=True))).astype(_BF))
            yield
        xa = jnp.dot(jnp.concatenate(probs, axis=1), vo_ref[0], preferred_element_type=_F32)
        yield
        h2 = _layer_norm(ALPHA * h1 + xa, ln2_g[...], ln2_b[...])
        h2b = h2.astype(_BF)
        yield
        ff = jnp.zeros(h2.shape, _F32)
        for f in range(D_FF // FF_BLK):
            fs = slice(f * FF_BLK, (f + 1) * FF_BLK)
            hid = jnp.maximum(jnp.dot(h2b, w1[:, fs], preferred_element_type=_F32), 0.0)
            ff = ff + _dot(hid * hid, w2[fs, :])
            yield
        o_ref[0, rs, :] = _layer_norm(ALPHA * h2 + ff, ln3_g[...], ln3_b[...])

    _round_robin([chain(r, r * ROW_DELAY) for r in range(ROW_SPLIT)])


def _attnmlp_call(h1, wqk, vo, ln2_g, ln2_b, w1, w2, ln3_g, ln3_b, tm):
    bsz, t, d = h1.shape
    tile = pl.BlockSpec((1, tm, d), lambda b, i: (b, i, 0))
    per_batch = lambda a: pl.BlockSpec((1,) + a.shape[1:], lambda b, i: (b, 0, 0))
    params = (ln2_g, ln2_b, w1, w2, ln3_g, ln3_b)
    return pl.pallas_call(
        _attnmlp_kernel,
        out_shape=jax.ShapeDtypeStruct((bsz, t, d), _F32),
        grid=(bsz, t // tm),
        in_specs=[tile, per_batch(wqk), per_batch(vo)] + [_full(p.shape) for p in params],
        out_specs=tile,
        compiler_params=pltpu.CompilerParams(
            dimension_semantics=("arbitrary", "arbitrary"), vmem_limit_bytes=VMEM_LIMIT),
        name="attnmlp",
    )(h1, wqk, vo, *params)


def _time_tile(t, target):
    tt = min(t, target)
    assert t % tt == 0 and tt % WIN == 0, (t, tt)
    return tt


def kernel(x, mem, ln_in_g, ln_in_b, w_in, conv_w, conv_b, m_i_bias, m_f_bias, m_norm_g, g_lr_w, g_lr_b, g_norm_g, w_out, ln1_g, ln1_b, x_wq, x_wk, x_wv, x_wo, ln2_g, ln2_b, w_ff1, w_ff2, ln3_g, ln3_b):
    assert w_in.shape[0] == DEPTH == 1
    row = lambda a: a.reshape(1, -1).astype(_F32)
    wi = w_in[0]
    w_m = wi[:, O_MQ:O_MI].astype(_BF)
    w_g = wi[:, O_GQ:O_GLR].astype(_BF)
    w_gate = jnp.concatenate([wi[:, O_MI:O_GQ], wi[:, O_GLR:O_GLR + G_RANK],
                              jnp.zeros((D_MODEL, LANE - 2 * M_HEADS - G_RANK), wi.dtype)], axis=1).astype(_BF)
    gate_bias = jnp.concatenate([m_i_bias[0], m_f_bias[0],
                                 jnp.zeros((LANE - 2 * M_HEADS,), _F32)]).reshape(1, LANE)
    glw = jnp.zeros((LANE, G_HEADS * G_DK), _F32).at[2 * M_HEADS:2 * M_HEADS + G_RANK].set(g_lr_w[0]).astype(_BF)
    tt = _time_tile(x.shape[1], MIXER_ROWS)
    h1, wk_b, wv_b, wq_b, wo_b, w1_b, w2_b = _mixer_call(
        x, row(ln_in_g), row(ln_in_b), w_m, w_g, w_gate, conv_w[0].astype(_F32), row(conv_b[0]),
        gate_bias, row(m_norm_g[0]), glw, row(g_lr_b[0]), row(g_norm_g[0]), w_out[0].astype(_BF),
        row(ln1_g[0]), row(ln1_b[0]), tt, (x_wk, x_wv, x_wq, x_wo, w_ff1, w_ff2))
    wqk, vo = _memkv_call(mem, wk_b, wv_b, wq_b, wo_b)
    tm = _time_tile(x.shape[1], ATTN_ROWS)
    return _attnmlp_call(h1, wqk, vo, row(ln2_g[0]), row(ln2_b[0]), w1_b, w2_b, row(ln3_g[0]), row(ln3_b[0]), tm)
```

```python
import functools

import numpy as np
import jax
import jax.numpy as jnp
from jax import lax
from jax.experimental import pallas as pl
from jax.experimental.pallas import tpu as pltpu

D_MODEL = 1024
CHUNK = 64
SUB = 16
WIN = 2 * CHUNK
N_MEM = 256
M_HEADS, M_DK, M_DV = 4, 128, 128
M_QK = M_HEADS * M_DK
G_HEADS, G_DK, G_DV = 4, 64, 128
G_RANK = 16
G_TAU = 16.0
X_HEADS = 4
X_DH = D_MODEL // X_HEADS
D_FF = 4 * D_MODEL
MIXER_ROWS = 256
ATTN_ROWS = 1024
FF_BLK = 512
IN_BLK = 512
ROW_SPLIT = 2
ROW_DELAY = 6
DEPTH = 1
ALPHA = (2.0 * DEPTH) ** 0.25
LN_EPS = 1e-5
LOG2E = 1.4426950408889634
LANE = 128

C_MQ, C_MK, C_MV, C_MO = 0, 512, 1024, 1536
C_GQ, C_GK, C_GV, C_GG = 2048, 2304, 2560, 3072
C_GATE = 3584
IN_COLS_R = C_GATE + LANE
O_MQ, O_MK, O_MV, O_MO, O_MI, O_MF = 0, 512, 1024, 1536, 2048, 2052
O_GQ, O_GK, O_GV, O_GG, O_GLR = 2056, 2312, 2568, 3080, 3592

VMEM_LIMIT = 56 * 1024 * 1024

_BF = jnp.bfloat16
_F32 = jnp.float32


def _dot(a, b):
    return jnp.dot(a.astype(_BF), b.astype(_BF), preferred_element_type=_F32)


def _dot_nt(a, b):
    return lax.dot_general(a.astype(_BF), b.astype(_BF), (((1,), (1,)), ((), ())),
                           preferred_element_type=_F32)


def _dot_tn(a, b):
    return lax.dot_general(a.astype(_BF), b.astype(_BF), (((0,), (0,)), ((), ())),
                           preferred_element_type=_F32)


def _split2(x):
    hi = x.astype(_BF)
    lo = (x - hi.astype(_F32)).astype(_BF)
    return jnp.concatenate([hi, lo], axis=1)


def _merge2(y, n):
    return y[:, 0:n] + y[:, n:2 * n]


def _layer_norm(x, g, b):
    mu = jnp.mean(x, axis=-1, keepdims=True)
    xc = x - mu
    var = jnp.mean(xc * xc, axis=-1, keepdims=True)
    return xc * lax.rsqrt(var + LN_EPS) * g + b


def _log_sigmoid(x):
    return -(jnp.maximum(-x, 0.0) + jnp.log(1.0 + jnp.exp(-jnp.abs(x))))


def _sigmoid(x):
    return 1.0 / (1.0 + jnp.exp(-x))


def _round_robin(chains):
    while chains:
        alive = []
        for ch in chains:
            try:
                next(ch)
                alive.append(ch)
            except StopIteration:
                pass
        chains = alive


def _mixer_kernel(x_ref, lnin_g, lnin_b, w_m, w_g, w_gate, conv_w, conv_b, gate_bias, m_norm_g, glw, glb, g_norm_g,
                  w_out, ln1_g, ln1_b, tri64, tri16, ones16, gsel, rsel, *rest, tt, nt, n_cast):
    cast_in, rest = rest[:n_cast], rest[n_cast:]
    o_ref, cast_out = rest[0], rest[1:1 + n_cast]
    uext, eext, c_st, n_st, m_st, s_st, proj_buf, h0_buf = rest[1 + n_cast:]
    g = pl.program_id(0)

    @pl.when(g == 0)
    def _():
        proj_buf[1] = jnp.zeros(proj_buf.shape[1:], _F32)
        h0_buf[1] = jnp.zeros(h0_buf.shape[1:], _F32)

    @pl.when((g == 0) | ((g - 1) % nt == 0))
    def _():
        uext[0:8, :] = jnp.zeros((8, 2 * M_QK), _F32)
        eext[0:8, :] = jnp.zeros((8, 2 * M_QK), _F32)
        c_st[...] = jnp.zeros(c_st.shape, _F32)
        n_st[...] = jnp.zeros(n_st.shape, _F32)
        m_st[...] = jnp.zeros(m_st.shape, _F32)
        s_st[...] = jnp.zeros(s_st.shape, _F32)

    def step(slot_a):
        slot_b = 1 - slot_a

        def in_proj():
            h0n = _layer_norm(x_ref[0], lnin_g[...], lnin_b[...])
            h0_buf[slot_a] = h0n
            h0b = h0n.astype(_BF)
            yield
            for w_ref, base in ((w_m, C_MQ), (w_g, C_GQ), (w_gate, C_GATE)):
                ncol = w_ref.shape[1]
                for c0 in range(0, ncol, IN_BLK):
                    c1 = min(c0 + IN_BLK, ncol)
                    proj_buf[slot_a, :, base + c0:base + c1] = jnp.dot(h0b, w_ref[:, c0:c1],
                                                                       preferred_element_type=_F32)
                    yield

        pb = proj_buf.at[slot_b]
        h0 = h0_buf[slot_b]

        uext[8:8 + tt, :] = pb[:, C_MQ:C_MV]
        cw = conv_w[...]
        u0 = uext[8:8 + tt, :]
        u1 = uext[7:7 + tt, :]
        eext[8:8 + tt, :] = u0 * cw[1:2, :] + u1 * cw[0:1, :]
        conv = u0 * cw[3:4, :] + u1 * cw[2:3, :] + eext[6:6 + tt, :] + conv_b[...]
        uext[0:8, :] = uext[tt:tt + 8, :]
        eext[0:8, :] = eext[tt:tt + 8, :]
        qk = conv * _sigmoid(conv)
        mq = qk[:, 0:M_QK]
        mk = qk[:, M_QK:2 * M_QK] * (M_DK ** -0.5)
        mv = pb[:, C_MV:C_MO]
        mo = pb[:, C_MO:C_GQ]

        gb = pb[:, C_GATE:C_GATE + LANE] + gate_bias[...]
        lane = lax.broadcasted_iota(jnp.int32, (tt, LANE), 1)
        lf = _log_sigmoid(gb)
        bcum = _merge2(_dot(tri64[...], _split2(lf)), LANE)
        vcol = jnp.where((lane >= M_HEADS) & (lane < 2 * M_HEADS), bcum, gb) * LOG2E
        vrow = vcol.T

        gw = G_HEADS * G_DK
        la = _log_sigmoid(_dot(gb, glw[...]) + glb[...]) * (1.0 / G_TAU)
        la2 = _split2(la)
        bc = _merge2(_dot(tri16[...], la2), gw)
        bend = _merge2(_dot(ones16[...], la2), gw)
        gq = pb[:, C_GQ:C_GK] * (G_DK ** -0.5)
        gk = pb[:, C_GK:C_GV]
        gv = pb[:, C_GV:C_GG]
        gg = pb[:, C_GG:C_GATE]
        bc2 = bc * LOG2E
        bend2 = bend * LOG2E
        qh = gq * jnp.exp2(bc2)
        kh = gk * jnp.exp2(bend2 - bc2)
        eg = jnp.exp2(bend2)
        nsub = tt // SUB

        rr = lax.broadcasted_iota(jnp.int32, (WIN, WIN), 0)
        cc = lax.broadcasted_iota(jnp.int32, (WIN, WIN), 1)
        valid = (cc <= rr) & ((cc >= CHUNK) == (rr >= CHUNK))
        first = lax.broadcasted_iota(jnp.int32, (WIN, 1), 0) < CHUNK
        mg = m_norm_g[...]
        m_outs = [None] * M_HEADS

        def mlstm_head(h):
            hs = slice(h * M_DK, (h + 1) * M_DK)
            c_cur = c_st[h]
            n_cur = n_st[h]
            m_cur = m_st[h]
            win_outs = []
            for w in range(tt // WIN):
                rs = slice(w * WIN, (w + 1) * WIN)
                q, k, v = mq[rs, hs], mk[rs, hs], mv[rs, hs]
                li_col, b_col = vcol[rs, h:h + 1], vcol[rs, M_HEADS + h:M_HEADS + h + 1]
                li_row, b_row = vrow[h:h + 1, rs], vrow[M_HEADS + h:M_HEADS + h + 1, rs]
                dm = jnp.where(valid, b_col - b_row + li_row, -jnp.inf)
                md = jnp.max(dm, axis=1, keepdims=True)
                g0, g1 = b_col[CHUNK - 1:CHUNK, :], b_col[WIN - 1:WIN, :]
                a_col = jnp.where(first, g0, g1) - b_col + li_col
                yield
                am0 = jnp.max(a_col[0:CHUNK], axis=0, keepdims=True)
                am1 = jnp.max(a_col[CHUNK:WIN], axis=0, keepdims=True)
                wk = jnp.exp2(a_col - jnp.where(first, am0, am1)) * k
                s_raw = _dot_nt(q, k)
                yield
                cprev, nprev, mprev = [], [], []
                for c, (gc, am) in enumerate(((g0, am0), (g1, am1))):
                    cs = slice(c * CHUNK, (c + 1) * CHUNK)
                    cprev.append(c_cur)
                    nprev.append(n_cur)
                    mprev.append(m_cur)
                    u_mat = _dot_tn(wk[cs], v[cs])
                    u_vec = jnp.sum(wk[cs], axis=0, keepdims=True)
                    m_new = jnp.maximum(gc + m_cur, am)
                    dec = jnp.exp2(gc + m_cur - m_new)
                    inj = jnp.exp2(am - m_new)
                    c_cur = dec * c_cur + inj * u_mat
                    n_cur = dec * n_cur + inj * u_vec
                    m_cur = m_new
                yield
                inter = b_col + jnp.where(first, mprev[0], mprev[1])
                m_t = jnp.maximum(inter, md)
                sc = jnp.exp2(inter - m_t)
                s_mat = s_raw * jnp.exp2(dm - m_t)
                yield
                sv = _dot(s_mat, v)
                qc = jnp.concatenate([_dot(q[0:CHUNK], cprev[0]), _dot(q[CHUNK:WIN], cprev[1])], axis=0)
                qn = jnp.sum(q * jnp.where(first, nprev[0], nprev[1]), axis=1, keepdims=True)
                ssum = jnp.sum(s_mat, axis=1, keepdims=True)
                yield
                num = sc * qc + sv
                den = sc * qn + ssum
                hh = num * (1.0 / jnp.maximum(jnp.abs(den), jnp.exp2(-m_t)))
                mu = jnp.mean(hh, axis=1, keepdims=True)
                yield
                hc = hh - mu
                var = jnp.mean(hc * hc, axis=1, keepdims=True)
                yield
                win_outs.append(hc * lax.rsqrt(var + LN_EPS))
            c_st[h] = c_cur
            n_st[h] = n_cur
            m_st[h] = m_cur
            hn = win_outs[0] if len(win_outs) == 1 else jnp.concatenate(win_outs, axis=0)
            m_outs[h] = _sigmoid(mo[:, hs]) * (hn * mg[:, hs])

        o_diag = [None] * G_HEADS

        def gla_diag():
            gk3 = gk.reshape(nsub, SUB, gw)
            bc3 = bc2.reshape(nsub, SUB, gw)
            half = SUB // 2
            hi = lambda a: a.reshape(nsub, 2, half, a.shape[-1])[:, 1].reshape(nsub * half, a.shape[-1])
            gq_hi, bc2_hi = hi(gq), hi(bc2)
            p_acc = jnp.zeros((tt, LANE), _F32)
            p_hi = jnp.zeros((tt // 2, LANE), _F32)
            for s in range(SUB):
                if s < half:
                    kb = jnp.broadcast_to(gk3[:, s:s + 1, :], (nsub, SUB, gw)).reshape(tt, gw)
                    bb = jnp.broadcast_to(bc3[:, s:s + 1, :], (nsub, SUB, gw)).reshape(tt, gw)
                    y = gq * kb * jnp.exp2(jnp.minimum(bc2 - bb, 0.0))
                    p_acc = p_acc + _dot(y, gsel[s])
                else:
                    kb = jnp.broadcast_to(gk3[:, s:s + 1, :], (nsub, half, gw)).reshape(tt // 2, gw)
                    bb = jnp.broadcast_to(bc3[:, s:s + 1, :], (nsub, half, gw)).reshape(tt // 2, gw)
                    y = gq_hi * kb * jnp.exp2(jnp.minimum(bc2_hi - bb, 0.0))
                    p_hi = p_hi + _dot(y, gsel[s])
                yield
            p4 = p_acc.reshape(nsub, 2, half, LANE)
            p_acc = jnp.concatenate([p4[:, 0:1], p4[:, 1:2] + p_hi.reshape(nsub, 1, half, LANE)],
                                    axis=1).reshape(tt, LANE)
            p_bf = p_acc.astype(_BF)
            r2 = lax.broadcasted_iota(jnp.int32, (WIN, WIN), 0)
            c2 = lax.broadcasted_iota(jnp.int32, (WIN, WIN), 1)
            sub_mask = (c2 <= r2) & ((c2 // SUB) == (r2 // SUB))
            for h in range(G_HEADS):
                outs = []
                for w in range(tt // WIN):
                    rs = slice(w * WIN, (w + 1) * WIN)
                    a_full = jnp.dot(p_bf[rs], rsel[h], preferred_element_type=_F32)
                    a_h = jnp.where(sub_mask, a_full, 0.0)
                    outs.append(_dot(a_h, gv[rs, h * G_DV:(h + 1) * G_DV]))
                o_diag[h] = outs[0] if len(outs) == 1 else jnp.concatenate(outs, axis=0)
                yield

        head_of_lane = lax.broadcasted_iota(jnp.int32, (SUB, gw), 1) // G_DK
        o_int = [[None] * nsub for _ in range(G_HEADS)]

        def gla_rec():
            st = s_st[...]
            for j in range(nsub):
                js = slice(j * SUB, (j + 1) * SUB)
                qj, kj = qh[js], kh[js]
                qq = jnp.concatenate([jnp.where(head_of_lane == h, qj, 0.0) for h in range(G_HEADS)], axis=0)
                kk = jnp.concatenate([jnp.where(head_of_lane == h, kj, 0.0) for h in range(G_HEADS)], axis=0)
                vv = jnp.concatenate([gv[js, h * G_DV:(h + 1) * G_DV] for h in range(G_HEADS)], axis=0)
                oj = _dot_nt(qq, st)
                for h in range(G_HEADS):
                    o_int[h][j] = oj[h * SUB:(h + 1) * SUB]
                st = eg[j * SUB:j * SUB + 1, :] * st + _dot_tn(vv, kk)
                yield
            s_st[...] = st

        def cast_weights():
            for src_ref, dst_ref in zip(cast_in, cast_out):
                dst_ref[...] = src_ref[0].astype(_BF)
                yield

        _round_robin([in_proj(), gla_diag(), gla_rec()] + [mlstm_head(h) for h in range(M_HEADS)]
                     + [cast_weights()])

        gn = g_norm_g[...]
        g_outs = []
        for h in range(G_HEADS):
            hs = slice(h * G_DV, (h + 1) * G_DV)
            og = o_diag[h] + jnp.concatenate(o_int[h], axis=0)
            rms = lax.rsqrt(jnp.mean(og * og, axis=1, keepdims=True) + LN_EPS)
            gate = gg[:, hs]
            g_outs.append(gate * _sigmoid(gate) * (og * rms * gn[:, hs]))

        y = jnp.concatenate(m_outs + g_outs, axis=1)
        mix = _dot(y, w_out[...])
        o_ref[0] = _layer_norm(ALPHA * h0 + mix, ln1_g[...], ln1_b[...])

    for par in range(2):
        pl.when(g % 2 == par)(functools.partial(step, par))


def _mixer_constants(tt):
    r = np.arange(tt)
    tri64 = ((r[None, :] <= r[:, None]) & (r[None, :] // CHUNK == r[:, None] // CHUNK))
    same16 = (r[None, :] // SUB == r[:, None] // SUB)
    tri16 = (r[None, :] <= r[:, None]) & same16
    gw = G_HEADS * G_DK
    gsel = np.zeros((SUB, gw, LANE), np.float32)
    for s in range(SUB):
        for h in range(G_HEADS):
            gsel[s, h * G_DK:(h + 1) * G_DK, h * SUB + s] = 1.0
    rsel = np.zeros((G_HEADS, LANE, WIN), np.float32)
    for h in range(G_HEADS):
        for s in range(SUB):
            rsel[h, h * SUB + s, s::SUB] = 1.0
    to_bf = lambda a: jnp.asarray(a.astype(np.float32), dtype=_BF)
    return to_bf(tri64), to_bf(tri16), to_bf(same16), to_bf(gsel), to_bf(rsel)


def _full(shape):
    nd = len(shape)
    return pl.BlockSpec(shape, lambda b, i, _nd=nd: (0,) * _nd)


def _mixer_call(x, lnin_g, lnin_b, w_m, w_g, w_gate, conv_w, conv_b, gate_bias, m_norm_g, glw, glb, g_norm_g,
                w_out, ln1_g, ln1_b, tt, to_cast):
    bsz, t, d = x.shape
    consts = _mixer_constants(tt)
    params = (lnin_g, lnin_b, w_m, w_g, w_gate, conv_w, conv_b, gate_bias, m_norm_g, glw, glb, g_norm_g,
              w_out, ln1_g, ln1_b) + consts
    nt = t // tt
    ntiles = bsz * nt
    in_tile = pl.BlockSpec((1, tt, d), lambda g: (jnp.minimum(g, ntiles - 1) // nt, jnp.minimum(g, ntiles - 1) % nt, 0))
    out_tile = pl.BlockSpec((1, tt, d), lambda g: (jnp.maximum(g - 1, 0) // nt, jnp.maximum(g - 1, 0) % nt, 0))
    full = lambda shape: pl.BlockSpec(shape, lambda g, _nd=len(shape): (0,) * _nd)
    cast_in = [pl.BlockSpec((1, w.shape[1] // ntiles, w.shape[2]), lambda g: (0, jnp.minimum(g, ntiles - 1), 0))
               for w in to_cast]
    cast_out = [pl.BlockSpec((w.shape[1] // ntiles, w.shape[2]), lambda g: (jnp.minimum(g, ntiles - 1), 0))
                for w in to_cast]
    assert all(w.shape[1] % (16 * ntiles) == 0 for w in to_cast)
    return pl.pallas_call(
        functools.partial(_mixer_kernel, tt=tt, nt=nt, n_cast=len(to_cast)),
        out_shape=[jax.ShapeDtypeStruct((bsz, t, d), _F32)]
        + [jax.ShapeDtypeStruct(w.shape[1:], _BF) for w in to_cast],
        grid=(ntiles + 1,),
        in_specs=[in_tile] + [full(p.shape) for p in params] + cast_in,
        out_specs=[out_tile] + cast_out,
        scratch_shapes=[
            pltpu.VMEM((tt + 8, 2 * M_QK), _F32),
            pltpu.VMEM((tt + 8, 2 * M_QK), _F32),
            pltpu.VMEM((M_HEADS, M_DK, M_DV), _F32),
            pltpu.VMEM((M_HEADS, 1, M_DK), _F32),
            pltpu.VMEM((M_HEADS, 1, 1), _F32),
            pltpu.VMEM((G_DV, G_HEADS * G_DK), _F32),
            pltpu.VMEM((2, tt, IN_COLS_R), _F32),
            pltpu.VMEM((2, tt, D_MODEL), _F32),
        ],
        compiler_params=pltpu.CompilerParams(
            dimension_semantics=("arbitrary",), vmem_limit_bytes=VMEM_LIMIT),
        name="mixer",
    )(x, *params, *to_cast)


def _memkv_kernel(mem_ref, wk_ref, wv_ref, wq_ref, wo_ref, wqk_ref, vo_ref):
    bsz, nm, d = mem_ref.shape
    m = mem_ref[...].reshape(bsz * nm, d)
    k = _dot(m, wk_ref[...])
    v = _dot(m, wv_ref[...])
    for b in range(bsz):
        rs = slice(b * nm, (b + 1) * nm)
        for h in range(X_HEADS):
            hs = slice(h * X_DH, (h + 1) * X_DH)
            ms = slice(h * N_MEM, (h + 1) * N_MEM)
            wqk_ref[b, :, ms] = (_dot_nt(wq_ref[:, hs], k[rs, hs]) * (X_DH ** -0.5)).astype(_BF)
            vo_ref[b, ms, :] = _dot(v[rs, hs], wo_ref[hs, :]).astype(_BF)


def _memkv_call(mem, wk, wv, wq, wo):
    bsz, nm, d = mem.shape
    blk = pl.BlockSpec((bsz, nm, d), lambda b: (0, 0, 0))
    wspec = pl.BlockSpec((d, d), lambda b: (0, 0))
    return pl.pallas_call(
        _memkv_kernel,
        out_shape=(jax.ShapeDtypeStruct((bsz, d, X_HEADS * nm), _BF),
                   jax.ShapeDtypeStruct((bsz, X_HEADS * nm, d), _BF)),
        grid=(1,),
        in_specs=[blk, wspec, wspec, wspec, wspec],
        out_specs=(pl.BlockSpec((bsz, d, X_HEADS * nm), lambda b: (0, 0, 0)),
                   pl.BlockSpec((bsz, X_HEADS * nm, d), lambda b: (0, 0, 0))),
        compiler_params=pltpu.CompilerParams(
            dimension_semantics=("arbitrary",), vmem_limit_bytes=VMEM_LIMIT),
        name="memkv",
    )(mem, wk, wv, wq, wo)


def _attnmlp_kernel(h_ref, wqk_ref, vo_ref, ln2_g, ln2_b, w1, w2, ln3_g, ln3_b, o_ref):
    rows = h_ref.shape[1] // ROW_SPLIT

    def chain(r, wait):
        for _ in range(wait):
            yield
        rs = slice(r * rows, (r + 1) * rows)
        h1 = h_ref[0, rs, :]
        s_all = _dot(h1, wqk_ref[0])
        yield
        probs = []
        for h in range(X_HEADS):
            s = s_all[:, h * N_MEM:(h + 1) * N_MEM]
            e = jnp.exp(s - jnp.max(s, axis=1, keepdims=True))
            probs.append((e * (1.0 / jnp.sum(e, axis=1, keepdims=True))).astype(_BF))
            yield
        xa = jnp.dot(jnp.concatenate(probs, axis=1), vo_ref[0], preferred_element_type=_F32)
        yield
        h2 = _layer_norm(ALPHA * h1 + xa, ln2_g[...], ln2_b[...])
        h2b = h2.astype(_BF)
        yield
        ff = jnp.zeros(h2.shape, _F32)
        for f in range(D_FF // FF_BLK):
            fs = slice(f * FF_BLK, (f + 1) * FF_BLK)
            hid = jnp.maximum(jnp.dot(h2b, w1[:, fs], preferred_element_type=_F32), 0.0)
            ff = ff + _dot(hid * hid, w2[fs, :])
            yield
        o_ref[0, rs, :] = _layer_norm(ALPHA * h2 + ff, ln3_g[...], ln3_b[...])

    _round_robin([chain(r, r * ROW_DELAY) for r in range(ROW_SPLIT)])


def _attnmlp_call(h1, wqk, vo, ln2_g, ln2_b, w1, w2, ln3_g, ln3_b, tm):
    bsz, t, d = h1.shape
    tile = pl.BlockSpec((1, tm, d), lambda b, i: (b, i, 0))
    per_batch = lambda a: pl.BlockSpec((1,) + a.shape[1:], lambda b, i: (b, 0, 0))
    params = (ln2_g, ln2_b, w1, w2, ln3_g, ln3_b)
    return pl.pallas_call(
        _attnmlp_kernel,
        out_shape=jax.ShapeDtypeStruct((bsz, t, d), _F32),
        grid=(bsz, t // tm),
        in_specs=[tile, per_batch(wqk), per_batch(vo)] + [_full(p.shape) for p in params],
        out_specs=tile,
        compiler_params=pltpu.CompilerParams(
            dimension_semantics=("arbitrary", "arbitrary"), vmem_limit_bytes=VMEM_LIMIT),
        name="attnmlp",
    )(h1, wqk, vo, *params)


def _time_tile(t, target):
    tt = min(t, target)
    assert t % tt == 0 and tt % WIN == 0, (t, tt)
    return tt


def kernel(x, mem, ln_in_g, ln_in_b, w_in, conv_w, conv_b, m_i_bias, m_f_bias, m_norm_g, g_lr_w, g_lr_b, g_norm_g, w_out, ln1_g, ln1_b, x_wq, x_wk, x_wv, x_wo, ln2_g, ln2_b, w_ff1, w_ff2, ln3_g, ln3_b):
    assert w_in.shape[0] == DEPTH == 1
    row = lambda a: a.reshape(1, -1).astype(_F32)
    wi = w_in[0]
    w_m = wi[:, O_MQ:O_MI].astype(_BF)
    w_g = wi[:, O_GQ:O_GLR].astype(_BF)
    w_gate = jnp.concatenate([wi[:, O_MI:O_GQ], wi[:, O_GLR:O_GLR + G_RANK],
                              jnp.zeros((D_MODEL, LANE - 2 * M_HEADS - G_RANK), wi.dtype)], axis=1).astype(_BF)
    gate_bias = jnp.concatenate([m_i_bias[0], m_f_bias[0],
                                 jnp.zeros((LANE - 2 * M_HEADS,), _F32)]).reshape(1, LANE)
    glw = jnp.zeros((LANE, G_HEADS * G_DK), _F32).at[2 * M_HEADS:2 * M_HEADS + G_RANK].set(g_lr_w[0]).astype(_BF)
    tt = _time_tile(x.shape[1], MIXER_ROWS)
    h1, wk_b, wv_b, wq_b, wo_b, w1_b, w2_b = _mixer_call(
        x, row(ln_in_g), row(ln_in_b), w_m, w_g, w_gate, conv_w[0].astype(_F32), row(conv_b[0]),
        gate_bias, row(m_norm_g[0]), glw, row(g_lr_b[0]), row(g_norm_g[0]), w_out[0].astype(_BF),
        row(ln1_g[0]), row(ln1_b[0]), tt, (x_wk, x_wv, x_wq, x_wo, w_ff1, w_ff2))
    wqk, vo = _memkv_call(mem, wk_b, wv_b, wq_b, wo_b)
    tm = _time_tile(x.shape[1], ATTN_ROWS)
    return _attnmlp_call(h1, wqk, vo, row(ln2_g[0]), row(ln2_b[0]), w1_b, w2_b, row(ln3_g[0]), row(ln3_b[0]), tm)
```
